```python
import math
import jax, jax.numpy as jnp
from jax import lax
import numpy as np

D_MODEL = 1024
BATCH = 2
SEQ = 16384
DEPTH = 1
DEC_BATCH = 128
DEC_SEQ = 8
PAST_LEN = 8192
PAGE_SIZE = 128

D_MIX = D_MODEL
D_A = D_MIX // 2
D_B = D_MIX - D_A
CHUNK = 128
N_A_GROUPS = 4
A_GROUP_DIM = D_A // N_A_GROUPS
HEAD_DIM = 64
N_HEADS = D_B // HEAD_DIM
N_KV_HEADS = 2
GQA = N_HEADS // N_KV_HEADS
KV_W = N_KV_HEADS * HEAD_DIM
ROT_DIM = HEAD_DIM // 4
N_IDX_HEADS = 8
D_IDX = 64
IDX_ROT_DIM = D_IDX // 4
TOP_K_MAX = 256
ROPE_THETA = 500000.0
EPS = 1e-6
Q_BLOCK = 128
D_IN_PROJ = 3 * D_A + 2 * D_B + 2 * KV_W + N_IDX_HEADS * D_IDX + D_IDX + N_IDX_HEADS

kernel_name = 'hybrid_gmlp_dsa_decode_step'


def rms_norm(x, g):
    x32 = x.astype(jnp.float32)
    y = x32 * lax.rsqrt(jnp.mean(x32 * x32, axis=-1, keepdims=True) + EPS)
    return (y * g.astype(jnp.float32)).astype(x.dtype)


def layer_norm(x, g, b):
    x32 = x.astype(jnp.float32)
    mu = jnp.mean(x32, axis=-1, keepdims=True)
    xc = x32 - mu
    y = xc * lax.rsqrt(jnp.mean(xc * xc, axis=-1, keepdims=True) + EPS)
    return (y * g.astype(jnp.float32) + b.astype(jnp.float32)).astype(x.dtype)


def rope(x, pos, rot_dim):
    half = rot_dim // 2
    inv = jnp.power(jnp.float32(ROPE_THETA), -jnp.arange(half, dtype=jnp.float32) * 2.0 / rot_dim)
    ang = pos.astype(jnp.float32)[:, None] * inv[None, :]
    cos = jnp.cos(ang)[:, None, :].astype(x.dtype)
    sin = jnp.sin(ang)[:, None, :].astype(x.dtype)
    x1 = x[..., :half]
    x2 = x[..., half:rot_dim]
    return jnp.concatenate([x1 * cos - x2 * sin, x2 * cos + x1 * sin, x[..., rot_dim:]], axis=-1)


def in_project(x, pos, g_pre, w_in):
    B, T, _ = x.shape
    h = rms_norm(x, g_pre)
    p = h @ w_in
    sizes = [D_A, D_A, D_A, D_B, KV_W, KV_W, D_B, N_IDX_HEADS * D_IDX, D_IDX, N_IDX_HEADS]
    cuts = []
    acc = 0
    for s in sizes[:-1]:
        acc += s
        cuts.append(acc)
    u, v, z_a, q, k, vv, z_b, qi, ki, wi = jnp.split(p, cuts, axis=-1)
    q = rope(q.reshape(B, T, N_HEADS, HEAD_DIM), pos, ROT_DIM)
    k = rope(k.reshape(B, T, N_KV_HEADS, HEAD_DIM), pos, ROT_DIM)
    vv = vv.reshape(B, T, N_KV_HEADS, HEAD_DIM)
    qi = rope(qi.reshape(B, T, N_IDX_HEADS, D_IDX), pos, IDX_ROT_DIM)
    ki = rope(ki[:, :, None, :], pos, IDX_ROT_DIM)[:, :, 0]
    wi = wi * (N_IDX_HEADS ** -0.5)
    return u, v, z_a, z_b, q, k, vv, qi, ki, wi


def gmlp_mix(u, v, ln_g, ln_b, w_s, b_s):
    B, T, _ = v.shape
    vn = layer_norm(v, ln_g, ln_b)
    n = -(-T // CHUNK)
    vp = jnp.pad(vn, ((0, 0), (0, n * CHUNK - T), (0, 0))).reshape(B, n, CHUNK, N_A_GROUPS, A_GROUP_DIM)
    mask = jnp.tril(jnp.ones((CHUNK, CHUNK), dtype=bool))
    ws = jnp.where(mask[None], w_s, jnp.zeros_like(w_s))
    mixed = jnp.einsum('gts,bnsgc->bntgc', ws, vp) + b_s.T[None, None, :, :, None]
    mixed = mixed.reshape(B, n * CHUNK, D_A)[:, :T]
    start = ((T - 1) // CHUNK) * CHUNK
    return u * mixed, vn[:, start:]


def indexer_scores(qi, wi, ki, q_pos, k_pos):
    s = jnp.einsum('bthd,bsd->bths', qi, ki)
    s = jnp.einsum('bths,bth->bts', jax.nn.relu(s), wi).astype(jnp.float32)
    mask = k_pos[None, None, :] <= q_pos[None, :, None]
    return jnp.where(mask, s, -jnp.inf)


def select_top(scores, k_sel):
    vals, idx = lax.top_k(scores, k_sel)
    return idx, jnp.isfinite(vals)


def sparse_attend(q, k_sel, v_sel, valid):
    B, T = q.shape[:2]
    qg = q.reshape(B, T, N_KV_HEADS, GQA, HEAD_DIM)
    logits = jnp.einsum('btngd,btknd->btngk', qg, k_sel).astype(jnp.float32) * (HEAD_DIM ** -0.5)
    logits = jnp.where(valid[:, :, None, None, :], logits, -jnp.inf)
    p = jax.nn.softmax(logits, axis=-1).astype(q.dtype)
    o = jnp.einsum('btngk,btknd->btngd', p, v_sel)
    return o.reshape(B, T, D_B)


def out_project(x, y_a, z_a, y_b, z_b, w_out, g_post):
    y = jnp.concatenate([y_a * jax.nn.silu(z_a), y_b * jax.nn.silu(z_b)], axis=-1) @ w_out
    return x + rms_norm(y, g_post)


def gather_paged(pool, page_table, new_rows, idx, past):
    b = jnp.arange(idx.shape[0])[:, None, None]
    ip = jnp.minimum(idx, past - 1)
    phys = page_table[b, ip // PAGE_SIZE]
    from_pool = pool[phys, ip % PAGE_SIZE]
    inew = jnp.clip(idx - past, 0, new_rows.shape[1] - 1)
    from_new = new_rows[b, inew]
    return jnp.where((idx < past)[..., None, None], from_pool, from_new)


def prompt_layer(x, g_pre, w_in, ln_v_g, ln_v_b, w_s, b_s, w_out, g_post):
    B, T, _ = x.shape
    pos = jnp.arange(T, dtype=jnp.int32)
    u, v, z_a, z_b, q, k, vv, qi, ki, wi = in_project(x, pos, g_pre, w_in)
    y_a, v_state = gmlp_mix(u, v, ln_v_g, ln_v_b, w_s, b_s)
    k_sel_n = min(TOP_K_MAX, T // 4)
    n_blk = T // Q_BLOCK

    def block(args):
        qb, qib, wib, pb = args
        sc = indexer_scores(qib, wib, ki, pb, pos)
        idx, valid = select_top(sc, k_sel_n)
        ks = jax.vmap(lambda a, i: a[i])(k, idx)
        vs = jax.vmap(lambda a, i: a[i])(vv, idx)
        return sparse_attend(qb, ks, vs, valid)

    def to_blocks(a):
        return a.reshape((B, n_blk, Q_BLOCK) + a.shape[2:]).swapaxes(0, 1)

    ob = lax.map(block, (to_blocks(q), to_blocks(qi), to_blocks(wi), pos.reshape(n_blk, Q_BLOCK)))
    y_b = ob.swapaxes(0, 1).reshape(B, T, D_B)
    y = out_project(x, y_a, z_a, y_b, z_b, w_out, g_post)
    return y, k, vv, ki, v_state


def sample_layer(x, cache_k, cache_v, cache_idx_k, page_table, g_pre, w_in, ln_v_g, ln_v_b, w_s, b_s, w_out, g_post):
    B, T, _ = x.shape
    past = page_table.shape[1] * PAGE_SIZE
    pos = past + jnp.arange(T, dtype=jnp.int32)
    u, v, z_a, z_b, q, k, vv, qi, ki, wi = in_project(x, pos, g_pre, w_in)
    y_a, v_state = gmlp_mix(u, v, ln_v_g, ln_v_b, w_s, b_s)
    k_pos = jnp.arange(past + T, dtype=jnp.int32)
    ki_past = cache_idx_k[page_table].reshape(B, past, D_IDX)
    ki_all = jnp.concatenate([ki_past, ki], axis=1)
    sc = indexer_scores(qi, wi, ki_all, pos, k_pos)
    idx, valid = select_top(sc, min(TOP_K_MAX, (past + T) // 4))
    ks = gather_paged(cache_k, page_table, k, idx, past)
    vs = gather_paged(cache_v, page_table, vv, idx, past)
    y_b = sparse_attend(q, ks, vs, valid)
    y = out_project(x, y_a, z_a, y_b, z_b, w_out, g_post)
    return y, k, vv, ki, v_state


def setup_inputs(seed: int = 0) -> dict:
    key = jax.random.key(seed)
    ks = jax.random.split(key, 16)
    n_pages = PAST_LEN // PAGE_SIZE
    n_pool = (DEC_BATCH * n_pages * 5) // 4
    x_prompt = jax.random.normal(ks[0], (BATCH, SEQ, D_MODEL), jnp.float32)
    x_sample = jax.random.normal(ks[1], (DEC_BATCH, DEC_SEQ, D_MODEL), jnp.float32)
    cache_k = jax.random.normal(ks[2], (DEPTH, n_pool, PAGE_SIZE, N_KV_HEADS, HEAD_DIM), jnp.float32)
    cache_v = jax.random.normal(ks[3], (DEPTH, n_pool, PAGE_SIZE, N_KV_HEADS, HEAD_DIM), jnp.float32)
    cache_idx_k = jax.random.normal(ks[4], (DEPTH, n_pool, PAGE_SIZE, D_IDX), jnp.float32)
    page_table = jax.random.permutation(ks[5], n_pool)[:DEC_BATCH * n_pages].reshape(DEC_BATCH, n_pages).astype(jnp.int32)
    g_pre = 1.0 + 0.01 * jax.random.normal(ks[6], (DEPTH, D_MODEL), jnp.float32)
    w_in = jax.random.normal(ks[7], (DEPTH, D_MODEL, D_IN_PROJ), jnp.float32) * D_MODEL ** -0.5
    ln_v_g = 1.0 + 0.01 * jax.random.normal(ks[8], (DEPTH, D_A), jnp.float32)
    ln_v_b = 0.01 * jax.random.normal(ks[9], (DEPTH, D_A), jnp.float32)
    w_s = jax.random.normal(ks[10], (DEPTH, N_A_GROUPS, CHUNK, CHUNK), jnp.float32) * CHUNK ** -0.5
    b_s = 1.0 + 0.01 * jax.random.normal(ks[11], (DEPTH, N_A_GROUPS, CHUNK), jnp.float32)
    w_out = jax.random.normal(ks[12], (DEPTH, D_MIX, D_MODEL), jnp.float32) * D_MIX ** -0.5
    g_post = 1.0 + 0.01 * jax.random.normal(ks[13], (DEPTH, D_MODEL), jnp.float32)
    return {'x_prompt': x_prompt, 'x_sample': x_sample, 'cache_k': cache_k, 'cache_v': cache_v,
            'cache_idx_k': cache_idx_k, 'page_table': page_table, 'g_pre': g_pre, 'w_in': w_in,
            'ln_v_g': ln_v_g, 'ln_v_b': ln_v_b, 'w_s': w_s, 'b_s': b_s, 'w_out': w_out, 'g_post': g_post}


def reference(x_prompt, x_sample, cache_k, cache_v, cache_idx_k, page_table, g_pre, w_in, ln_v_g, ln_v_b, w_s, b_s, w_out, g_post):
    xp = x_prompt
    xs = x_sample
    kp_l, vp_l, ikp_l, sp_l = [], [], [], []
    ks_l, vs_l, iks_l, ss_l = [], [], [], []
    for l in range(DEPTH):
        xp, kp, vp, ikp, sp = prompt_layer(xp, g_pre[l], w_in[l], ln_v_g[l], ln_v_b[l], w_s[l], b_s[l], w_out[l], g_post[l])
        xs, kn, vn, ikn, sn = sample_layer(xs, cache_k[l], cache_v[l], cache_idx_k[l], page_table, g_pre[l], w_in[l],
                                           ln_v_g[l], ln_v_b[l], w_s[l], b_s[l], w_out[l], g_post[l])
        kp_l.append(kp); vp_l.append(vp); ikp_l.append(ikp); sp_l.append(sp)
        ks_l.append(kn); vs_l.append(vn); iks_l.append(ikn); ss_l.append(sn)
    new_k_prompt = jnp.stack(kp_l, 0)
    new_v_prompt = jnp.stack(vp_l, 0)
    new_idx_k_prompt = jnp.stack(ikp_l, 0)
    state_gmlp_v_prompt = jnp.stack(sp_l, 0)
    new_k_sample = jnp.stack(ks_l, 0)
    new_v_sample = jnp.stack(vs_l, 0)
    new_idx_k_sample = jnp.stack(iks_l, 0)
    state_gmlp_v_sample = jnp.stack(ss_l, 0)
    return (xp, xs, new_k_prompt, new_v_prompt, new_idx_k_prompt, state_gmlp_v_prompt,
            new_k_sample, new_v_sample, new_idx_k_sample, state_gmlp_v_sample)
```

```python
import functools

import numpy as np
import jax
import jax.numpy as jnp
from jax import lax
from jax.experimental import pallas as pl
from jax.experimental.pallas import tpu as pltpu

_F32 = jnp.float32
_MXU_DTYPE = jnp.bfloat16

D_MODEL = 1024
D_A = 512
D_B = 512
CHUNK = 128
N_A_GROUPS = 4
A_GROUP_DIM = D_A // N_A_GROUPS
HEAD_DIM = 64
N_HEADS = 8
N_KV_HEADS = 2
KV_W = N_KV_HEADS * HEAD_DIM
ROT_DIM = HEAD_DIM // 4
N_IDX_HEADS = 8
D_IDX = 64
TOP_K_MAX = 256
ROPE_THETA = 500000.0
EPS = 1e-6
PAGE_SIZE = 128

LANES = 128
VMEM_LIMIT_BYTES = 56 * 1024 * 1024

C_U, C_V, C_ZA, C_Q, C_K, C_VV, C_ZB, C_QI, C_KI, C_WI, C_END = (
    0, 512, 1024, 1536, 2048, 2176, 2304, 2816, 3328, 3456, 3584)

_FLT_MAX = float(np.finfo(np.float32).max)
_NEG_INF = float("-inf")
_INT_MIN = -2 ** 31


def _head_perm():
    c = np.arange(N_HEADS * HEAD_DIM)
    j = c // LANES
    half = (c % LANES) // HEAD_DIM
    d = c % HEAD_DIM
    return (j + 4 * half) * HEAD_DIM + d


def _rope_tables(pos):
    half = ROT_DIM // 2
    inv = jnp.power(jnp.float32(ROPE_THETA), -jnp.arange(half, dtype=_F32) * 2.0 / ROT_DIM)
    ang = pos.astype(_F32)[:, None] * inv[None, :]
    cos = jnp.cos(ang)
    sin = jnp.sin(ang)
    l64 = np.arange(LANES) % HEAD_DIM
    fidx = l64 % half
    in_rot = (l64 < ROT_DIM)[None, :]
    first = (l64 < half)[None, :]
    second = ((l64 >= half) & (l64 < ROT_DIM))[None, :]
    cos_t = jnp.where(in_rot, cos[:, fidx], 1.0)
    sin_a = jnp.where(first, -sin[:, fidx], 0.0)
    sin_b = jnp.where(second, sin[:, fidx], 0.0)
    return cos_t, sin_a, sin_b


def _silu(z):
    return z * (1.0 / (1.0 + jnp.exp(-z)))


def _inproj_kernel(x_ref, cos_ref, sa_ref, sb_ref, gpre_ref, win_ref, lng_ref, lnb_ref,
                   wmix_ref, bmix_ref,
                   yag_ref, zbs_ref, qs_ref, qi_ref, k32_ref, v32_ref, ki32_ref, wi_ref,
                   kbf_ref, vt_ref, kibf_ref, vstate_ref, *, tm, state_rows):
    x = x_ref[...]
    ms = jnp.mean(x * x, axis=-1, keepdims=True)
    h = (x * lax.rsqrt(ms + EPS) * gpre_ref[...]).astype(_MXU_DTYPE)

    def proj(a, b):
        return jnp.dot(h, win_ref[:, a:b], preferred_element_type=_F32)

    cos_t = cos_ref[...]
    sin_a = sa_ref[...]
    sin_b = sb_ref[...]

    def rope(p):
        blocks = []
        for j in range(p.shape[1] // LANES):
            pj = p[:, j * LANES:(j + 1) * LANES]
            blocks.append(pj * cos_t
                          + pltpu.roll(pj, LANES - ROT_DIM // 2, 1) * sin_a
                          + pltpu.roll(pj, ROT_DIM // 2, 1) * sin_b)
        return blocks[0] if len(blocks) == 1 else jnp.concatenate(blocks, axis=1)

    u = proj(C_U, C_V)
    v = proj(C_V, C_ZA)
    mu = jnp.mean(v, axis=-1, keepdims=True)
    vc = v - mu
    var = jnp.mean(vc * vc, axis=-1, keepdims=True)
    vn = vc * lax.rsqrt(var + EPS) * lng_ref[...] + lnb_ref[...]
    vstate_ref[...] = vn[tm - state_rows:, :].reshape(vstate_ref.shape)
    vnb = vn.astype(_MXU_DTYPE)
    bmix = bmix_ref[...]
    rows = []
    for c in range(tm // CHUNK):
        cols = []
        for g in range(N_A_GROUPS):
            blk = vnb[c * CHUNK:(c + 1) * CHUNK, g * A_GROUP_DIM:(g + 1) * A_GROUP_DIM]
            cols.append(jnp.dot(wmix_ref[g], blk, preferred_element_type=_F32)
                        + bmix[:, g * A_GROUP_DIM:(g + 1) * A_GROUP_DIM])
        rows.append(jnp.concatenate(cols, axis=1))
    mixed = jnp.concatenate(rows, axis=0)
    za = proj(C_ZA, C_Q)
    yag_ref[...] = ((u * mixed) * _silu(za)).astype(yag_ref.dtype)

    q = rope(proj(C_Q, C_K))
    qs_ref[...] = (q * (HEAD_DIM ** -0.5)).astype(qs_ref.dtype)
    k = rope(proj(C_K, C_VV))
    k32_ref[...] = k
    kbf_ref[...] = k.astype(kbf_ref.dtype)
    vv = proj(C_VV, C_ZB)
    v32_ref[...] = vv
    vt_ref[0] = vv.T.astype(vt_ref.dtype)
    zb = proj(C_ZB, C_QI)
    zbs_ref[...] = _silu(zb)
    qi = rope(proj(C_QI, C_KI))
    qi_ref[...] = qi.astype(qi_ref.dtype)
    ki = rope(proj(C_KI, C_WI))
    ki32_ref[...] = ki[:, :D_IDX]
    kibf_ref[...] = ki.astype(kibf_ref.dtype)
    wi_ref[...] = proj(C_WI, C_END) * (N_IDX_HEADS ** -0.5)


def _inproj(x2d, tables, table_index, gpre, w_pad, lng, lnb, wmix, bmix, *, tm, state_rows,
            state_shape, state_index):
    r = x2d.shape[0]
    nt = r // tm
    cos_t, sin_a, sin_b = tables
    row = lambda w: pl.BlockSpec((tm, w), lambda i: (i, 0))
    const = lambda shape: pl.BlockSpec(shape, lambda i: (0,) * len(shape))
    tab = pl.BlockSpec((tm, LANES), table_index)
    out_shapes = (
        jax.ShapeDtypeStruct((r, D_A), _MXU_DTYPE),
        jax.ShapeDtypeStruct((r, D_B), _F32),
        jax.ShapeDtypeStruct((r, D_B), _MXU_DTYPE),
        jax.ShapeDtypeStruct((r, N_IDX_HEADS * D_IDX), _MXU_DTYPE),
        jax.ShapeDtypeStruct((r, KV_W), _F32),
        jax.ShapeDtypeStruct((r, KV_W), _F32),
        jax.ShapeDtypeStruct((r, D_IDX), _F32),
        jax.ShapeDtypeStruct((r, LANES), _F32),
        jax.ShapeDtypeStruct((r, KV_W), _MXU_DTYPE),
        jax.ShapeDtypeStruct((nt, KV_W, tm), _MXU_DTYPE),
        jax.ShapeDtypeStruct((r, 2 * D_IDX), _MXU_DTYPE),
        jax.ShapeDtypeStruct(state_shape, _F32),
    )
    out_specs = (
        row(D_A), row(D_B), row(D_B), row(N_IDX_HEADS * D_IDX), row(KV_W), row(KV_W),
        row(D_IDX), row(LANES), row(KV_W),
        pl.BlockSpec((1, KV_W, tm), lambda i: (i, 0, 0)),
        row(2 * D_IDX),
        pl.BlockSpec((1,) * (len(state_shape) - 2) + (state_rows, D_A), state_index),
    )
    return pl.pallas_call(
        functools.partial(_inproj_kernel, tm=tm, state_rows=state_rows),
        grid=(nt,),
        in_specs=[row(D_MODEL), tab, tab, tab, const((1, D_MODEL)), const((D_MODEL, C_END)),
                  const((1, D_A)), const((1, D_A)), const((N_A_GROUPS, CHUNK, CHUNK)),
                  const((CHUNK, D_A))],
        out_specs=out_specs,
        out_shape=out_shapes,
        compiler_params=pltpu.CompilerParams(
            dimension_semantics=("arbitrary",), vmem_limit_bytes=VMEM_LIMIT_BYTES),
        name="inproj",
    )(x2d, cos_t, sin_a, sin_b, gpre, w_pad, lng, lnb, wmix, bmix)


def _ordered_bits_to_f32(u):
    key = u ^ jnp.int32(_INT_MIN)
    bits = jnp.where(key >= 0, key, key ^ jnp.int32(0x7FFFFFFF))
    return lax.bitcast_convert_type(bits, _F32)


def _topk_threshold(count_ge, vec_shape, total, k_sel):
    kf = jnp.float32(k_sel)

    def cond(carry):
        bit, _, _, done = carry
        return jnp.logical_and(bit >= 0, done < 0.5)

    def body(carry):
        bit, ucand, clo, _ = carry
        trial = ucand | lax.shift_left(jnp.int32(1), bit)
        cnt = count_ge(_ordered_bits_to_f32(trial))
        ok = cnt >= kf
        ucand = jnp.where(ok, trial, ucand)
        clo = jnp.where(ok, cnt, clo)
        done = jnp.min(jnp.where(clo == kf, 1.0, 0.0))
        return bit - 1, ucand, clo, done

    init = (jnp.int32(31), jnp.zeros(vec_shape, jnp.int32),
            jnp.full(vec_shape, total, _F32), jnp.float32(0.0))
    _, ucand, clo, _ = lax.while_loop(cond, body, init)
    below_finite = jnp.logical_and(ucand >= 0, ucand < 0x00800000)
    thr = jnp.where(below_finite, -_FLT_MAX, _ordered_bits_to_f32(ucand))
    return thr, clo


def _tie_cutoff(count_tie_le, need, want, total, n_steps):
    def body(_, carry):
        lo, hi = carry
        mid = lo + lax.shift_right_arithmetic(hi - lo, 1)
        ok = count_tie_le(mid) >= want
        return jnp.where(ok, lo, mid), jnp.where(ok, mid, hi)

    lo0 = jnp.full(need.shape, -1, jnp.int32)
    hi0 = jnp.full(need.shape, total - 1, jnp.int32)
    _, hi = lax.fori_loop(0, n_steps, body, (lo0, hi0))
    return jnp.where(need, hi, total)


def _attn_prompt_kernel(qs_ref, qi_ref, wi_ref, zbs_ref, ki_ref, k_ref, vt_ref, out_ref,
                        s_ref, q8_ref, qi8_ref, acc_ref, m_ref, l_ref, *, tq, tk, k_sel,
                        tie_steps):
    i = pl.program_id(1)
    ratio = tk // tq
    nchunks = lax.div(i + ratio, ratio)
    total = nchunks * tk
    kf = jnp.float32(k_sel)

    lane = lax.broadcasted_iota(jnp.int32, (tq, LANES), 1)
    lo_half = lane < HEAD_DIM
    for j in range(N_HEADS // 2):
        for src, dst in ((qs_ref, q8_ref), (qi_ref, qi8_ref)):
            blk = src[:, j * LANES:(j + 1) * LANES].astype(_F32)
            dst[(2 * j) * tq:(2 * j + 1) * tq, :] = jnp.where(lo_half, blk, 0.0).astype(dst.dtype)
            dst[(2 * j + 1) * tq:(2 * j + 2) * tq, :] = jnp.where(lo_half, 0.0, blk).astype(dst.dtype)
    w_t = wi_ref[...].T

    row = lax.broadcasted_iota(jnp.int32, (tk, tq), 0)
    col = lax.broadcasted_iota(jnp.int32, (tk, tq), 1)
    key_minus_query = row - col
    nt_dims = (((1,), (1,)), ((), ()))

    def scores(c, carry):
        k0 = pl.multiple_of(c * tk, tk)
        rt = lax.dot_general(ki_ref[pl.ds(k0, tk), :], qi8_ref[...], nt_dims,
                             preferred_element_type=_F32)
        s = None
        for hb in range(N_IDX_HEADS):
            t = jnp.maximum(rt[:, hb * tq:(hb + 1) * tq], 0.0) * w_t[hb:hb + 1, :]
            s = t if s is None else s + t
        s_ref[c] = jnp.where(key_minus_query <= i * tq - c * tk, s, _NEG_INF)
        return carry

    lax.fori_loop(0, nchunks, scores, 0)

    def count(pred):
        def body(c, acc):
            x = jnp.where(pred(s_ref[c], c), 1.0, 0.0)
            return acc + x.reshape(tk // 32, 32, tq).sum(axis=0)
        acc = lax.fori_loop(0, nchunks, body, jnp.zeros((32, tq), _F32))
        return jnp.sum(acc, axis=0, keepdims=True)

    thr, clo = _topk_threshold(lambda t: count(lambda s, c: s >= t), (1, tq),
                               total.astype(_F32), k_sel)

    need = jnp.logical_and(clo > kf, thr > -_FLT_MAX)

    @pl.when(jnp.max(jnp.where(need, 1.0, 0.0)) > 0.5)
    def _():
        want = kf - count(lambda s, c: s > thr)
        cutoff = _tie_cutoff(
            lambda jv: count(lambda s, c: jnp.where(s == thr, row + c * tk, total) <= jv),
            need, want, total, tie_steps)

        def drop(c, carry):
            s = s_ref[c]
            s_ref[c] = jnp.where(jnp.where(s == thr, row + c * tk, -1) > cutoff, _NEG_INF, s)
            return carry

        lax.fori_loop(0, nchunks, drop, 0)

    m_ref[...] = jnp.full(m_ref.shape, -_FLT_MAX, _F32)
    l_ref[...] = jnp.zeros(l_ref.shape, _F32)
    acc_ref[...] = jnp.zeros(acc_ref.shape, _F32)

    def attend(c, carry):
        k0 = pl.multiple_of(c * tk, tk)
        lt = lax.dot_general(k_ref[pl.ds(k0, tk), :], q8_ref[...], nt_dims,
                             preferred_element_type=_F32)
        sel = s_ref[c] >= thr
        parts = []
        alphas = []
        for hb in range(N_HEADS):
            lh = jnp.where(sel, lt[:, hb * tq:(hb + 1) * tq], _NEG_INF)
            m_old = m_ref[hb:hb + 1, :]
            m_new = jnp.maximum(m_old, jnp.max(lh, axis=0, keepdims=True))
            p = jnp.exp(lh - m_new)
            alpha = jnp.exp(m_old - m_new)
            l_ref[hb:hb + 1, :] = alpha * l_ref[hb:hb + 1, :] + jnp.sum(p, axis=0, keepdims=True)
            m_ref[hb:hb + 1, :] = m_new
            parts.append(p.astype(_MXU_DTYPE))
            alphas.append(alpha)
        pt = jnp.concatenate(parts, axis=1)
        alpha_all = jnp.concatenate(alphas, axis=1)
        pv = jnp.dot(vt_ref[c], pt, preferred_element_type=_F32)
        acc_ref[...] = acc_ref[...] * alpha_all + pv
        return carry

    lax.fori_loop(0, nchunks, attend, 0)

    l_all = jnp.concatenate([l_ref[hb:hb + 1, :] for hb in range(N_HEADS)], axis=1)
    o_t = acc_ref[...] / l_all
    blocks = []
    for j in range(N_HEADS // 2):
        mj = jnp.concatenate(
            [o_t[:HEAD_DIM, (2 * j) * tq:(2 * j + 1) * tq],
             o_t[HEAD_DIM:, (2 * j + 1) * tq:(2 * j + 2) * tq]], axis=0)
        blocks.append(mj.T)
    yb = jnp.concatenate(blocks, axis=1)
    out_ref[...] = (yb * zbs_ref[...]).astype(out_ref.dtype)


def _attn_prompt(qs, qi, wi, zbs, kibf, kbf, vt3, *, nb, t, tq, tk):
    nq = t // tq
    k_sel = min(TOP_K_MAX, t // 4)
    blk = lambda w: pl.BlockSpec((tq, w), lambda b, i: (b * nq + i, 0))
    res = lambda w: pl.BlockSpec((t, w), lambda b, i: (b, 0))
    kernel = functools.partial(_attn_prompt_kernel, tq=tq, tk=tk, k_sel=k_sel,
                               tie_steps=int(np.ceil(np.log2(t + 1))))
    return pl.pallas_call(
        kernel,
        grid=(nb, nq),
        in_specs=[blk(D_B), blk(N_IDX_HEADS * D_IDX), blk(LANES), blk(D_B),
                  res(2 * D_IDX), res(KV_W),
                  pl.BlockSpec((t // tk, KV_W, tk), lambda b, i: (b, 0, 0))],
        out_specs=blk(D_B),
        out_shape=jax.ShapeDtypeStruct((nb * t, D_B), _MXU_DTYPE),
        scratch_shapes=[
            pltpu.VMEM((t // tk, tk, tq), _F32),
            pltpu.VMEM((N_HEADS * tq, LANES), _MXU_DTYPE),
            pltpu.VMEM((N_IDX_HEADS * tq, LANES), _MXU_DTYPE),
            pltpu.VMEM((KV_W, N_HEADS * tq), _F32),
            pltpu.VMEM((N_HEADS, tq), _F32),
            pltpu.VMEM((N_HEADS, tq), _F32),
        ],
        compiler_params=pltpu.CompilerParams(
            dimension_semantics=("arbitrary", "arbitrary"), vmem_limit_bytes=VMEM_LIMIT_BYTES),
        name="attn_prompt",
    )(qs, qi, wi, zbs, kibf, kbf, vt3)


def _attn_sample_kernel(pt_ref, qs_ref, qi_ref, wi_ref, zbs_ref, kn_ref, vn_ref, kin_ref,
                        ck_hbm, cv_hbm, ci_hbm, out_ref,
                        kbuf, vbuf, ibuf, sems, s_ref, *, npages, tk, nchunks, k_sel, tie_steps):
    b = pl.program_id(0)
    nb = pl.num_programs(0)
    slot = lax.rem(b, 2)
    past = npages * PAGE_SIZE
    tq = qs_ref.shape[1]
    total = nchunks * tk
    kf = jnp.float32(k_sel)

    def page_copies(bb, sl, p):
        phys = pt_ref[bb * npages + p]
        dst = pl.ds(pl.multiple_of(p * PAGE_SIZE, PAGE_SIZE), PAGE_SIZE)
        return (pltpu.make_async_copy(ck_hbm.at[phys], kbuf.at[sl, dst, :], sems.at[0, sl]),
                pltpu.make_async_copy(cv_hbm.at[phys], vbuf.at[sl, dst, :], sems.at[1, sl]),
                pltpu.make_async_copy(ci_hbm.at[phys], ibuf.at[sl, dst, :], sems.at[2, sl]))

    def start_pages(bb, sl):
        def body(p, carry):
            for cp in page_copies(bb, sl, p):
                cp.start()
            return carry
        lax.fori_loop(0, npages, body, 0)

    def wait_pages(bb, sl):
        def body(p, carry):
            for cp in page_copies(bb, sl, p):
                cp.wait()
            return carry
        lax.fori_loop(0, npages, body, 0)

    @pl.when(b == 0)
    def _():
        for sl in range(2):
            kbuf[sl, past:, :] = jnp.zeros((tk, KV_W), _F32)
            vbuf[sl, past:, :] = jnp.zeros((tk, KV_W), _F32)
            ibuf[sl, past:, :] = jnp.zeros((tk, D_IDX), _F32)
        start_pages(0, 0)

    @pl.when(b + 1 < nb)
    def _():
        start_pages(b + 1, 1 - slot)

    wait_pages(b, slot)
    kbuf[slot, past:past + tq, :] = kn_ref[0]
    vbuf[slot, past:past + tq, :] = vn_ref[0]
    ibuf[slot, past:past + tq, :] = kin_ref[0]

    qs = qs_ref[0].astype(_F32)
    qi = qi_ref[0].astype(_F32)
    wi = wi_ref[0]
    lane = lax.broadcasted_iota(jnp.int32, (tq, LANES), 1)
    lo_half = lane < HEAD_DIM
    q8_rows = []
    for j in range(N_HEADS // 2):
        blk = qs[:, j * LANES:(j + 1) * LANES]
        q8_rows.append(jnp.where(lo_half, blk, 0.0))
        q8_rows.append(jnp.where(lo_half, 0.0, blk))
    q8 = jnp.concatenate(q8_rows, axis=0).astype(_MXU_DTYPE)
    qi8 = jnp.concatenate([qi[:, h * D_IDX:(h + 1) * D_IDX] for h in range(N_IDX_HEADS)],
                          axis=0).astype(_MXU_DTYPE)
    w_cols = [wi[:, h:h + 1] for h in range(N_IDX_HEADS)]
    nt_dims = (((1,), (1,)), ((), ()))

    qrow = lax.broadcasted_iota(jnp.int32, (tq, tk), 0)
    kcol = lax.broadcasted_iota(jnp.int32, (tq, tk), 1)
    key_minus_query = kcol - qrow

    def scores(c, carry):
        k0 = pl.multiple_of(c * tk, tk)
        kic = ibuf[slot, pl.ds(k0, tk), :].astype(_MXU_DTYPE)
        r = lax.dot_general(qi8, kic, nt_dims, preferred_element_type=_F32)
        s = None
        for h in range(N_IDX_HEADS):
            t = jnp.maximum(r[h * tq:(h + 1) * tq, :], 0.0) * w_cols[h]
            s = t if s is None else s + t
        s_ref[c] = jnp.where(key_minus_query <= past - c * tk, s, _NEG_INF)
        return carry

    lax.fori_loop(0, nchunks, scores, 0)

    def count(pred):
        def body(c, acc):
            return acc + jnp.where(pred(s_ref[c], c), 1.0, 0.0)
        acc = lax.fori_loop(0, nchunks, body, jnp.zeros((tq, tk), _F32))
        return jnp.sum(acc, axis=1, keepdims=True)

    thr, clo = _topk_threshold(lambda t: count(lambda s, c: s >= t), (tq, 1),
                               jnp.float32(total), k_sel)
    need = jnp.logical_and(clo > kf, thr > -_FLT_MAX)

    @pl.when(jnp.max(jnp.where(need, 1.0, 0.0)) > 0.5)
    def _():
        want = kf - count(lambda s, c: s > thr)
        cutoff = _tie_cutoff(
            lambda jv: count(lambda s, c: jnp.where(s == thr, kcol + c * tk, total) <= jv),
            need, want, total, tie_steps)

        def drop(c, carry):
            s = s_ref[c]
            s_ref[c] = jnp.where(jnp.where(s == thr, kcol + c * tk, -1) > cutoff, _NEG_INF, s)
            return carry

        lax.fori_loop(0, nchunks, drop, 0)

    def attend(c, carry):
        m_old, l_old, acc = carry
        k0 = pl.multiple_of(c * tk, tk)
        kc = kbuf[slot, pl.ds(k0, tk), :].astype(_MXU_DTYPE)
        vc = vbuf[slot, pl.ds(k0, tk), :].astype(_MXU_DTYPE)
        lg = lax.dot_general(q8, kc, nt_dims, preferred_element_type=_F32)
        sel = s_ref[c] >= thr
        lg = jnp.concatenate(
            [jnp.where(sel, lg[h * tq:(h + 1) * tq, :], _NEG_INF) for h in range(N_HEADS)], axis=0)
        m_new = jnp.maximum(m_old, jnp.max(lg, axis=1, keepdims=True))
        p = jnp.exp(lg - m_new)
        alpha = jnp.exp(m_old - m_new)
        l_new = alpha * l_old + jnp.sum(p, axis=1, keepdims=True)
        acc = alpha * acc + jnp.dot(p.astype(_MXU_DTYPE), vc, preferred_element_type=_F32)
        return m_new, l_new, acc

    init = (jnp.full((N_HEADS * tq, 1), -_FLT_MAX, _F32), jnp.zeros((N_HEADS * tq, 1), _F32),
            jnp.zeros((N_HEADS * tq, KV_W), _F32))
    _, l_fin, acc = lax.fori_loop(0, nchunks, attend, init)
    o = acc / l_fin
    blocks = []
    for j in range(N_HEADS // 2):
        blocks.append(jnp.where(lo_half, o[(2 * j) * tq:(2 * j + 1) * tq, :],
                                o[(2 * j + 1) * tq:(2 * j + 2) * tq, :]))
    yb = jnp.concatenate(blocks, axis=1)
    out_ref[0] = yb * zbs_ref[0]


def _attn_sample(page_table, qs, qi, wi, zbs, k_new, v_new, ki_new, cache_k, cache_v, cache_i,
                 *, tk):
    nb, npages = page_table.shape
    tq = qs.shape[0] // nb
    past = npages * PAGE_SIZE
    nchunks = (past + tk) // tk
    k_sel = min(TOP_K_MAX, (past + tq) // 4)
    three = lambda a: a.reshape(nb, tq, a.shape[-1])
    blk = lambda w: pl.BlockSpec((1, tq, w), lambda b, pt: (b, 0, 0))
    hbm = pl.BlockSpec(memory_space=pl.ANY)
    kernel = functools.partial(_attn_sample_kernel, npages=npages, tk=tk, nchunks=nchunks,
                               k_sel=k_sel, tie_steps=int(np.ceil(np.log2(nchunks * tk + 1))))
    grid_spec = pltpu.PrefetchScalarGridSpec(
        num_scalar_prefetch=1,
        grid=(nb,),
        in_specs=[blk(D_B), blk(N_IDX_HEADS * D_IDX), blk(LANES), blk(D_B),
                  blk(KV_W), blk(KV_W), blk(D_IDX), hbm, hbm, hbm],
        out_specs=blk(D_B),
        scratch_shapes=[
            pltpu.VMEM((2, past + tk, KV_W), _F32),
            pltpu.VMEM((2, past + tk, KV_W), _F32),
            pltpu.VMEM((2, past + tk, D_IDX), _F32),
            pltpu.SemaphoreType.DMA((3, 2)),
            pltpu.VMEM((nchunks, tq, tk), _F32),
        ],
    )
    out = pl.pallas_call(
        kernel,
        grid_spec=grid_spec,
        out_shape=jax.ShapeDtypeStruct((nb, tq, D_B), _F32),
        compiler_params=pltpu.CompilerParams(
            dimension_semantics=("arbitrary",), vmem_limit_bytes=VMEM_LIMIT_BYTES),
        name="attn_sample",
    )(page_table.reshape(-1), three(qs), three(qi), three(wi), three(zbs), three(k_new),
      three(v_new), three(ki_new), cache_k, cache_v, cache_i)
    return out.reshape(nb * tq, D_B)


def _outproj_kernel(yag_ref, ybg_ref, x_ref, w_ref, g_ref, out_ref):
    y = (jnp.dot(yag_ref[...], w_ref[:D_A, :], preferred_element_type=_F32)
         + jnp.dot(ybg_ref[...].astype(_MXU_DTYPE), w_ref[D_A:, :], preferred_element_type=_F32))
    ms = jnp.mean(y * y, axis=-1, keepdims=True)
    out_ref[...] = x_ref[...] + y * lax.rsqrt(ms + EPS) * g_ref[...]


def _outproj(yag, ybg, x2d, w_out_p, g_post, *, tm):
    r = x2d.shape[0]
    row = lambda w: pl.BlockSpec((tm, w), lambda i: (i, 0))
    const = lambda shape: pl.BlockSpec(shape, lambda i: (0,) * len(shape))
    return pl.pallas_call(
        _outproj_kernel,
        grid=(r // tm,),
        in_specs=[row(D_A), row(D_B), row(D_MODEL), const((D_A + D_B, D_MODEL)),
                  const((1, D_MODEL))],
        out_specs=row(D_MODEL),
        out_shape=jax.ShapeDtypeStruct((r, D_MODEL), _F32),
        compiler_params=pltpu.CompilerParams(
            dimension_semantics=("arbitrary",), vmem_limit_bytes=VMEM_LIMIT_BYTES),
        name="outproj",
    )(yag, ybg, x2d, w_out_p, g_post)


def _pad_in_weights(w):
    perm = _head_perm()
    nat = np.cumsum([0, D_A, D_A, D_A, D_B, KV_W, KV_W, D_B, N_IDX_HEADS * D_IDX, D_IDX,
                     N_IDX_HEADS])
    seg = lambda n: w[:, nat[n]:nat[n + 1]]
    pad = jnp.zeros((w.shape[0], C_END - C_WI - N_IDX_HEADS), w.dtype)
    return jnp.concatenate(
        [seg(0), seg(1), seg(2), seg(3)[:, perm], seg(4), seg(5), seg(6)[:, perm], seg(7),
         seg(8), seg(8), seg(9), pad], axis=1).astype(_MXU_DTYPE)


def _layer(x_prompt, x_sample, cache_k, cache_v, cache_idx_k, page_table, g_pre, w_in, ln_v_g,
           ln_v_b, w_s, b_s, w_out, g_post):
    nb, t, _ = x_prompt.shape
    nd, td, _ = x_sample.shape
    npages = page_table.shape[1]
    past = npages * PAGE_SIZE
    tm = 512
    tq = 128
    tk = 512

    w_pad = _pad_in_weights(w_in)
    perm = _head_perm()
    w_out_p = jnp.concatenate([w_out[:D_A], w_out[D_A:][perm]], axis=0).astype(_MXU_DTYPE)
    gpre = g_pre.reshape(1, D_MODEL)
    gpost = g_post.reshape(1, D_MODEL)
    lng = ln_v_g.reshape(1, D_A)
    lnb = ln_v_b.reshape(1, D_A)
    tril = jnp.tril(jnp.ones((CHUNK, CHUNK), dtype=bool))
    ws_tril = jnp.where(tril[None], w_s, jnp.zeros_like(w_s))

    xp = x_prompt.reshape(nb * t, D_MODEL)
    tpb = t // tm
    wmix_p = ws_tril.astype(_MXU_DTYPE)
    bmix_p = jnp.repeat(b_s.T, A_GROUP_DIM, axis=1)
    outs = _inproj(xp, _rope_tables(jnp.arange(t, dtype=jnp.int32)), lambda i: (i % tpb, 0),
                   gpre, w_pad, lng, lnb, wmix_p, bmix_p, tm=tm, state_rows=CHUNK,
                   state_shape=(nb, CHUNK, D_A), state_index=lambda i: (i // tpb, 0, 0))
    yag, zbs, qs, qi, k32, v32, ki32, wi, kbf, vt3, kibf, vstate_p = outs
    ybg = _attn_prompt(qs, qi, wi, zbs, kibf, kbf, vt3, nb=nb, t=t, tq=tq, tk=tk)
    y_prompt = _outproj(yag, ybg, xp, w_out_p, gpost, tm=tm).reshape(nb, t, D_MODEL)

    xs = x_sample.reshape(nd * td, D_MODEL)
    reps = CHUNK // td
    eye = jnp.eye(reps, dtype=w_s.dtype)
    wmix_s = jnp.stack([jnp.kron(eye, ws_tril[g, :td, :td]) for g in range(N_A_GROUPS)]
                       ).astype(_MXU_DTYPE)
    bmix_s = jnp.repeat(jnp.tile(b_s[:, :td], (1, reps)).T, A_GROUP_DIM, axis=1)
    tm_s = min(tm, nd * td)
    pos_s = past + (jnp.arange(tm_s, dtype=jnp.int32) % td)
    outs = _inproj(xs, _rope_tables(pos_s), lambda i: (0, 0), gpre, w_pad, lng, lnb, wmix_s,
                   bmix_s, tm=tm_s, state_rows=tm_s, state_shape=(nd * td, D_A),
                   state_index=lambda i: (i, 0))
    yag_s, zbs_s, qs_s, qi_s, k32_s, v32_s, ki32_s, wi_s, _, _, _, vstate_s = outs
    pool = cache_k.shape[0]
    ybg_s = _attn_sample(page_table, qs_s, qi_s, wi_s, zbs_s, k32_s, v32_s, ki32_s,
                         cache_k.reshape(pool, PAGE_SIZE, KV_W),
                         cache_v.reshape(pool, PAGE_SIZE, KV_W), cache_idx_k, tk=tk)
    y_sample = _outproj(yag_s, ybg_s, xs, w_out_p, gpost, tm=tm_s).reshape(nd, td, D_MODEL)

    return (y_prompt, y_sample,
            k32.reshape(nb, t, N_KV_HEADS, HEAD_DIM), v32.reshape(nb, t, N_KV_HEADS, HEAD_DIM),
            ki32.reshape(nb, t, D_IDX), vstate_p,
            k32_s.reshape(nd, td, N_KV_HEADS, HEAD_DIM), v32_s.reshape(nd, td, N_KV_HEADS, HEAD_DIM),
            ki32_s.reshape(nd, td, D_IDX), vstate_s.reshape(nd, td, D_A))


def kernel(x_prompt, x_sample, cache_k, cache_v, cache_idx_k, page_table, g_pre, w_in, ln_v_g,
           ln_v_b, w_s, b_s, w_out, g_post):
    xp, xs = x_prompt, x_sample
    per_layer = []
    for l in range(g_pre.shape[0]):
        outs = _layer(xp, xs, cache_k[l], cache_v[l], cache_idx_k[l], page_table, g_pre[l],
                      w_in[l], ln_v_g[l], ln_v_b[l], w_s[l], b_s[l], w_out[l], g_post[l])
        xp, xs = outs[0], outs[1]
        per_layer.append(outs[2:])
    stacked = tuple(jnp.stack(leaves, 0) for leaves in zip(*per_layer))
    return (xp, xs) + stacked
```

```python
import functools

import numpy as np
import jax
import jax.numpy as jnp
from jax import lax
from jax.scipy.special import ndtri
from jax.experimental import pallas as pl
from jax.experimental.pallas import tpu as pltpu

_F32 = jnp.float32
_MXU_DTYPE = jnp.bfloat16

D_MODEL = 1024
D_A = 512
D_B = 512
CHUNK = 128
N_A_GROUPS = 4
A_GROUP_DIM = D_A // N_A_GROUPS
HEAD_DIM = 64
N_HEADS = 8
N_KV_HEADS = 2
KV_W = N_KV_HEADS * HEAD_DIM
ROT_DIM = HEAD_DIM // 4
N_IDX_HEADS = 8
D_IDX = 64
TOP_K_MAX = 256
ROPE_THETA = 500000.0
EPS = 1e-6
PAGE_SIZE = 128

LANES = 128
VMEM_LIMIT_BYTES = 56 * 1024 * 1024

C_U, C_V, C_ZA, C_Q, C_K, C_VV, C_ZB, C_QI, C_KI, C_WI, C_END = (
    0, 512, 1024, 1536, 2048, 2176, 2304, 2816, 3328, 3456, 3584)

_FLT_MAX = float(np.finfo(np.float32).max)
_NEG_INF = float("-inf")
_INT_MIN = -2 ** 31


def _head_perm():
    c = np.arange(N_HEADS * HEAD_DIM)
    j = c // LANES
    half = (c % LANES) // HEAD_DIM
    d = c % HEAD_DIM
    return (j + 4 * half) * HEAD_DIM + d


def _rope_tables(pos):
    half = ROT_DIM // 2
    inv = jnp.power(jnp.float32(ROPE_THETA), -jnp.arange(half, dtype=_F32) * 2.0 / ROT_DIM)
    ang = pos.astype(_F32)[:, None] * inv[None, :]
    cos = jnp.cos(ang)
    sin = jnp.sin(ang)
    l64 = np.arange(LANES) % HEAD_DIM
    fidx = l64 % half
    in_rot = (l64 < ROT_DIM)[None, :]
    first = (l64 < half)[None, :]
    second = ((l64 >= half) & (l64 < ROT_DIM))[None, :]
    cos_t = jnp.where(in_rot, cos[:, fidx], 1.0)
    sin_a = jnp.where(first, -sin[:, fidx], 0.0)
    sin_b = jnp.where(second, sin[:, fidx], 0.0)
    return cos_t, sin_a, sin_b


def _silu(z):
    return z * (1.0 / (1.0 + jnp.exp(-z)))


def _inproj_kernel(x_ref, cos_ref, sa_ref, sb_ref, gpre_ref, win_ref, lng_ref, lnb_ref,
                   wmix_ref, bmix_ref,
                   yag_ref, zbs_ref, qs_ref, qi_ref, k32_ref, v32_ref, ki32_ref, wi_ref,
                   kbf_ref, vt_ref, kibf_ref, vstate_ref, *, tm, state_rows):
    x = x_ref[...]
    ms = jnp.mean(x * x, axis=-1, keepdims=True)
    h = (x * lax.rsqrt(ms + EPS) * gpre_ref[...]).astype(_MXU_DTYPE)

    def proj(a, b):
        return jnp.dot(h, win_ref[:, a:b], preferred_element_type=_F32)

    cos_t = cos_ref[...]
    sin_a = sa_ref[...]
    sin_b = sb_ref[...]

    def rope(p):
        blocks = []
        for j in range(p.shape[1] // LANES):
            pj = p[:, j * LANES:(j + 1) * LANES]
            blocks.append(pj * cos_t
                          + pltpu.roll(pj, LANES - ROT_DIM // 2, 1) * sin_a
                          + pltpu.roll(pj, ROT_DIM // 2, 1) * sin_b)
        return blocks[0] if len(blocks) == 1 else jnp.concatenate(blocks, axis=1)

    u = proj(C_U, C_V)
    v = proj(C_V, C_ZA)
    mu = jnp.mean(v, axis=-1, keepdims=True)
    vc = v - mu
    var = jnp.mean(vc * vc, axis=-1, keepdims=True)
    vn = vc * lax.rsqrt(var + EPS) * lng_ref[...] + lnb_ref[...]
    vstate_ref[...] = vn[tm - state_rows:, :].reshape(vstate_ref.shape)
    vnb = vn.astype(_MXU_DTYPE)
    bmix = bmix_ref[...]
    rows = []
    for c in range(tm // CHUNK):
        cols = []
        for g in range(N_A_GROUPS):
            blk = vnb[c * CHUNK:(c + 1) * CHUNK, g * A_GROUP_DIM:(g + 1) * A_GROUP_DIM]
            cols.append(jnp.dot(wmix_ref[g], blk, preferred_element_type=_F32)
                        + bmix[:, g * A_GROUP_DIM:(g + 1) * A_GROUP_DIM])
        rows.append(jnp.concatenate(cols, axis=1))
    mixed = jnp.concatenate(rows, axis=0)
    za = proj(C_ZA, C_Q)
    yag_ref[...] = ((u * mixed) * _silu(za)).astype(yag_ref.dtype)

    q = rope(proj(C_Q, C_K))
    qs_ref[...] = (q * (HEAD_DIM ** -0.5)).astype(qs_ref.dtype)
    k = rope(proj(C_K, C_VV))
    k32_ref[...] = k
    kbf_ref[...] = k.astype(kbf_ref.dtype)
    vv = proj(C_VV, C_ZB)
    v32_ref[...] = vv
    vt_ref[0] = vv.T.astype(vt_ref.dtype)
    zb = proj(C_ZB, C_QI)
    zbs_ref[...] = _silu(zb)
    qi = rope(proj(C_QI, C_KI))
    qi_ref[...] = qi.astype(qi_ref.dtype)
    ki = rope(proj(C_KI, C_WI))
    ki32_ref[...] = ki[:, :D_IDX]
    kibf_ref[...] = ki.astype(kibf_ref.dtype)
    wi_ref[...] = proj(C_WI, C_END) * (N_IDX_HEADS ** -0.5)


def _inproj(x2d, tables, table_index, gpre, w_pad, lng, lnb, wmix, bmix, *, tm, state_rows,
            state_shape, state_index):
    r = x2d.shape[0]
    nt = r // tm
    cos_t, sin_a, sin_b = tables
    row = lambda w: pl.BlockSpec((tm, w), lambda i: (i, 0))
    const = lambda shape: pl.BlockSpec(shape, lambda i: (0,) * len(shape))
    tab = pl.BlockSpec((tm, LANES), table_index)
    out_shapes = (
        jax.ShapeDtypeStruct((r, D_A), _MXU_DTYPE),
        jax.ShapeDtypeStruct((r, D_B), _F32),
        jax.ShapeDtypeStruct((r, D_B), _MXU_DTYPE),
        jax.ShapeDtypeStruct((r, N_IDX_HEADS * D_IDX), _MXU_DTYPE),
        jax.ShapeDtypeStruct((r, KV_W), _F32),
        jax.ShapeDtypeStruct((r, KV_W), _F32),
        jax.ShapeDtypeStruct((r, D_IDX), _F32),
        jax.ShapeDtypeStruct((r, LANES), _F32),
        jax.ShapeDtypeStruct((r, KV_W), _MXU_DTYPE),
        jax.ShapeDtypeStruct((nt, KV_W, tm), _MXU_DTYPE),
        jax.ShapeDtypeStruct((r, 2 * D_IDX), _MXU_DTYPE),
        jax.ShapeDtypeStruct(state_shape, _F32),
    )
    out_specs = (
        row(D_A), row(D_B), row(D_B), row(N_IDX_HEADS * D_IDX), row(KV_W), row(KV_W),
        row(D_IDX), row(LANES), row(KV_W),
        pl.BlockSpec((1, KV_W, tm), lambda i: (i, 0, 0)),
        row(2 * D_IDX),
        pl.BlockSpec((1,) * (len(state_shape) - 2) + (state_rows, D_A), state_index),
    )
    return pl.pallas_call(
        functools.partial(_inproj_kernel, tm=tm, state_rows=state_rows),
        grid=(nt,),
        in_specs=[row(D_MODEL), tab, tab, tab, const((1, D_MODEL)), const((D_MODEL, C_END)),
                  const((1, D_A)), const((1, D_A)), const((N_A_GROUPS, CHUNK, CHUNK)),
                  const((CHUNK, D_A))],
        out_specs=out_specs,
        out_shape=out_shapes,
        compiler_params=pltpu.CompilerParams(
            dimension_semantics=("arbitrary",), vmem_limit_bytes=VMEM_LIMIT_BYTES),
        name="inproj",
    )(x2d, cos_t, sin_a, sin_b, gpre, w_pad, lng, lnb, wmix, bmix)


def _f32_to_key(x):
    b = lax.bitcast_convert_type(x, jnp.int32)
    return jnp.where(b >= 0, b, b ^ jnp.int32(0x7FFFFFFF))


def _key_to_f32(k):
    return lax.bitcast_convert_type(jnp.where(k >= 0, k, k ^ jnp.int32(0x7FFFFFFF)), _F32)


N_INTERP_PASSES = 10
MAX_EXP_ARG = 40.0


def _topk_threshold(count_ge, max_below, smin, smax, n_adm, f0, k_sel):
    kf = jnp.float32(k_sel)
    log_k = jnp.log(kf - 0.5)
    lo0 = _f32_to_key(smin)
    hi0 = _f32_to_key(smax) + 1
    short = n_adm <= kf
    done0 = jnp.where(jnp.logical_or(short, hi0 <= lo0 + 1), 1.0, 0.0)
    g_hi0 = jnp.full(smin.shape, np.log(0.5), _F32) - log_k

    def all_done(done):
        return jnp.min(done) > 0.5

    def interp_cond(carry):
        return jnp.logical_and(carry[0] < N_INTERP_PASSES, jnp.logical_not(all_done(carry[-1])))

    def probe(carry, frac):
        it, lo, hi, clo, chi, glo, ghi, last, done = carry
        xlo = _key_to_f32(lo)
        xhi = _key_to_f32(hi)
        cand = jnp.clip(_f32_to_key(xlo + (xhi - xlo) * frac), lo + 1, hi - 1)
        cnt = count_ge(_key_to_f32(cand))
        g = jnp.log(jnp.maximum(cnt, 0.5)) - log_k
        active = done < 0.5
        new_lo = jnp.logical_and(active, cnt >= kf)
        new_hi = jnp.logical_and(active, cnt < kf)
        ghi = jnp.where(jnp.logical_and(new_lo, last == 1), ghi * 0.5, ghi)
        glo = jnp.where(jnp.logical_and(new_hi, last == 2), glo * 0.5, glo)
        lo = jnp.where(new_lo, cand, lo)
        clo = jnp.where(new_lo, cnt, clo)
        glo = jnp.where(new_lo, g, glo)
        hi = jnp.where(new_hi, cand, hi)
        chi = jnp.where(new_hi, cnt, chi)
        ghi = jnp.where(new_hi, g, ghi)
        last = jnp.where(new_lo, 1, jnp.where(new_hi, 2, last))
        done = jnp.where(jnp.logical_or(clo == kf, hi <= lo + 1), 1.0, done)
        return it + 1, lo, hi, clo, chi, glo, ghi, last, done

    def interp_body(carry):
        return probe(carry, carry[5] / (carry[5] - carry[6]))

    init = (jnp.int32(0), lo0, hi0, n_adm, jnp.zeros_like(n_adm),
            jnp.log(jnp.maximum(n_adm, 1.0)) - log_k, g_hi0, jnp.zeros(smin.shape, jnp.int32),
            done0)
    state = lax.while_loop(interp_cond, interp_body, probe(init, f0))
    _, lo, hi, clo, chi, _, _, _, done = state

    def peel_cond(carry):
        return jnp.logical_not(all_done(carry[-1]))

    def peel_body(carry):
        lo, hi, clo, chi, done = carry
        top = max_below(_key_to_f32(hi))
        cnt = count_ge(top)
        active = done < 0.5
        hit = jnp.logical_and(active, cnt >= kf)
        miss = jnp.logical_and(active, cnt < kf)
        lo = jnp.where(hit, _f32_to_key(top), lo)
        clo = jnp.where(hit, cnt, clo)
        hi = jnp.where(miss, _f32_to_key(top), hi)
        chi = jnp.where(miss, cnt, chi)
        done = jnp.where(hit, 1.0, done)
        return lo, hi, clo, chi, done

    lo, _, clo, chi, _ = lax.while_loop(peel_cond, peel_body, (lo, hi, clo, chi, done))
    thr = jnp.where(short, -_FLT_MAX, _key_to_f32(lo))
    return thr, clo, chi, short


def _seed_fraction(n_adm, k_sel):
    n = n_adm.astype(_F32)
    z_k = ndtri(1.0 - jnp.minimum(k_sel / n, 0.999))
    z_n = ndtri(1.0 - 1.0 / (n + 1.0))
    return (0.5 + 0.5 * z_k / z_n).astype(_F32)


def _attn_prompt_kernel(qs_ref, qi_ref, wi_ref, zbs_ref, f0_ref, ki_ref, k_ref, vt_ref, out_ref,
                        s_ref, q8_ref, qi8_ref, bias_ref, pt_ref, acc_ref, m_ref, l_ref,
                        mnew_ref, beta_ref, psum_ref, risk_ref, scale_ref, *, tq, tk, k_sel):
    i = pl.program_id(1)
    ratio = tk // tq
    nchunks = lax.div(i + ratio, ratio)
    kf = jnp.float32(k_sel)

    lane = lax.broadcasted_iota(jnp.int32, (tq, LANES), 1)
    lo_half = lane < HEAD_DIM
    for j in range(N_HEADS // 2):
        for src, dst in ((qs_ref, q8_ref), (qi_ref, qi8_ref)):
            blk = src[:, j * LANES:(j + 1) * LANES].astype(_F32)
            dst[(2 * j) * tq:(2 * j + 1) * tq, :] = jnp.where(lo_half, blk, 0.0).astype(dst.dtype)
            dst[(2 * j + 1) * tq:(2 * j + 2) * tq, :] = jnp.where(lo_half, 0.0, blk).astype(dst.dtype)
    w_t = wi_ref[...].T

    row = lax.broadcasted_iota(jnp.int32, (tk, tq), 0)
    col = lax.broadcasted_iota(jnp.int32, (tk, tq), 1)
    key_minus_query = row - col
    nt_dims = (((1,), (1,)), ((), ()))
    fold = lambda x: x.reshape(tk // 32, 32, tq)

    def scores(c, carry):
        mn, mx = carry
        k0 = pl.multiple_of(c * tk, tk)
        rt = lax.dot_general(ki_ref[pl.ds(k0, tk), :], qi8_ref[...], nt_dims,
                             preferred_element_type=_F32)
        s = None
        for hb in range(N_IDX_HEADS):
            t = jnp.maximum(rt[:, hb * tq:(hb + 1) * tq], 0.0) * w_t[hb:hb + 1, :]
            s = t if s is None else s + t
        s_ref[c] = jnp.where(key_minus_query <= i * tq - c * tk, s, _NEG_INF)
        return jnp.minimum(mn, fold(s).min(axis=0)), jnp.maximum(mx, fold(s).max(axis=0))

    mn, mx = lax.fori_loop(0, nchunks, scores,
                           (jnp.full((32, tq), jnp.inf, _F32), jnp.full((32, tq), -jnp.inf, _F32)))
    smin = jnp.min(mn, axis=0, keepdims=True)
    smax = jnp.max(mx, axis=0, keepdims=True)

    def count_ge(t):
        def body(c, acc):
            return acc + fold(jnp.where(s_ref[c] >= t, 1.0, 0.0)).sum(axis=0)
        acc = lax.fori_loop(0, nchunks, body, jnp.zeros((32, tq), _F32))
        return jnp.sum(acc, axis=0, keepdims=True)

    def max_below(x):
        def body(c, acc):
            s = s_ref[c]
            return jnp.maximum(acc, fold(jnp.where(s < x, s, _NEG_INF)).max(axis=0))
        acc = lax.fori_loop(0, nchunks, body, jnp.full((32, tq), -jnp.inf, _F32))
        return jnp.max(acc, axis=0, keepdims=True)

    n_adm = (i * tq + 1 + lax.broadcasted_iota(jnp.int32, (1, tq), 1)).astype(_F32)
    thr, clo, chi, short = _topk_threshold(count_ge, max_below, smin, smax, n_adm, f0_ref[0],
                                           k_sel)

    need = jnp.logical_and(clo > kf, jnp.logical_not(short))

    @pl.when(jnp.max(jnp.where(need, 1.0, 0.0)) > 0.5)
    def _():
        keep = jnp.where(need, kf - chi, _FLT_MAX)
        rk = lax.broadcasted_iota(jnp.int32, (tk, tk), 0)
        ck = lax.broadcasted_iota(jnp.int32, (tk, tk), 1)
        earlier = jnp.where(rk > ck, 1.0, 0.0).astype(_MXU_DTYPE)

        def drop(c, seen):
            s = s_ref[c]
            tie = s == thr
            tm = jnp.where(tie, 1.0, 0.0)
            before = jnp.dot(earlier, tm.astype(_MXU_DTYPE), preferred_element_type=_F32) + seen
            s_ref[c] = jnp.where(jnp.where(tie, before, -1.0) >= keep, _NEG_INF, s)
            return seen + jnp.sum(fold(tm).sum(axis=0), axis=0, keepdims=True)

        lax.fori_loop(0, nchunks, drop, jnp.zeros((1, tq), _F32))

    m_ref[...] = jnp.full(m_ref.shape, -_FLT_MAX, _F32)
    l_ref[...] = jnp.zeros(l_ref.shape, _F32)
    acc_ref[...] = jnp.zeros(acc_ref.shape, _F32)
    wide = lambda a: jnp.concatenate([a[hb:hb + 1, :] for hb in range(N_HEADS)], axis=1)

    def set_bias(c):
        bias_ref[...] = jnp.where(s_ref[c] >= thr, 0.0, _NEG_INF)

    def chunk_probs(c, against_own_max):
        k_chunk = k_ref[pl.ds(pl.multiple_of(c * tk, tk), tk), :]
        slot = lax.rem(c, 2)
        for j in range(N_HEADS // 2):
            l2 = lax.dot_general(k_chunk, q8_ref[(2 * j) * tq:(2 * j + 2) * tq, :], nt_dims,
                                 preferred_element_type=_F32)
            for half in range(2):
                hb = 2 * j + half
                lh = l2[:, half * tq:(half + 1) * tq] + bias_ref[...]
                m_old = m_ref[hb:hb + 1, :]
                m_new = jnp.maximum(m_old, jnp.max(fold(lh).max(axis=0), axis=0, keepdims=True))
                if against_own_max:
                    p = jnp.exp(lh - m_new)
                    beta_ref[hb:hb + 1, :] = jnp.ones((1, tq), _F32)
                else:
                    p = jnp.exp(lh - m_old)
                    beta_ref[hb:hb + 1, :] = jnp.exp(m_old - m_new)
                    risk_ref[hb:hb + 1, :] = m_new - m_old
                pt_ref[slot, :, hb * tq:(hb + 1) * tq] = p.astype(pt_ref.dtype)
                psum_ref[hb:hb + 1, :] = jnp.sum(fold(p).sum(axis=0), axis=0, keepdims=True)
                mnew_ref[hb:hb + 1, :] = m_new

    def fold_in_stats():
        m_new = mnew_ref[...]
        alpha = jnp.exp(m_ref[...] - m_new)
        beta = beta_ref[...]
        l_ref[...] = l_ref[...] * alpha + psum_ref[...] * beta
        m_ref[...] = m_new
        scale_ref[0:1, :] = wide(alpha)
        scale_ref[1:2, :] = wide(beta)

    def apply_values(c):
        pv = jnp.dot(vt_ref[c], pt_ref[lax.rem(c, 2)], preferred_element_type=_F32)
        acc_ref[...] = acc_ref[...] * scale_ref[0:1, :] + pv * scale_ref[1:2, :]

    set_bias(0)
    chunk_probs(0, against_own_max=True)
    fold_in_stats()

    def attend(c, carry):
        set_bias(c)
        apply_values(c - 1)
        chunk_probs(c, against_own_max=False)

        @pl.when(jnp.max(risk_ref[...]) > MAX_EXP_ARG)
        def _():
            chunk_probs(c, against_own_max=True)

        fold_in_stats()
        return carry

    lax.fori_loop(1, nchunks, attend, 0)
    apply_values(nchunks - 1)

    l_all = jnp.concatenate([l_ref[hb:hb + 1, :] for hb in range(N_HEADS)], axis=1)
    o_t = acc_ref[...] / l_all
    blocks = []
    for j in range(N_HEADS // 2):
        mj = jnp.concatenate(
            [o_t[:HEAD_DIM, (2 * j) * tq:(2 * j + 1) * tq],
             o_t[HEAD_DIM:, (2 * j + 1) * tq:(2 * j + 2) * tq]], axis=0)
        blocks.append(mj.T)
    yb = jnp.concatenate(blocks, axis=1)
    out_ref[...] = (yb * zbs_ref[...]).astype(out_ref.dtype)


def _attn_prompt(qs, qi, wi, zbs, kibf, kbf, vt3, *, nb, t, tq, tk):
    nq = t // tq
    k_sel = min(TOP_K_MAX, t // 4)
    f0 = _seed_fraction(jnp.arange(1, t + 1), k_sel).reshape(nq, 1, tq)
    blk = lambda w: pl.BlockSpec((tq, w), lambda b, i: (b * nq + i, 0))
    res = lambda w: pl.BlockSpec((t, w), lambda b, i: (b, 0))
    stat = pltpu.VMEM((N_HEADS, tq), _F32)
    kernel = functools.partial(_attn_prompt_kernel, tq=tq, tk=tk, k_sel=k_sel)
    return pl.pallas_call(
        kernel,
        grid=(nb, nq),
        in_specs=[blk(D_B), blk(N_IDX_HEADS * D_IDX), blk(LANES), blk(D_B),
                  pl.BlockSpec((1, 1, tq), lambda b, i: (i, 0, 0)),
                  res(2 * D_IDX), res(KV_W),
                  pl.BlockSpec((t // tk, KV_W, tk), lambda b, i: (b, 0, 0))],
        out_specs=blk(D_B),
        out_shape=jax.ShapeDtypeStruct((nb * t, D_B), _MXU_DTYPE),
        scratch_shapes=[
            pltpu.VMEM((t // tk, tk, tq), _F32),
            pltpu.VMEM((N_HEADS * tq, LANES), _MXU_DTYPE),
            pltpu.VMEM((N_IDX_HEADS * tq, LANES), _MXU_DTYPE),
            pltpu.VMEM((tk, tq), _F32),
            pltpu.VMEM((2, tk, N_HEADS * tq), _MXU_DTYPE),
            pltpu.VMEM((KV_W, N_HEADS * tq), _F32),
            stat, stat,
            stat, stat, stat, stat,
            pltpu.VMEM((2, N_HEADS * tq), _F32),
        ],
        compiler_params=pltpu.CompilerParams(
            dimension_semantics=("arbitrary", "arbitrary"), vmem_limit_bytes=VMEM_LIMIT_BYTES),
        name="attn_prompt",
    )(qs, qi, wi, zbs, f0, kibf, kbf, vt3)


def _attn_sample_kernel(pt_ref, qs_ref, qi_ref, wi_ref, zbs_ref, kn_ref, vn_ref, kin_ref, f0_ref,
                        ck_hbm, cv_hbm, ci_hbm, out_ref,
                        kbuf, vbuf, ibuf, sems, s_ref, *, npages, tk, nchunks, k_sel):
    b = pl.program_id(0)
    nb = pl.num_programs(0)
    slot = lax.rem(b, 2)
    past = npages * PAGE_SIZE
    tq = qs_ref.shape[1]
    kf = jnp.float32(k_sel)

    def page_copies(bb, sl, p):
        phys = pt_ref[bb * npages + p]
        dst = pl.ds(pl.multiple_of(p * PAGE_SIZE, PAGE_SIZE), PAGE_SIZE)
        return (pltpu.make_async_copy(ck_hbm.at[phys], kbuf.at[sl, dst, :], sems.at[0, sl]),
                pltpu.make_async_copy(cv_hbm.at[phys], vbuf.at[sl, dst, :], sems.at[1, sl]),
                pltpu.make_async_copy(ci_hbm.at[phys], ibuf.at[sl, dst, :], sems.at[2, sl]))

    def start_pages(bb, sl):
        def body(p, carry):
            for cp in page_copies(bb, sl, p):
                cp.start()
            return carry
        lax.fori_loop(0, npages, body, 0)

    def wait_pages(bb, sl):
        def body(p, carry):
            for cp in page_copies(bb, sl, p):
                cp.wait()
            return carry
        lax.fori_loop(0, npages, body, 0)

    @pl.when(b == 0)
    def _():
        for sl in range(2):
            kbuf[sl, past:, :] = jnp.zeros((nchunks * tk - past, KV_W), _F32)
            vbuf[sl, past:, :] = jnp.zeros((nchunks * tk - past, KV_W), _F32)
            ibuf[sl, past:, :] = jnp.zeros((nchunks * tk - past, D_IDX), _F32)
        start_pages(0, 0)

    @pl.when(b + 1 < nb)
    def _():
        start_pages(b + 1, 1 - slot)

    wait_pages(b, slot)
    kbuf[slot, past:past + tq, :] = kn_ref[0]
    vbuf[slot, past:past + tq, :] = vn_ref[0]
    ibuf[slot, past:past + tq, :] = kin_ref[0]

    qs = qs_ref[0].astype(_F32)
    qi = qi_ref[0].astype(_F32)
    wi = wi_ref[0]
    lane = lax.broadcasted_iota(jnp.int32, (tq, LANES), 1)
    lo_half = lane < HEAD_DIM
    q8_rows = []
    for j in range(N_HEADS // 2):
        blk = qs[:, j * LANES:(j + 1) * LANES]
        q8_rows.append(jnp.where(lo_half, blk, 0.0))
        q8_rows.append(jnp.where(lo_half, 0.0, blk))
    q8 = jnp.concatenate(q8_rows, axis=0).astype(_MXU_DTYPE)
    qi8 = jnp.concatenate([qi[:, h * D_IDX:(h + 1) * D_IDX] for h in range(N_IDX_HEADS)],
                          axis=0).astype(_MXU_DTYPE)
    w_cols = [wi[:, h:h + 1] for h in range(N_IDX_HEADS)]
    nt_dims = (((1,), (1,)), ((), ()))

    qrow = lax.broadcasted_iota(jnp.int32, (tq, tk), 0)
    kcol = lax.broadcasted_iota(jnp.int32, (tq, tk), 1)
    key_minus_query = kcol - qrow

    def scores(c, carry):
        mn, mx = carry
        k0 = pl.multiple_of(c * tk, tk)
        kic = ibuf[slot, pl.ds(k0, tk), :].astype(_MXU_DTYPE)
        r = lax.dot_general(qi8, kic, nt_dims, preferred_element_type=_F32)
        s = None
        for h in range(N_IDX_HEADS):
            t = jnp.maximum(r[h * tq:(h + 1) * tq, :], 0.0) * w_cols[h]
            s = t if s is None else s + t
        s_ref[c] = jnp.where(key_minus_query <= past - c * tk, s, _NEG_INF)
        return jnp.minimum(mn, s), jnp.maximum(mx, s)

    mn, mx = lax.fori_loop(0, nchunks, scores,
                           (jnp.full((tq, tk), jnp.inf, _F32), jnp.full((tq, tk), -jnp.inf, _F32)))
    smin = jnp.min(mn, axis=1, keepdims=True)
    smax = jnp.max(mx, axis=1, keepdims=True)

    def count_ge(t):
        def body(c, acc):
            return acc + jnp.where(s_ref[c] >= t, 1.0, 0.0)
        acc = lax.fori_loop(0, nchunks, body, jnp.zeros((tq, tk), _F32))
        return jnp.sum(acc, axis=1, keepdims=True)

    def max_below(x):
        def body(c, acc):
            s = s_ref[c]
            return jnp.maximum(acc, jnp.where(s < x, s, _NEG_INF))
        acc = lax.fori_loop(0, nchunks, body, jnp.full((tq, tk), -jnp.inf, _F32))
        return jnp.max(acc, axis=1, keepdims=True)

    n_adm = (past + 1 + lax.broadcasted_iota(jnp.int32, (tq, 1), 0)).astype(_F32)
    thr, clo, chi, short = _topk_threshold(count_ge, max_below, smin, smax, n_adm, f0_ref[...],
                                           k_sel)
    need = jnp.logical_and(clo > kf, jnp.logical_not(short))

    @pl.when(jnp.max(jnp.where(need, 1.0, 0.0)) > 0.5)
    def _():
        keep = jnp.where(need, kf - chi, _FLT_MAX)
        rk = lax.broadcasted_iota(jnp.int32, (tk, tk), 0)
        ck = lax.broadcasted_iota(jnp.int32, (tk, tk), 1)
        earlier = jnp.where(rk < ck, 1.0, 0.0).astype(_MXU_DTYPE)

        def drop(c, seen):
            s = s_ref[c]
            tie = s == thr
            tm = jnp.where(tie, 1.0, 0.0)
            before = jnp.dot(tm.astype(_MXU_DTYPE), earlier, preferred_element_type=_F32) + seen
            s_ref[c] = jnp.where(jnp.where(tie, before, -1.0) >= keep, _NEG_INF, s)
            return seen + jnp.sum(tm, axis=1, keepdims=True)

        lax.fori_loop(0, nchunks, drop, jnp.zeros((tq, 1), _F32))

    def attend(c, carry):
        m_old, l_old, acc = carry
        k0 = pl.multiple_of(c * tk, tk)
        kc = kbuf[slot, pl.ds(k0, tk), :].astype(_MXU_DTYPE)
        vc = vbuf[slot, pl.ds(k0, tk), :].astype(_MXU_DTYPE)
        lg = lax.dot_general(q8, kc, nt_dims, preferred_element_type=_F32)
        sel = s_ref[c] >= thr
        lg = jnp.concatenate(
            [jnp.where(sel, lg[h * tq:(h + 1) * tq, :], _NEG_INF) for h in range(N_HEADS)], axis=0)
        m_new = jnp.maximum(m_old, jnp.max(lg, axis=1, keepdims=True))
        p = jnp.exp(lg - m_new)
        alpha = jnp.exp(m_old - m_new)
        l_new = alpha * l_old + jnp.sum(p, axis=1, keepdims=True)
        acc = alpha * acc + jnp.dot(p.astype(_MXU_DTYPE), vc, preferred_element_type=_F32)
        return m_new, l_new, acc

    init = (jnp.full((N_HEADS * tq, 1), -_FLT_MAX, _F32), jnp.zeros((N_HEADS * tq, 1), _F32),
            jnp.zeros((N_HEADS * tq, KV_W), _F32))
    _, l_fin, acc = lax.fori_loop(0, nchunks, attend, init)
    o = acc / l_fin
    blocks = []
    for j in range(N_HEADS // 2):
        blocks.append(jnp.where(lo_half, o[(2 * j) * tq:(2 * j + 1) * tq, :],
                                o[(2 * j + 1) * tq:(2 * j + 2) * tq, :]))
    yb = jnp.concatenate(blocks, axis=1)
    out_ref[0] = yb * zbs_ref[0]


def _attn_sample(page_table, qs, qi, wi, zbs, k_new, v_new, ki_new, cache_k, cache_v, cache_i,
                 *, tk):
    nb, npages = page_table.shape
    tq = qs.shape[0] // nb
    past = npages * PAGE_SIZE
    nchunks = -(-(past + tq) // tk)
    rows = nchunks * tk
    k_sel = min(TOP_K_MAX, (past + tq) // 4)
    three = lambda a: a.reshape(nb, tq, a.shape[-1])
    blk = lambda w: pl.BlockSpec((1, tq, w), lambda b, pt: (b, 0, 0))
    hbm = pl.BlockSpec(memory_space=pl.ANY)
    f0 = _seed_fraction(past + 1 + jnp.arange(tq), k_sel).reshape(tq, 1)
    kernel = functools.partial(_attn_sample_kernel, npages=npages, tk=tk, nchunks=nchunks,
                               k_sel=k_sel)
    grid_spec = pltpu.PrefetchScalarGridSpec(
        num_scalar_prefetch=1,
        grid=(nb,),
        in_specs=[blk(D_B), blk(N_IDX_HEADS * D_IDX), blk(LANES), blk(D_B),
                  blk(KV_W), blk(KV_W), blk(D_IDX),
                  pl.BlockSpec((tq, 1), lambda b, pt: (0, 0)), hbm, hbm, hbm],
        out_specs=blk(D_B),
        scratch_shapes=[
            pltpu.VMEM((2, rows, KV_W), _F32),
            pltpu.VMEM((2, rows, KV_W), _F32),
            pltpu.VMEM((2, rows, D_IDX), _F32),
            pltpu.SemaphoreType.DMA((3, 2)),
            pltpu.VMEM((nchunks, tq, tk), _F32),
        ],
    )
    out = pl.pallas_call(
        kernel,
        grid_spec=grid_spec,
        out_shape=jax.ShapeDtypeStruct((nb, tq, D_B), _F32),
        compiler_params=pltpu.CompilerParams(
            dimension_semantics=("arbitrary",), vmem_limit_bytes=VMEM_LIMIT_BYTES),
        name="attn_sample",
    )(page_table.reshape(-1), three(qs), three(qi), three(wi), three(zbs), three(k_new),
      three(v_new), three(ki_new), f0, cache_k, cache_v, cache_i)
    return out.reshape(nb * tq, D_B)


def _outproj_kernel(yag_ref, ybg_ref, x_ref, w_ref, g_ref, out_ref):
    y = (jnp.dot(yag_ref[...], w_ref[:D_A, :], preferred_element_type=_F32)
         + jnp.dot(ybg_ref[...].astype(_MXU_DTYPE), w_ref[D_A:, :], preferred_element_type=_F32))
    ms = jnp.mean(y * y, axis=-1, keepdims=True)
    out_ref[...] = x_ref[...] + y * lax.rsqrt(ms + EPS) * g_ref[...]


def _outproj(yag, ybg, x2d, w_out_p, g_post, *, tm):
    r = x2d.shape[0]
    row = lambda w: pl.BlockSpec((tm, w), lambda i: (i, 0))
    const = lambda shape: pl.BlockSpec(shape, lambda i: (0,) * len(shape))
    return pl.pallas_call(
        _outproj_kernel,
        grid=(r // tm,),
        in_specs=[row(D_A), row(D_B), row(D_MODEL), const((D_A + D_B, D_MODEL)),
                  const((1, D_MODEL))],
        out_specs=row(D_MODEL),
        out_shape=jax.ShapeDtypeStruct((r, D_MODEL), _F32),
        compiler_params=pltpu.CompilerParams(
            dimension_semantics=("arbitrary",), vmem_limit_bytes=VMEM_LIMIT_BYTES),
        name="outproj",
    )(yag, ybg, x2d, w_out_p, g_post)


def _pad_in_weights(w):
    perm = _head_perm()
    nat = np.cumsum([0, D_A, D_A, D_A, D_B, KV_W, KV_W, D_B, N_IDX_HEADS * D_IDX, D_IDX,
                     N_IDX_HEADS])
    seg = lambda n: w[:, nat[n]:nat[n + 1]]
    pad = jnp.zeros((w.shape[0], C_END - C_WI - N_IDX_HEADS), w.dtype)
    return jnp.concatenate(
        [seg(0), seg(1), seg(2), seg(3)[:, perm], seg(4), seg(5), seg(6)[:, perm], seg(7),
         seg(8), seg(8), seg(9), pad], axis=1).astype(_MXU_DTYPE)


def _layer(x_prompt, x_sample, cache_k, cache_v, cache_idx_k, page_table, g_pre, w_in, ln_v_g,
           ln_v_b, w_s, b_s, w_out, g_post):
    nb, t, _ = x_prompt.shape
    nd, td, _ = x_sample.shape
    npages = page_table.shape[1]
    past = npages * PAGE_SIZE
    tm = 1024
    tq = 128
    tk = 1024
    tk_sample = 512

    w_pad = _pad_in_weights(w_in)
    perm = _head_perm()
    w_out_p = jnp.concatenate([w_out[:D_A], w_out[D_A:][perm]], axis=0).astype(_MXU_DTYPE)
    gpre = g_pre.reshape(1, D_MODEL)
    gpost = g_post.reshape(1, D_MODEL)
    lng = ln_v_g.reshape(1, D_A)
    lnb = ln_v_b.reshape(1, D_A)
    tril = jnp.tril(jnp.ones((CHUNK, CHUNK), dtype=bool))
    ws_tril = jnp.where(tril[None], w_s, jnp.zeros_like(w_s))

    xp = x_prompt.reshape(nb * t, D_MODEL)
    tpb = t // tm
    wmix_p = ws_tril.astype(_MXU_DTYPE)
    bmix_p = jnp.repeat(b_s.T, A_GROUP_DIM, axis=1)
    outs = _inproj(xp, _rope_tables(jnp.arange(t, dtype=jnp.int32)), lambda i: (i % tpb, 0),
                   gpre, w_pad, lng, lnb, wmix_p, bmix_p, tm=tm, state_rows=CHUNK,
                   state_shape=(nb, CHUNK, D_A), state_index=lambda i: (i // tpb, 0, 0))
    yag, zbs, qs, qi, k32, v32, ki32, wi, kbf, vt3, kibf, vstate_p = outs
    ybg = _attn_prompt(qs, qi, wi, zbs, kibf, kbf, vt3, nb=nb, t=t, tq=tq, tk=tk)
    y_prompt = _outproj(yag, ybg, xp, w_out_p, gpost, tm=tm).reshape(nb, t, D_MODEL)

    xs = x_sample.reshape(nd * td, D_MODEL)
    reps = CHUNK // td
    eye = jnp.eye(reps, dtype=w_s.dtype)
    wmix_s = jnp.stack([jnp.kron(eye, ws_tril[g, :td, :td]) for g in range(N_A_GROUPS)]
                       ).astype(_MXU_DTYPE)
    bmix_s = jnp.repeat(jnp.tile(b_s[:, :td], (1, reps)).T, A_GROUP_DIM, axis=1)
    tm_s = min(tm, nd * td)
    pos_s = past + (jnp.arange(tm_s, dtype=jnp.int32) % td)
    outs = _inproj(xs, _rope_tables(pos_s), lambda i: (0, 0), gpre, w_pad, lng, lnb, wmix_s,
                   bmix_s, tm=tm_s, state_rows=tm_s, state_shape=(nd * td, D_A),
                   state_index=lambda i: (i, 0))
    yag_s, zbs_s, qs_s, qi_s, k32_s, v32_s, ki32_s, wi_s, _, _, _, vstate_s = outs
    pool = cache_k.shape[0]
    ybg_s = _attn_sample(page_table, qs_s, qi_s, wi_s, zbs_s, k32_s, v32_s, ki32_s,
                         cache_k.reshape(pool, PAGE_SIZE, KV_W),
                         cache_v.reshape(pool, PAGE_SIZE, KV_W), cache_idx_k, tk=tk_sample)
    y_sample = _outproj(yag_s, ybg_s, xs, w_out_p, gpost, tm=tm_s).reshape(nd, td, D_MODEL)

    return (y_prompt, y_sample,
            k32.reshape(nb, t, N_KV_HEADS, HEAD_DIM), v32.reshape(nb, t, N_KV_HEADS, HEAD_DIM),
            ki32.reshape(nb, t, D_IDX), vstate_p,
            k32_s.reshape(nd, td, N_KV_HEADS, HEAD_DIM), v32_s.reshape(nd, td, N_KV_HEADS, HEAD_DIM),
            ki32_s.reshape(nd, td, D_IDX), vstate_s.reshape(nd, td, D_A))


def kernel(x_prompt, x_sample, cache_k, cache_v, cache_idx_k, page_table, g_pre, w_in, ln_v_g,
           ln_v_b, w_s, b_s, w_out, g_post):
    xp, xs = x_prompt, x_sample
    per_layer = []
    for l in range(g_pre.shape[0]):
        outs = _layer(xp, xs, cache_k[l], cache_v[l], cache_idx_k[l], page_table, g_pre[l],
                      w_in[l], ln_v_g[l], ln_v_b[l], w_s[l], b_s[l], w_out[l], g_post[l])
        xp, xs = outs[0], outs[1]
        per_layer.append(outs[2:])
    stacked = tuple(jnp.stack(leaves, 0) for leaves in zip(*per_layer))
    return (xp, xs) + stacked
```

```python
import functools

import numpy as np
import jax
import jax.numpy as jnp
from jax import lax
from jax.scipy.special import ndtri
from jax.experimental import pallas as pl
from jax.experimental.pallas import tpu as pltpu

_F32 = jnp.float32
_MXU_DTYPE = jnp.bfloat16

D_MODEL = 1024
D_A = 512
D_B = 512
CHUNK = 128
N_A_GROUPS = 4
A_GROUP_DIM = D_A // N_A_GROUPS
HEAD_DIM = 64
N_HEADS = 8
N_KV_HEADS = 2
KV_W = N_KV_HEADS * HEAD_DIM
ROT_DIM = HEAD_DIM // 4
N_IDX_HEADS = 8
D_IDX = 64
TOP_K_MAX = 256
ROPE_THETA = 500000.0
EPS = 1e-6
PAGE_SIZE = 128

LANES = 128
NEW_ROWS = 16
VMEM_LIMIT_BYTES = 56 * 1024 * 1024

C_U, C_V, C_ZA, C_Q, C_K, C_VV, C_ZB, C_QI, C_KI, C_WI, C_END = (
    0, 512, 1024, 1536, 2048, 2176, 2304, 2816, 3328, 3456, 3584)

_FLT_MAX = float(np.finfo(np.float32).max)
_NEG_INF = float("-inf")
_INT_MIN = -2 ** 31


def _head_perm():
    c = np.arange(N_HEADS * HEAD_DIM)
    j = c // LANES
    half = (c % LANES) // HEAD_DIM
    d = c % HEAD_DIM
    return (j + 4 * half) * HEAD_DIM + d


def _rope_tables(pos):
    half = ROT_DIM // 2
    inv = jnp.power(jnp.float32(ROPE_THETA), -jnp.arange(half, dtype=_F32) * 2.0 / ROT_DIM)
    ang = pos.astype(_F32)[:, None] * inv[None, :]
    cos = jnp.cos(ang)
    sin = jnp.sin(ang)
    l64 = np.arange(LANES) % HEAD_DIM
    fidx = l64 % half
    in_rot = (l64 < ROT_DIM)[None, :]
    first = (l64 < half)[None, :]
    second = ((l64 >= half) & (l64 < ROT_DIM))[None, :]
    cos_t = jnp.where(in_rot, cos[:, fidx], 1.0)
    sin_a = jnp.where(first, -sin[:, fidx], 0.0)
    sin_b = jnp.where(second, sin[:, fidx], 0.0)
    return cos_t, sin_a, sin_b


def _silu(z):
    return z * (1.0 / (1.0 + jnp.exp(-z)))


def _inproj_kernel(x_ref, cos_ref, sa_ref, sb_ref, gpre_ref, win_ref, lng_ref, lnb_ref,
                   wmix_ref, bmix_ref,
                   yag_ref, zbs_ref, qs_ref, qi_ref, k32_ref, v32_ref, ki32_ref, wi_ref,
                   kbf_ref, vt_ref, kibf_ref, vstate_ref, *, tm, state_rows):
    x = x_ref[...]
    ms = jnp.mean(x * x, axis=-1, keepdims=True)
    h = (x * lax.rsqrt(ms + EPS) * gpre_ref[...]).astype(_MXU_DTYPE)

    def proj(a, b):
        return jnp.dot(h, win_ref[:, a:b], preferred_element_type=_F32)

    cos_t = cos_ref[...]
    sin_a = sa_ref[...]
    sin_b = sb_ref[...]

    def rope(p):
        blocks = []
        for j in range(p.shape[1] // LANES):
            pj = p[:, j * LANES:(j + 1) * LANES]
            blocks.append(pj * cos_t
                          + pltpu.roll(pj, LANES - ROT_DIM // 2, 1) * sin_a
                          + pltpu.roll(pj, ROT_DIM // 2, 1) * sin_b)
        return blocks[0] if len(blocks) == 1 else jnp.concatenate(blocks, axis=1)

    u = proj(C_U, C_V)
    v = proj(C_V, C_ZA)
    mu = jnp.mean(v, axis=-1, keepdims=True)
    vc = v - mu
    var = jnp.mean(vc * vc, axis=-1, keepdims=True)
    vn = vc * lax.rsqrt(var + EPS) * lng_ref[...] + lnb_ref[...]
    vstate_ref[...] = vn[tm - state_rows:, :].reshape(vstate_ref.shape)
    vnb = vn.astype(_MXU_DTYPE)
    bmix = bmix_ref[...]
    rows = []
    for c in range(tm // CHUNK):
        cols = []
        for g in range(N_A_GROUPS):
            blk = vnb[c * CHUNK:(c + 1) * CHUNK, g * A_GROUP_DIM:(g + 1) * A_GROUP_DIM]
            cols.append(jnp.dot(wmix_ref[g], blk, preferred_element_type=_F32)
                        + bmix[:, g * A_GROUP_DIM:(g + 1) * A_GROUP_DIM])
        rows.append(jnp.concatenate(cols, axis=1))
    mixed = jnp.concatenate(rows, axis=0)
    za = proj(C_ZA, C_Q)
    yag_ref[...] = ((u * mixed) * _silu(za)).astype(yag_ref.dtype)

    q = rope(proj(C_Q, C_K))
    qs_ref[...] = (q * (HEAD_DIM ** -0.5)).astype(qs_ref.dtype)
    k = rope(proj(C_K, C_VV))
    k32_ref[...] = k
    kbf_ref[...] = k.astype(kbf_ref.dtype)
    vv = proj(C_VV, C_ZB)
    v32_ref[...] = vv
    vt_ref[0] = vv.T.astype(vt_ref.dtype)
    zb = proj(C_ZB, C_QI)
    zbs_ref[...] = _silu(zb)
    qi = rope(proj(C_QI, C_KI))
    qi_ref[...] = qi.astype(qi_ref.dtype)
    ki = rope(proj(C_KI, C_WI))
    ki32_ref[...] = ki[:, :D_IDX]
    kibf_ref[...] = ki.astype(kibf_ref.dtype)
    wi_ref[...] = proj(C_WI, C_END) * (N_IDX_HEADS ** -0.5)


def _inproj(x2d, tables, table_index, gpre, w_pad, lng, lnb, wmix, bmix, *, tm, state_rows,
            state_shape, state_index):
    r = x2d.shape[0]
    nt = r // tm
    cos_t, sin_a, sin_b = tables
    row = lambda w: pl.BlockSpec((tm, w), lambda i: (i, 0))
    const = lambda shape: pl.BlockSpec(shape, lambda i: (0,) * len(shape))
    tab = pl.BlockSpec((tm, LANES), table_index)
    out_shapes = (
        jax.ShapeDtypeStruct((r, D_A), _MXU_DTYPE),
        jax.ShapeDtypeStruct((r, D_B), _F32),
        jax.ShapeDtypeStruct((r, D_B), _MXU_DTYPE),
        jax.ShapeDtypeStruct((r, N_IDX_HEADS * D_IDX), _MXU_DTYPE),
        jax.ShapeDtypeStruct((r, KV_W), _F32),
        jax.ShapeDtypeStruct((r, KV_W), _F32),
        jax.ShapeDtypeStruct((r, D_IDX), _F32),
        jax.ShapeDtypeStruct((r, LANES), _F32),
        jax.ShapeDtypeStruct((r, KV_W), _MXU_DTYPE),
        jax.ShapeDtypeStruct((nt, KV_W, tm), _MXU_DTYPE),
        jax.ShapeDtypeStruct((r, 2 * D_IDX), _MXU_DTYPE),
        jax.ShapeDtypeStruct(state_shape, _F32),
    )
    out_specs = (
        row(D_A), row(D_B), row(D_B), row(N_IDX_HEADS * D_IDX), row(KV_W), row(KV_W),
        row(D_IDX), row(LANES), row(KV_W),
        pl.BlockSpec((1, KV_W, tm), lambda i: (i, 0, 0)),
        row(2 * D_IDX),
        pl.BlockSpec((1,) * (len(state_shape) - 2) + (state_rows, D_A), state_index),
    )
    return pl.pallas_call(
        functools.partial(_inproj_kernel, tm=tm, state_rows=state_rows),
        grid=(nt,),
        in_specs=[row(D_MODEL), tab, tab, tab, const((1, D_MODEL)), const((D_MODEL, C_END)),
                  const((1, D_A)), const((1, D_A)), const((N_A_GROUPS, CHUNK, CHUNK)),
                  const((CHUNK, D_A))],
        out_specs=out_specs,
        out_shape=out_shapes,
        compiler_params=pltpu.CompilerParams(
            dimension_semantics=("arbitrary",), vmem_limit_bytes=VMEM_LIMIT_BYTES),
        name="inproj",
    )(x2d, cos_t, sin_a, sin_b, gpre, w_pad, lng, lnb, wmix, bmix)


def _f32_to_key(x):
    b = lax.bitcast_convert_type(x, jnp.int32)
    return jnp.where(b >= 0, b, b ^ jnp.int32(0x7FFFFFFF))


def _key_to_f32(k):
    return lax.bitcast_convert_type(jnp.where(k >= 0, k, k ^ jnp.int32(0x7FFFFFFF)), _F32)


N_INTERP_PASSES = 10
MAX_EXP_ARG = 40.0


def _topk_threshold(count_ge, max_below, smin, smax, n_adm, f0, k_sel):
    kf = jnp.float32(k_sel)
    log_k = jnp.log(kf - 0.5)
    lo0 = _f32_to_key(smin)
    hi0 = _f32_to_key(smax) + 1
    short = n_adm <= kf
    done0 = jnp.where(jnp.logical_or(short, hi0 <= lo0 + 1), 1.0, 0.0)
    g_hi0 = jnp.full(smin.shape, np.log(0.5), _F32) - log_k

    def all_done(done):
        return jnp.min(done) > 0.5

    def interp_cond(carry):
        return jnp.logical_and(carry[0] < N_INTERP_PASSES, jnp.logical_not(all_done(carry[-1])))

    def probe(carry, frac):
        it, lo, hi, clo, chi, glo, ghi, last, done = carry
        xlo = _key_to_f32(lo)
        xhi = _key_to_f32(hi)
        cand = jnp.clip(_f32_to_key(xlo + (xhi - xlo) * frac), lo + 1, hi - 1)
        cnt = count_ge(_key_to_f32(cand))
        g = jnp.log(jnp.maximum(cnt, 0.5)) - log_k
        active = done < 0.5
        new_lo = jnp.logical_and(active, cnt >= kf)
        new_hi = jnp.logical_and(active, cnt < kf)
        ghi = jnp.where(jnp.logical_and(new_lo, last == 1), ghi * 0.5, ghi)
        glo = jnp.where(jnp.logical_and(new_hi, last == 2), glo * 0.5, glo)
        lo = jnp.where(new_lo, cand, lo)
        clo = jnp.where(new_lo, cnt, clo)
        glo = jnp.where(new_lo, g, glo)
        hi = jnp.where(new_hi, cand, hi)
        chi = jnp.where(new_hi, cnt, chi)
        ghi = jnp.where(new_hi, g, ghi)
        last = jnp.where(new_lo, 1, jnp.where(new_hi, 2, last))
        done = jnp.where(jnp.logical_or(clo == kf, hi <= lo + 1), 1.0, done)
        return it + 1, lo, hi, clo, chi, glo, ghi, last, done

    def interp_body(carry):
        return probe(carry, carry[5] / (carry[5] - carry[6]))

    init = (jnp.int32(0), lo0, hi0, n_adm, jnp.zeros_like(n_adm),
            jnp.log(jnp.maximum(n_adm, 1.0)) - log_k, g_hi0, jnp.zeros(smin.shape, jnp.int32),
            done0)
    state = lax.while_loop(interp_cond, interp_body, probe(init, f0))
    _, lo, hi, clo, chi, _, _, _, done = state

    def peel_cond(carry):
        return jnp.logical_not(all_done(carry[-1]))

    def peel_body(carry):
        lo, hi, clo, chi, done = carry
        top = max_below(_key_to_f32(hi))
        cnt = count_ge(top)
        active = done < 0.5
        hit = jnp.logical_and(active, cnt >= kf)
        miss = jnp.logical_and(active, cnt < kf)
        lo = jnp.where(hit, _f32_to_key(top), lo)
        clo = jnp.where(hit, cnt, clo)
        hi = jnp.where(miss, _f32_to_key(top), hi)
        chi = jnp.where(miss, cnt, chi)
        done = jnp.where(hit, 1.0, done)
        return lo, hi, clo, chi, done

    lo, _, clo, chi, _ = lax.while_loop(peel_cond, peel_body, (lo, hi, clo, chi, done))
    thr = jnp.where(short, -_FLT_MAX, _key_to_f32(lo))
    return thr, clo, chi, short


def _seed_fraction(n_adm, k_sel):
    n = n_adm.astype(_F32)
    z_k = ndtri(1.0 - jnp.minimum(k_sel / n, 0.999))
    z_n = ndtri(1.0 - 1.0 / (n + 1.0))
    return (0.5 + 0.5 * z_k / z_n).astype(_F32)


def _attn_prompt_kernel(qs_ref, qi_ref, wi_ref, zbs_ref, f0_ref, ki_ref, k_ref, vt_ref, out_ref,
                        s_ref, q8_ref, qi8_ref, bias_ref, pt_ref, acc_ref, m_ref, l_ref,
                        mnew_ref, beta_ref, psum_ref, risk_ref, scale_ref, *, tq, tk, k_sel):
    i = pl.program_id(1)
    ratio = tk // tq
    nchunks = lax.div(i + ratio, ratio)
    kf = jnp.float32(k_sel)

    lane = lax.broadcasted_iota(jnp.int32, (tq, LANES), 1)
    lo_half = lane < HEAD_DIM
    for j in range(N_HEADS // 2):
        for src, dst in ((qs_ref, q8_ref), (qi_ref, qi8_ref)):
            blk = src[:, j * LANES:(j + 1) * LANES].astype(_F32)
            dst[(2 * j) * tq:(2 * j + 1) * tq, :] = jnp.where(lo_half, blk, 0.0).astype(dst.dtype)
            dst[(2 * j + 1) * tq:(2 * j + 2) * tq, :] = jnp.where(lo_half, 0.0, blk).astype(dst.dtype)
    w_t = wi_ref[...].T

    row = lax.broadcasted_iota(jnp.int32, (tk, tq), 0)
    col = lax.broadcasted_iota(jnp.int32, (tk, tq), 1)
    key_minus_query = row - col
    nt_dims = (((1,), (1,)), ((), ()))
    fold = lambda x: x.reshape(tk // 32, 32, tq)

    def scores(c, carry):
        mn, mx = carry
        k0 = pl.multiple_of(c * tk, tk)
        rt = lax.dot_general(ki_ref[pl.ds(k0, tk), :], qi8_ref[...], nt_dims,
                             preferred_element_type=_F32)
        s = None
        for hb in range(N_IDX_HEADS):
            t = jnp.maximum(rt[:, hb * tq:(hb + 1) * tq], 0.0) * w_t[hb:hb + 1, :]
            s = t if s is None else s + t
        s_ref[c] = jnp.where(key_minus_query <= i * tq - c * tk, s, _NEG_INF)
        return jnp.minimum(mn, fold(s).min(axis=0)), jnp.maximum(mx, fold(s).max(axis=0))

    mn, mx = lax.fori_loop(0, nchunks, scores,
                           (jnp.full((32, tq), jnp.inf, _F32), jnp.full((32, tq), -jnp.inf, _F32)))
    smin = jnp.min(mn, axis=0, keepdims=True)
    smax = jnp.max(mx, axis=0, keepdims=True)

    def count_ge(t):
        def body(c, acc):
            return acc + fold(jnp.where(s_ref[c] >= t, 1.0, 0.0)).sum(axis=0)
        acc = lax.fori_loop(0, nchunks, body, jnp.zeros((32, tq), _F32))
        return jnp.sum(acc, axis=0, keepdims=True)

    def max_below(x):
        def body(c, acc):
            s = s_ref[c]
            return jnp.maximum(acc, fold(jnp.where(s < x, s, _NEG_INF)).max(axis=0))
        acc = lax.fori_loop(0, nchunks, body, jnp.full((32, tq), -jnp.inf, _F32))
        return jnp.max(acc, axis=0, keepdims=True)

    n_adm = (i * tq + 1 + lax.broadcasted_iota(jnp.int32, (1, tq), 1)).astype(_F32)
    thr, clo, chi, short = _topk_threshold(count_ge, max_below, smin, smax, n_adm, f0_ref[0],
                                           k_sel)

    need = jnp.logical_and(clo > kf, jnp.logical_not(short))

    @pl.when(jnp.max(jnp.where(need, 1.0, 0.0)) > 0.5)
    def _():
        keep = jnp.where(need, kf - chi, _FLT_MAX)
        rk = lax.broadcasted_iota(jnp.int32, (tk, tk), 0)
        ck = lax.broadcasted_iota(jnp.int32, (tk, tk), 1)
        earlier = jnp.where(rk > ck, 1.0, 0.0).astype(_MXU_DTYPE)

        def drop(c, seen):
            s = s_ref[c]
            tie = s == thr
            tm = jnp.where(tie, 1.0, 0.0)
            before = jnp.dot(earlier, tm.astype(_MXU_DTYPE), preferred_element_type=_F32) + seen
            s_ref[c] = jnp.where(jnp.where(tie, before, -1.0) >= keep, _NEG_INF, s)
            return seen + jnp.sum(fold(tm).sum(axis=0), axis=0, keepdims=True)

        lax.fori_loop(0, nchunks, drop, jnp.zeros((1, tq), _F32))

    m_ref[...] = jnp.full(m_ref.shape, -_FLT_MAX, _F32)
    l_ref[...] = jnp.zeros(l_ref.shape, _F32)
    acc_ref[...] = jnp.zeros(acc_ref.shape, _F32)
    wide = lambda a: jnp.concatenate([a[hb:hb + 1, :] for hb in range(N_HEADS)], axis=1)

    def set_bias(c):
        bias_ref[...] = jnp.where(s_ref[c] >= thr, 0.0, _NEG_INF)

    def chunk_probs(c, against_own_max):
        k_chunk = k_ref[pl.ds(pl.multiple_of(c * tk, tk), tk), :]
        slot = lax.rem(c, 2)
        for j in range(N_HEADS // 2):
            l2 = lax.dot_general(k_chunk, q8_ref[(2 * j) * tq:(2 * j + 2) * tq, :], nt_dims,
                                 preferred_element_type=_F32)
            for half in range(2):
                hb = 2 * j + half
                lh = l2[:, half * tq:(half + 1) * tq] + bias_ref[...]
                m_old = m_ref[hb:hb + 1, :]
                m_new = jnp.maximum(m_old, jnp.max(fold(lh).max(axis=0), axis=0, keepdims=True))
                if against_own_max:
                    p = jnp.exp(lh - m_new)
                    beta_ref[hb:hb + 1, :] = jnp.ones((1, tq), _F32)
                else:
                    p = jnp.exp(lh - m_old)
                    beta_ref[hb:hb + 1, :] = jnp.exp(m_old - m_new)
                    risk_ref[hb:hb + 1, :] = m_new - m_old
                pt_ref[slot, :, hb * tq:(hb + 1) * tq] = p.astype(pt_ref.dtype)
                psum_ref[hb:hb + 1, :] = jnp.sum(fold(p).sum(axis=0), axis=0, keepdims=True)
                mnew_ref[hb:hb + 1, :] = m_new

    def fold_in_stats():
        m_new = mnew_ref[...]
        alpha = jnp.exp(m_ref[...] - m_new)
        beta = beta_ref[...]
        l_ref[...] = l_ref[...] * alpha + psum_ref[...] * beta
        m_ref[...] = m_new
        scale_ref[0:1, :] = wide(alpha)
        scale_ref[1:2, :] = wide(beta)

    def apply_values(c):
        pv = jnp.dot(vt_ref[c], pt_ref[lax.rem(c, 2)], preferred_element_type=_F32)
        acc_ref[...] = acc_ref[...] * scale_ref[0:1, :] + pv * scale_ref[1:2, :]

    set_bias(0)
    chunk_probs(0, against_own_max=True)
    fold_in_stats()

    def attend(c, carry):
        set_bias(c)
        apply_values(c - 1)
        chunk_probs(c, against_own_max=False)

        @pl.when(jnp.max(risk_ref[...]) > MAX_EXP_ARG)
        def _():
            chunk_probs(c, against_own_max=True)

        fold_in_stats()
        return carry

    lax.fori_loop(1, nchunks, attend, 0)
    apply_values(nchunks - 1)

    l_all = jnp.concatenate([l_ref[hb:hb + 1, :] for hb in range(N_HEADS)], axis=1)
    o_t = acc_ref[...] / l_all
    blocks = []
    for j in range(N_HEADS // 2):
        mj = jnp.concatenate(
            [o_t[:HEAD_DIM, (2 * j) * tq:(2 * j + 1) * tq],
             o_t[HEAD_DIM:, (2 * j + 1) * tq:(2 * j + 2) * tq]], axis=0)
        blocks.append(mj.T)
    yb = jnp.concatenate(blocks, axis=1)
    out_ref[...] = (yb * zbs_ref[...]).astype(out_ref.dtype)


def _attn_prompt(qs, qi, wi, zbs, kibf, kbf, vt3, *, nb, t, tq, tk):
    nq = t // tq
    k_sel = min(TOP_K_MAX, t // 4)
    f0 = _seed_fraction(jnp.arange(1, t + 1), k_sel).reshape(nq, 1, tq)
    blk = lambda w: pl.BlockSpec((tq, w), lambda b, i: (b * nq + i, 0))
    res = lambda w: pl.BlockSpec((t, w), lambda b, i: (b, 0))
    stat = pltpu.VMEM((N_HEADS, tq), _F32)
    kernel = functools.partial(_attn_prompt_kernel, tq=tq, tk=tk, k_sel=k_sel)
    return pl.pallas_call(
        kernel,
        grid=(nb, nq),
        in_specs=[blk(D_B), blk(N_IDX_HEADS * D_IDX), blk(LANES), blk(D_B),
                  pl.BlockSpec((1, 1, tq), lambda b, i: (i, 0, 0)),
                  res(2 * D_IDX), res(KV_W),
                  pl.BlockSpec((t // tk, KV_W, tk), lambda b, i: (b, 0, 0))],
        out_specs=blk(D_B),
        out_shape=jax.ShapeDtypeStruct((nb * t, D_B), _MXU_DTYPE),
        scratch_shapes=[
            pltpu.VMEM((t // tk, tk, tq), _F32),
            pltpu.VMEM((N_HEADS * tq, LANES), _MXU_DTYPE),
            pltpu.VMEM((N_IDX_HEADS * tq, LANES), _MXU_DTYPE),
            pltpu.VMEM((tk, tq), _F32),
            pltpu.VMEM((2, tk, N_HEADS * tq), _MXU_DTYPE),
            pltpu.VMEM((KV_W, N_HEADS * tq), _F32),
            stat, stat,
            stat, stat, stat, stat,
            pltpu.VMEM((2, N_HEADS * tq), _F32),
        ],
        compiler_params=pltpu.CompilerParams(
            dimension_semantics=("arbitrary", "arbitrary"), vmem_limit_bytes=VMEM_LIMIT_BYTES),
        name="attn_prompt",
    )(qs, qi, wi, zbs, f0, kibf, kbf, vt3)


def _attn_sample_kernel(pt_ref, qs_ref, qi_ref, wi_ref, zbs_ref, kn_ref, vn_ref, kin_ref, f0_ref,
                        ck_hbm, cv_hbm, ci_hbm, out_ref,
                        kbuf, vbuf, ibuf, sems, s_ref, *, npages, tk, nchunks, k_sel):
    b = pl.program_id(0)
    nb = pl.num_programs(0)
    slot = lax.rem(b, 2)
    past = npages * PAGE_SIZE
    tq = qs_ref.shape[1]
    kf = jnp.float32(k_sel)

    def page_copies(bb, sl, p):
        phys = pt_ref[bb * npages + p]
        dst = pl.ds(pl.multiple_of(p * PAGE_SIZE, PAGE_SIZE), PAGE_SIZE)
        return (pltpu.make_async_copy(ck_hbm.at[phys], kbuf.at[sl, dst, :], sems.at[0, sl]),
                pltpu.make_async_copy(cv_hbm.at[phys], vbuf.at[sl, dst, :], sems.at[1, sl]),
                pltpu.make_async_copy(ci_hbm.at[phys], ibuf.at[sl, dst, :], sems.at[2, sl]))

    def start_pages(bb, sl):
        def body(p, carry):
            for cp in page_copies(bb, sl, p):
                cp.start()
            return carry
        lax.fori_loop(0, npages, body, 0)

    def wait_pages(bb, sl):
        def body(p, carry):
            for cp in page_copies(bb, sl, p):
                cp.wait()
            return carry
        lax.fori_loop(0, npages, body, 0)

    @pl.when(b == 0)
    def _():
        for sl in range(2):
            kbuf[sl, past:, :] = jnp.zeros((nchunks * tk - past, KV_W), kbuf.dtype)
            vbuf[sl, past:, :] = jnp.zeros((nchunks * tk - past, KV_W), vbuf.dtype)
            ibuf[sl, past:, :] = jnp.zeros((nchunks * tk - past, D_IDX), ibuf.dtype)
        start_pages(0, 0)

    @pl.when(b + 1 < nb)
    def _():
        start_pages(b + 1, 1 - slot)

    wait_pages(b, slot)
    def new_rows(ref):
        pad = jnp.zeros((NEW_ROWS - tq, ref.shape[2]), _F32)
        return jnp.concatenate([ref[0], pad], axis=0)

    kbuf[slot, past:past + NEW_ROWS, :] = new_rows(kn_ref).astype(kbuf.dtype)
    vbuf[slot, past:past + NEW_ROWS, :] = new_rows(vn_ref).astype(vbuf.dtype)
    ibuf[slot, past:past + NEW_ROWS, :] = new_rows(kin_ref).astype(ibuf.dtype)

    qs = qs_ref[0].astype(_F32)
    qi = qi_ref[0].astype(_F32)
    wi = wi_ref[0]
    lane = lax.broadcasted_iota(jnp.int32, (tq, LANES), 1)
    lo_half = lane < HEAD_DIM
    q8_rows = []
    for j in range(N_HEADS // 2):
        blk = qs[:, j * LANES:(j + 1) * LANES]
        q8_rows.append(jnp.where(lo_half, blk, 0.0))
        q8_rows.append(jnp.where(lo_half, 0.0, blk))
    q8 = jnp.concatenate(q8_rows, axis=0).astype(_MXU_DTYPE)
    qi8 = jnp.concatenate([qi[:, h * D_IDX:(h + 1) * D_IDX] for h in range(N_IDX_HEADS)],
                          axis=0).astype(_MXU_DTYPE)
    w_cols = [jnp.broadcast_to(wi[:, h:h + 1], (tq, tk)) for h in range(N_IDX_HEADS)]
    nt_dims = (((1,), (1,)), ((), ()))

    qrow = lax.broadcasted_iota(jnp.int32, (tq, tk), 0)
    kcol = lax.broadcasted_iota(jnp.int32, (tq, tk), 1)
    key_minus_query = kcol - qrow

    def scores(c, carry):
        mn, mx = carry
        k0 = pl.multiple_of(c * tk, tk)
        kic = ibuf[slot, pl.ds(k0, tk), :].astype(_MXU_DTYPE)
        r = lax.dot_general(qi8, kic, nt_dims, preferred_element_type=_F32)
        s = None
        for h in range(N_IDX_HEADS):
            t = jnp.maximum(r[h * tq:(h + 1) * tq, :], 0.0) * w_cols[h]
            s = t if s is None else s + t
        s_ref[c] = jnp.where(key_minus_query <= past - c * tk, s, _NEG_INF)
        return jnp.minimum(mn, s), jnp.maximum(mx, s)

    mn, mx = lax.fori_loop(0, nchunks, scores,
                           (jnp.full((tq, tk), jnp.inf, _F32), jnp.full((tq, tk), -jnp.inf, _F32)))
    smin = jnp.min(mn, axis=1, keepdims=True)
    smax = jnp.max(mx, axis=1, keepdims=True)

    across_keys = lambda col: jnp.broadcast_to(col, (tq, tk))

    def count_ge(t):
        t_wide = across_keys(t)

        def body(c, acc):
            return acc + jnp.where(s_ref[c] >= t_wide, 1.0, 0.0)
        acc = lax.fori_loop(0, nchunks, body, jnp.zeros((tq, tk), _F32))
        return jnp.sum(acc, axis=1, keepdims=True)

    def max_below(x):
        x_wide = across_keys(x)

        def body(c, acc):
            s = s_ref[c]
            return jnp.maximum(acc, jnp.where(s < x_wide, s, _NEG_INF))
        acc = lax.fori_loop(0, nchunks, body, jnp.full((tq, tk), -jnp.inf, _F32))
        return jnp.max(acc, axis=1, keepdims=True)

    n_adm = (past + 1 + lax.broadcasted_iota(jnp.int32, (tq, 1), 0)).astype(_F32)
    thr, clo, chi, short = _topk_threshold(count_ge, max_below, smin, smax, n_adm, f0_ref[...],
                                           k_sel)
    need = jnp.logical_and(clo > kf, jnp.logical_not(short))

    @pl.when(jnp.max(jnp.where(need, 1.0, 0.0)) > 0.5)
    def _():
        keep = jnp.where(need, kf - chi, _FLT_MAX)
        rk = lax.broadcasted_iota(jnp.int32, (tk, tk), 0)
        ck = lax.broadcasted_iota(jnp.int32, (tk, tk), 1)
        earlier = jnp.where(rk < ck, 1.0, 0.0).astype(_MXU_DTYPE)
        thr_ties = across_keys(thr)
        keep_wide = across_keys(keep)

        def drop(c, seen):
            s = s_ref[c]
            tie = s == thr_ties
            tm = jnp.where(tie, 1.0, 0.0)
            before = jnp.dot(tm.astype(_MXU_DTYPE), earlier, preferred_element_type=_F32) + seen
            s_ref[c] = jnp.where(jnp.where(tie, before, -1.0) >= keep_wide, _NEG_INF, s)
            return seen + jnp.sum(tm, axis=1, keepdims=True)

        lax.fori_loop(0, nchunks, drop, jnp.zeros((tq, 1), _F32))

    thr_wide = across_keys(thr)

    def attend(c, carry):
        m_old, l_old, acc = carry
        k0 = pl.multiple_of(c * tk, tk)
        kc = kbuf[slot, pl.ds(k0, tk), :].astype(_MXU_DTYPE)
        vc = vbuf[slot, pl.ds(k0, tk), :].astype(_MXU_DTYPE)
        lg = lax.dot_general(q8, kc, nt_dims, preferred_element_type=_F32)
        sel = s_ref[c] >= thr_wide
        lg = jnp.concatenate(
            [jnp.where(sel, lg[h * tq:(h + 1) * tq, :], _NEG_INF) for h in range(N_HEADS)], axis=0)
        m_new = jnp.maximum(m_old, jnp.max(lg, axis=1, keepdims=True))
        p = jnp.exp(lg - m_new)
        alpha = jnp.exp(m_old - m_new)
        l_new = alpha * l_old + jnp.sum(p, axis=1, keepdims=True)
        acc = alpha * acc + jnp.dot(p.astype(_MXU_DTYPE), vc, preferred_element_type=_F32)
        return m_new, l_new, acc

    init = (jnp.full((N_HEADS * tq, 1), -_FLT_MAX, _F32), jnp.zeros((N_HEADS * tq, 1), _F32),
            jnp.zeros((N_HEADS * tq, KV_W), _F32))
    _, l_fin, acc = lax.fori_loop(0, nchunks, attend, init)
    o = acc / l_fin
    blocks = []
    for j in range(N_HEADS // 2):
        blocks.append(jnp.where(lo_half, o[(2 * j) * tq:(2 * j + 1) * tq, :],
                                o[(2 * j + 1) * tq:(2 * j + 2) * tq, :]))
    yb = jnp.concatenate(blocks, axis=1)
    out_ref[0] = yb * zbs_ref[0]


def _attn_sample(page_table, qs, qi, wi, zbs, k_new, v_new, ki_new, cache_k, cache_v, cache_i,
                 *, tk):
    nb, npages = page_table.shape
    tq = qs.shape[0] // nb
    past = npages * PAGE_SIZE
    assert tq <= NEW_ROWS and past % NEW_ROWS == 0
    nchunks = -(-(past + NEW_ROWS) // tk)
    rows = nchunks * tk
    k_sel = min(TOP_K_MAX, (past + tq) // 4)
    three = lambda a: a.reshape(nb, tq, a.shape[-1])
    blk = lambda w: pl.BlockSpec((1, tq, w), lambda b, pt: (b, 0, 0))
    hbm = pl.BlockSpec(memory_space=pl.ANY)
    f0 = _seed_fraction(past + 1 + jnp.arange(tq), k_sel).reshape(tq, 1)
    kernel = functools.partial(_attn_sample_kernel, npages=npages, tk=tk, nchunks=nchunks,
                               k_sel=k_sel)
    grid_spec = pltpu.PrefetchScalarGridSpec(
        num_scalar_prefetch=1,
        grid=(nb,),
        in_specs=[blk(D_B), blk(N_IDX_HEADS * D_IDX), blk(LANES), blk(D_B),
                  blk(KV_W), blk(KV_W), blk(D_IDX),
                  pl.BlockSpec((tq, 1), lambda b, pt: (0, 0)), hbm, hbm, hbm],
        out_specs=blk(D_B),
        scratch_shapes=[
            pltpu.VMEM((2, rows, KV_W), cache_k.dtype),
            pltpu.VMEM((2, rows, KV_W), cache_v.dtype),
            pltpu.VMEM((2, rows, D_IDX), cache_i.dtype),
            pltpu.SemaphoreType.DMA((3, 2)),
            pltpu.VMEM((nchunks, tq, tk), _F32),
        ],
    )
    out = pl.pallas_call(
        kernel,
        grid_spec=grid_spec,
        out_shape=jax.ShapeDtypeStruct((nb, tq, D_B), _F32),
        compiler_params=pltpu.CompilerParams(
            dimension_semantics=("arbitrary",), vmem_limit_bytes=VMEM_LIMIT_BYTES),
        name="attn_sample",
    )(page_table.reshape(-1), three(qs), three(qi), three(wi), three(zbs), three(k_new),
      three(v_new), three(ki_new), f0, cache_k, cache_v, cache_i)
    return out.reshape(nb * tq, D_B)


def _outproj_kernel(yag_ref, ybg_ref, x_ref, w_ref, g_ref, out_ref):
    y = (jnp.dot(yag_ref[...], w_ref[:D_A, :], preferred_element_type=_F32)
         + jnp.dot(ybg_ref[...].astype(_MXU_DTYPE), w_ref[D_A:, :], preferred_element_type=_F32))
    ms = jnp.mean(y * y, axis=-1, keepdims=True)
    out_ref[...] = x_ref[...] + y * lax.rsqrt(ms + EPS) * g_ref[...]


def _outproj(yag, ybg, x2d, w_out_p, g_post, *, tm):
    r = x2d.shape[0]
    row = lambda w: pl.BlockSpec((tm, w), lambda i: (i, 0))
    const = lambda shape: pl.BlockSpec(shape, lambda i: (0,) * len(shape))
    return pl.pallas_call(
        _outproj_kernel,
        grid=(r // tm,),
        in_specs=[row(D_A), row(D_B), row(D_MODEL), const((D_A + D_B, D_MODEL)),
                  const((1, D_MODEL))],
        out_specs=row(D_MODEL),
        out_shape=jax.ShapeDtypeStruct((r, D_MODEL), _F32),
        compiler_params=pltpu.CompilerParams(
            dimension_semantics=("arbitrary",), vmem_limit_bytes=VMEM_LIMIT_BYTES),
        name="outproj",
    )(yag, ybg, x2d, w_out_p, g_post)


def _pad_in_weights(w):
    perm = _head_perm()
    nat = np.cumsum([0, D_A, D_A, D_A, D_B, KV_W, KV_W, D_B, N_IDX_HEADS * D_IDX, D_IDX,
                     N_IDX_HEADS])
    seg = lambda n: w[:, nat[n]:nat[n + 1]]
    pad = jnp.zeros((w.shape[0], C_END - C_WI - N_IDX_HEADS), w.dtype)
    return jnp.concatenate(
        [seg(0), seg(1), seg(2), seg(3)[:, perm], seg(4), seg(5), seg(6)[:, perm], seg(7),
         seg(8), seg(8), seg(9), pad], axis=1).astype(_MXU_DTYPE)


def _layer(x_prompt, x_sample, cache_k, cache_v, cache_idx_k, page_table, g_pre, w_in, ln_v_g,
           ln_v_b, w_s, b_s, w_out, g_post):
    nb, t, _ = x_prompt.shape
    nd, td, _ = x_sample.shape
    npages = page_table.shape[1]
    past = npages * PAGE_SIZE
    tm = 1024
    tq = 128
    tk = 1024
    tk_sample = 512

    w_pad = _pad_in_weights(w_in)
    perm = _head_perm()
    w_out_p = jnp.concatenate([w_out[:D_A], w_out[D_A:][perm]], axis=0).astype(_MXU_DTYPE)
    gpre = g_pre.reshape(1, D_MODEL)
    gpost = g_post.reshape(1, D_MODEL)
    lng = ln_v_g.reshape(1, D_A)
    lnb = ln_v_b.reshape(1, D_A)
    tril = jnp.tril(jnp.ones((CHUNK, CHUNK), dtype=bool))
    ws_tril = jnp.where(tril[None], w_s, jnp.zeros_like(w_s))

    xp = x_prompt.reshape(nb * t, D_MODEL)
    tpb = t // tm
    wmix_p = ws_tril.astype(_MXU_DTYPE)
    bmix_p = jnp.repeat(b_s.T, A_GROUP_DIM, axis=1)
    outs = _inproj(xp, _rope_tables(jnp.arange(t, dtype=jnp.int32)), lambda i: (i % tpb, 0),
                   gpre, w_pad, lng, lnb, wmix_p, bmix_p, tm=tm, state_rows=CHUNK,
                   state_shape=(nb, CHUNK, D_A), state_index=lambda i: (i // tpb, 0, 0))
    yag, zbs, qs, qi, k32, v32, ki32, wi, kbf, vt3, kibf, vstate_p = outs
    ybg = _attn_prompt(qs, qi, wi, zbs, kibf, kbf, vt3, nb=nb, t=t, tq=tq, tk=tk)
    y_prompt = _outproj(yag, ybg, xp, w_out_p, gpost, tm=tm).reshape(nb, t, D_MODEL)

    xs = x_sample.reshape(nd * td, D_MODEL)
    reps = CHUNK // td
    eye = jnp.eye(reps, dtype=w_s.dtype)
    wmix_s = jnp.stack([jnp.kron(eye, ws_tril[g, :td, :td]) for g in range(N_A_GROUPS)]
                       ).astype(_MXU_DTYPE)
    bmix_s = jnp.repeat(jnp.tile(b_s[:, :td], (1, reps)).T, A_GROUP_DIM, axis=1)
    tm_s = min(tm, nd * td)
    pos_s = past + (jnp.arange(tm_s, dtype=jnp.int32) % td)
    outs = _inproj(xs, _rope_tables(pos_s), lambda i: (0, 0), gpre, w_pad, lng, lnb, wmix_s,
                   bmix_s, tm=tm_s, state_rows=tm_s, state_shape=(nd * td, D_A),
                   state_index=lambda i: (i, 0))
    yag_s, zbs_s, qs_s, qi_s, k32_s, v32_s, ki32_s, wi_s, _, _, _, vstate_s = outs
    pool = cache_k.shape[0]
    ybg_s = _attn_sample(page_table, qs_s, qi_s, wi_s, zbs_s, k32_s, v32_s, ki32_s,
                         cache_k.reshape(pool, PAGE_SIZE, KV_W).astype(_MXU_DTYPE),
                         cache_v.reshape(pool, PAGE_SIZE, KV_W).astype(_MXU_DTYPE),
                         cache_idx_k.astype(_MXU_DTYPE), tk=tk_sample)
    y_sample = _outproj(yag_s, ybg_s, xs, w_out_p, gpost, tm=tm_s).reshape(nd, td, D_MODEL)

    return (y_prompt, y_sample,
            k32.reshape(nb, t, N_KV_HEADS, HEAD_DIM), v32.reshape(nb, t, N_KV_HEADS, HEAD_DIM),
            ki32.reshape(nb, t, D_IDX), vstate_p,
            k32_s.reshape(nd, td, N_KV_HEADS, HEAD_DIM), v32_s.reshape(nd, td, N_KV_HEADS, HEAD_DIM),
            ki32_s.reshape(nd, td, D_IDX), vstate_s.reshape(nd, td, D_A))


def kernel(x_prompt, x_sample, cache_k, cache_v, cache_idx_k, page_table, g_pre, w_in, ln_v_g,
           ln_v_b, w_s, b_s, w_out, g_post):
    xp, xs = x_prompt, x_sample
    per_layer = []
    for l in range(g_pre.shape[0]):
        outs = _layer(xp, xs, cache_k[l], cache_v[l], cache_idx_k[l], page_table, g_pre[l],
                      w_in[l], ln_v_g[l], ln_v_b[l], w_s[l], b_s[l], w_out[l], g_post[l])
        xp, xs = outs[0], outs[1]
        per_layer.append(outs[2:])
    stacked = tuple(jnp.stack(leaves, 0) for leaves in zip(*per_layer))
    return (xp, xs) + stacked
```

```python
import functools

import numpy as np
import jax
import jax.numpy as jnp
from jax import lax
from jax.scipy.special import ndtri
from jax.experimental import pallas as pl
from jax.experimental.pallas import tpu as pltpu

_F32 = jnp.float32
_MXU_DTYPE = jnp.bfloat16

D_MODEL = 1024
D_A = 512
D_B = 512
CHUNK = 128
N_A_GROUPS = 4
A_GROUP_DIM = D_A // N_A_GROUPS
HEAD_DIM = 64
N_HEADS = 8
N_KV_HEADS = 2
KV_W = N_KV_HEADS * HEAD_DIM
ROT_DIM = HEAD_DIM // 4
N_IDX_HEADS = 8
D_IDX = 64
TOP_K_MAX = 256
ROPE_THETA = 500000.0
EPS = 1e-6
PAGE_SIZE = 128

LANES = 128
VMEM_LIMIT_BYTES = 56 * 1024 * 1024

C_U, C_V, C_ZA, C_Q, C_K, C_VV, C_ZB, C_QI, C_KI, C_WI, C_END = (
    0, 512, 1024, 1536, 2048, 2176, 2304, 2816, 3328, 3456, 3584)

_FLT_MAX = float(np.finfo(np.float32).max)
_NEG_INF = float("-inf")
_INT_MIN = -2 ** 31


def _head_perm():
    c = np.arange(N_HEADS * HEAD_DIM)
    j = c // LANES
    half = (c % LANES) // HEAD_DIM
    d = c % HEAD_DIM
    return (j + 4 * half) * HEAD_DIM + d


def _rope_tables(pos):
    half = ROT_DIM // 2
    inv = jnp.power(jnp.float32(ROPE_THETA), -jnp.arange(half, dtype=_F32) * 2.0 / ROT_DIM)
    ang = pos.astype(_F32)[:, None] * inv[None, :]
    cos = jnp.cos(ang)
    sin = jnp.sin(ang)
    l64 = np.arange(LANES) % HEAD_DIM
    fidx = l64 % half
    in_rot = (l64 < ROT_DIM)[None, :]
    first = (l64 < half)[None, :]
    second = ((l64 >= half) & (l64 < ROT_DIM))[None, :]
    cos_t = jnp.where(in_rot, cos[:, fidx], 1.0)
    sin_a = jnp.where(first, -sin[:, fidx], 0.0)
    sin_b = jnp.where(second, sin[:, fidx], 0.0)
    return cos_t, sin_a, sin_b


def _silu(z):
    return z * (1.0 / (1.0 + jnp.exp(-z)))


def _inproj_kernel(x_ref, cos_ref, sa_ref, sb_ref, gpre_ref, win_ref, lng_ref, lnb_ref,
                   wmix_ref, bmix_ref,
                   yag_ref, zbs_ref, qs_ref, qi_ref, k32_ref, v32_ref, ki32_ref, wi_ref,
                   kbf_ref, vt_ref, kibf_ref, vstate_ref, *, tm, state_rows):
    x = x_ref[...]
    ms = jnp.mean(x * x, axis=-1, keepdims=True)
    h = (x * lax.rsqrt(ms + EPS) * gpre_ref[...]).astype(_MXU_DTYPE)

    def proj(a, b):
        return jnp.dot(h, win_ref[:, a:b], preferred_element_type=_F32)

    cos_t = cos_ref[...]
    sin_a = sa_ref[...]
    sin_b = sb_ref[...]

    def rope(p):
        blocks = []
        for j in range(p.shape[1] // LANES):
            pj = p[:, j * LANES:(j + 1) * LANES]
            blocks.append(pj * cos_t
                          + pltpu.roll(pj, LANES - ROT_DIM // 2, 1) * sin_a
                          + pltpu.roll(pj, ROT_DIM // 2, 1) * sin_b)
        return blocks[0] if len(blocks) == 1 else jnp.concatenate(blocks, axis=1)

    u = proj(C_U, C_V)
    v = proj(C_V, C_ZA)
    mu = jnp.mean(v, axis=-1, keepdims=True)
    vc = v - mu
    var = jnp.mean(vc * vc, axis=-1, keepdims=True)
    vn = vc * lax.rsqrt(var + EPS) * lng_ref[...] + lnb_ref[...]
    vstate_ref[...] = vn[tm - state_rows:, :].reshape(vstate_ref.shape)
    vnb = vn.astype(_MXU_DTYPE)
    bmix = bmix_ref[...]
    rows = []
    for c in range(tm // CHUNK):
        cols = []
        for g in range(N_A_GROUPS):
            blk = vnb[c * CHUNK:(c + 1) * CHUNK, g * A_GROUP_DIM:(g + 1) * A_GROUP_DIM]
            cols.append(jnp.dot(wmix_ref[g], blk, preferred_element_type=_F32)
                        + bmix[:, g * A_GROUP_DIM:(g + 1) * A_GROUP_DIM])
        rows.append(jnp.concatenate(cols, axis=1))
    mixed = jnp.concatenate(rows, axis=0)
    za = proj(C_ZA, C_Q)
    yag_ref[...] = ((u * mixed) * _silu(za)).astype(yag_ref.dtype)

    q = rope(proj(C_Q, C_K))
    qs_ref[...] = (q * (HEAD_DIM ** -0.5)).astype(qs_ref.dtype)
    k = rope(proj(C_K, C_VV))
    k32_ref[...] = k
    kbf_ref[...] = k.astype(kbf_ref.dtype)
    vv = proj(C_VV, C_ZB)
    v32_ref[...] = vv
    vt_ref[0] = vv.T.astype(vt_ref.dtype)
    zb = proj(C_ZB, C_QI)
    zbs_ref[...] = _silu(zb)
    qi = rope(proj(C_QI, C_KI))
    qi_ref[...] = qi.astype(qi_ref.dtype)
    ki = rope(proj(C_KI, C_WI))
    ki32_ref[...] = ki[:, :D_IDX]
    kibf_ref[...] = ki.astype(kibf_ref.dtype)
    wi_ref[...] = proj(C_WI, C_END) * (N_IDX_HEADS ** -0.5)


def _inproj(x2d, tables, table_index, gpre, w_pad, lng, lnb, wmix, bmix, *, tm, state_rows,
            state_shape, state_index):
    r = x2d.shape[0]
    nt = r // tm
    cos_t, sin_a, sin_b = tables
    row = lambda w: pl.BlockSpec((tm, w), lambda i: (i, 0))
    const = lambda shape: pl.BlockSpec(shape, lambda i: (0,) * len(shape))
    tab = pl.BlockSpec((tm, LANES), table_index)
    out_shapes = (
        jax.ShapeDtypeStruct((r, D_A), _MXU_DTYPE),
        jax.ShapeDtypeStruct((r, D_B), _F32),
        jax.ShapeDtypeStruct((r, D_B), _MXU_DTYPE),
        jax.ShapeDtypeStruct((r, N_IDX_HEADS * D_IDX), _MXU_DTYPE),
        jax.ShapeDtypeStruct((r, KV_W), _F32),
        jax.ShapeDtypeStruct((r, KV_W), _F32),
        jax.ShapeDtypeStruct((r, D_IDX), _F32),
        jax.ShapeDtypeStruct((r, LANES), _F32),
        jax.ShapeDtypeStruct((r, KV_W), _MXU_DTYPE),
        jax.ShapeDtypeStruct((nt, KV_W, tm), _MXU_DTYPE),
        jax.ShapeDtypeStruct((r, 2 * D_IDX), _MXU_DTYPE),
        jax.ShapeDtypeStruct(state_shape, _F32),
    )
    out_specs = (
        row(D_A), row(D_B), row(D_B), row(N_IDX_HEADS * D_IDX), row(KV_W), row(KV_W),
        row(D_IDX), row(LANES), row(KV_W),
        pl.BlockSpec((1, KV_W, tm), lambda i: (i, 0, 0)),
        row(2 * D_IDX),
        pl.BlockSpec((1,) * (len(state_shape) - 2) + (state_rows, D_A), state_index),
    )
    return pl.pallas_call(
        functools.partial(_inproj_kernel, tm=tm, state_rows=state_rows),
        grid=(nt,),
        in_specs=[row(D_MODEL), tab, tab, tab, const((1, D_MODEL)), const((D_MODEL, C_END)),
                  const((1, D_A)), const((1, D_A)), const((N_A_GROUPS, CHUNK, CHUNK)),
                  const((CHUNK, D_A))],
        out_specs=out_specs,
        out_shape=out_shapes,
        compiler_params=pltpu.CompilerParams(
            dimension_semantics=("arbitrary",), vmem_limit_bytes=VMEM_LIMIT_BYTES),
        name="inproj",
    )(x2d, cos_t, sin_a, sin_b, gpre, w_pad, lng, lnb, wmix, bmix)


def _f32_to_key(x):
    b = lax.bitcast_convert_type(x, jnp.int32)
    return jnp.where(b >= 0, b, b ^ jnp.int32(0x7FFFFFFF))


def _key_to_f32(k):
    return lax.bitcast_convert_type(jnp.where(k >= 0, k, k ^ jnp.int32(0x7FFFFFFF)), _F32)


N_INTERP_PASSES = 10
MAX_EXP_ARG = 40.0


def _topk_threshold(count_ge, max_below, smin, smax, n_adm, f0, k_sel):
    kf = jnp.float32(k_sel)
    log_k = jnp.log(kf - 0.5)
    lo0 = _f32_to_key(smin)
    hi0 = _f32_to_key(smax) + 1
    short = n_adm <= kf
    done0 = jnp.where(jnp.logical_or(short, hi0 <= lo0 + 1), 1.0, 0.0)
    g_hi0 = jnp.full(smin.shape, np.log(0.5), _F32) - log_k

    def all_done(done):
        return jnp.min(done) > 0.5

    def interp_cond(carry):
        return jnp.logical_and(carry[0] < N_INTERP_PASSES, jnp.logical_not(all_done(carry[-1])))

    def probe(carry, frac):
        it, lo, hi, clo, chi, glo, ghi, last, done = carry
        xlo = _key_to_f32(lo)
        xhi = _key_to_f32(hi)
        cand = jnp.clip(_f32_to_key(xlo + (xhi - xlo) * frac), lo + 1, hi - 1)
        cnt = count_ge(_key_to_f32(cand))
        g = jnp.log(jnp.maximum(cnt, 0.5)) - log_k
        active = done < 0.5
        new_lo = jnp.logical_and(active, cnt >= kf)
        new_hi = jnp.logical_and(active, cnt < kf)
        ghi = jnp.where(jnp.logical_and(new_lo, last == 1), ghi * 0.5, ghi)
        glo = jnp.where(jnp.logical_and(new_hi, last == 2), glo * 0.5, glo)
        lo = jnp.where(new_lo, cand, lo)
        clo = jnp.where(new_lo, cnt, clo)
        glo = jnp.where(new_lo, g, glo)
        hi = jnp.where(new_hi, cand, hi)
        chi = jnp.where(new_hi, cnt, chi)
        ghi = jnp.where(new_hi, g, ghi)
        last = jnp.where(new_lo, 1, jnp.where(new_hi, 2, last))
        done = jnp.where(jnp.logical_or(clo == kf, hi <= lo + 1), 1.0, done)
        return it + 1, lo, hi, clo, chi, glo, ghi, last, done

    def interp_body(carry):
        return probe(carry, carry[5] / (carry[5] - carry[6]))

    init = (jnp.int32(0), lo0, hi0, n_adm, jnp.zeros_like(n_adm),
            jnp.log(jnp.maximum(n_adm, 1.0)) - log_k, g_hi0, jnp.zeros(smin.shape, jnp.int32),
            done0)
    state = lax.while_loop(interp_cond, interp_body, probe(init, f0))
    _, lo, hi, clo, chi, _, _, _, done = state

    def peel_cond(carry):
        return jnp.logical_not(all_done(carry[-1]))

    def peel_body(carry):
        lo, hi, clo, chi, done = carry
        top = max_below(_key_to_f32(hi))
        cnt = count_ge(top)
        active = done < 0.5
        hit = jnp.logical_and(active, cnt >= kf)
        miss = jnp.logical_and(active, cnt < kf)
        lo = jnp.where(hit, _f32_to_key(top), lo)
        clo = jnp.where(hit, cnt, clo)
        hi = jnp.where(miss, _f32_to_key(top), hi)
        chi = jnp.where(miss, cnt, chi)
        done = jnp.where(hit, 1.0, done)
        return lo, hi, clo, chi, done

    lo, _, clo, chi, _ = lax.while_loop(peel_cond, peel_body, (lo, hi, clo, chi, done))
    thr = jnp.where(short, -_FLT_MAX, _key_to_f32(lo))
    return thr, clo, chi, short


def _seed_fraction(n_adm, k_sel):
    n = n_adm.astype(_F32)
    z_k = ndtri(1.0 - jnp.minimum(k_sel / n, 0.999))
    z_n = ndtri(1.0 - 1.0 / (n + 1.0))
    return (0.5 + 0.5 * z_k / z_n).astype(_F32)


def _attn_prompt_kernel(qs_ref, qi_ref, wi_ref, zbs_ref, f0_ref, ki_ref, k_ref, vt_ref, out_ref,
                        s_ref, q8_ref, qi8_ref, bias_ref, pt_ref, acc_ref, m_ref, l_ref,
                        mnew_ref, beta_ref, psum_ref, risk_ref, scale_ref, *, tq, tk, k_sel):
    i = pl.program_id(1)
    ratio = tk // tq
    nchunks = lax.div(i + ratio, ratio)
    kf = jnp.float32(k_sel)

    lane = lax.broadcasted_iota(jnp.int32, (tq, LANES), 1)
    lo_half = lane < HEAD_DIM
    for j in range(N_HEADS // 2):
        for src, dst in ((qs_ref, q8_ref), (qi_ref, qi8_ref)):
            blk = src[:, j * LANES:(j + 1) * LANES].astype(_F32)
            dst[(2 * j) * tq:(2 * j + 1) * tq, :] = jnp.where(lo_half, blk, 0.0).astype(dst.dtype)
            dst[(2 * j + 1) * tq:(2 * j + 2) * tq, :] = jnp.where(lo_half, 0.0, blk).astype(dst.dtype)
    w_t = wi_ref[...].T

    row = lax.broadcasted_iota(jnp.int32, (tk, tq), 0)
    col = lax.broadcasted_iota(jnp.int32, (tk, tq), 1)
    key_minus_query = row - col
    nt_dims = (((1,), (1,)), ((), ()))
    fold = lambda x: x.reshape(tk // 32, 32, tq)

    def scores(c, carry):
        mn, mx = carry
        k0 = pl.multiple_of(c * tk, tk)
        rt = lax.dot_general(ki_ref[pl.ds(k0, tk), :], qi8_ref[...], nt_dims,
                             preferred_element_type=_F32)
        s = None
        for hb in range(N_IDX_HEADS):
            t = jnp.maximum(rt[:, hb * tq:(hb + 1) * tq], 0.0) * w_t[hb:hb + 1, :]
            s = t if s is None else s + t
        s_ref[c] = jnp.where(key_minus_query <= i * tq - c * tk, s, _NEG_INF)
        return jnp.minimum(mn, fold(s).min(axis=0)), jnp.maximum(mx, fold(s).max(axis=0))

    mn, mx = lax.fori_loop(0, nchunks, scores,
                           (jnp.full((32, tq), jnp.inf, _F32), jnp.full((32, tq), -jnp.inf, _F32)))
    smin = jnp.min(mn, axis=0, keepdims=True)
    smax = jnp.max(mx, axis=0, keepdims=True)

    def count_ge(t):
        def body(c, acc):
            return acc + fold(jnp.where(s_ref[c] >= t, 1.0, 0.0)).sum(axis=0)
        acc = lax.fori_loop(0, nchunks, body, jnp.zeros((32, tq), _F32))
        return jnp.sum(acc, axis=0, keepdims=True)

    def max_below(x):
        def body(c, acc):
            s = s_ref[c]
            return jnp.maximum(acc, fold(jnp.where(s < x, s, _NEG_INF)).max(axis=0))
        acc = lax.fori_loop(0, nchunks, body, jnp.full((32, tq), -jnp.inf, _F32))
        return jnp.max(acc, axis=0, keepdims=True)

    n_adm = (i * tq + 1 + lax.broadcasted_iota(jnp.int32, (1, tq), 1)).astype(_F32)
    thr, clo, chi, short = _topk_threshold(count_ge, max_below, smin, smax, n_adm, f0_ref[0],
                                           k_sel)

    need = jnp.logical_and(clo > kf, jnp.logical_not(short))

    @pl.when(jnp.max(jnp.where(need, 1.0, 0.0)) > 0.5)
    def _():
        keep = jnp.where(need, kf - chi, _FLT_MAX)
        rk = lax.broadcasted_iota(jnp.int32, (tk, tk), 0)
        ck = lax.broadcasted_iota(jnp.int32, (tk, tk), 1)
        earlier = jnp.where(rk > ck, 1.0, 0.0).astype(_MXU_DTYPE)

        def drop(c, seen):
            s = s_ref[c]
            tie = s == thr
            tm = jnp.where(tie, 1.0, 0.0)
            before = jnp.dot(earlier, tm.astype(_MXU_DTYPE), preferred_element_type=_F32) + seen
            s_ref[c] = jnp.where(jnp.where(tie, before, -1.0) >= keep, _NEG_INF, s)
            return seen + jnp.sum(fold(tm).sum(axis=0), axis=0, keepdims=True)

        lax.fori_loop(0, nchunks, drop, jnp.zeros((1, tq), _F32))

    m_ref[...] = jnp.full(m_ref.shape, -_FLT_MAX, _F32)
    l_ref[...] = jnp.zeros(l_ref.shape, _F32)
    acc_ref[...] = jnp.zeros(acc_ref.shape, _F32)
    wide = lambda a: jnp.concatenate([a[hb:hb + 1, :] for hb in range(N_HEADS)], axis=1)

    def set_bias(c):
        bias_ref[...] = jnp.where(s_ref[c] >= thr, 0.0, _NEG_INF)

    def chunk_probs(c, against_own_max):
        k_chunk = k_ref[pl.ds(pl.multiple_of(c * tk, tk), tk), :]
        slot = lax.rem(c, 2)
        for j in range(N_HEADS // 2):
            l2 = lax.dot_general(k_chunk, q8_ref[(2 * j) * tq:(2 * j + 2) * tq, :], nt_dims,
                                 preferred_element_type=_F32)
            for half in range(2):
                hb = 2 * j + half
                lh = l2[:, half * tq:(half + 1) * tq] + bias_ref[...]
                m_old = m_ref[hb:hb + 1, :]
                m_new = jnp.maximum(m_old, jnp.max(fold(lh).max(axis=0), axis=0, keepdims=True))
                if against_own_max:
                    p = jnp.exp(lh - m_new)
                    beta_ref[hb:hb + 1, :] = jnp.ones((1, tq), _F32)
                else:
                    p = jnp.exp(lh - m_old)
                    beta_ref[hb:hb + 1, :] = jnp.exp(m_old - m_new)
                    risk_ref[hb:hb + 1, :] = m_new - m_old
                pt_ref[slot, :, hb * tq:(hb + 1) * tq] = p.astype(pt_ref.dtype)
                psum_ref[hb:hb + 1, :] = jnp.sum(fold(p).sum(axis=0), axis=0, keepdims=True)
                mnew_ref[hb:hb + 1, :] = m_new

    def fold_in_stats():
        m_new = mnew_ref[...]
        alpha = jnp.exp(m_ref[...] - m_new)
        beta = beta_ref[...]
        l_ref[...] = l_ref[...] * alpha + psum_ref[...] * beta
        m_ref[...] = m_new
        scale_ref[0:1, :] = wide(alpha)
        scale_ref[1:2, :] = wide(beta)

    def apply_values(c):
        pv = jnp.dot(vt_ref[c], pt_ref[lax.rem(c, 2)], preferred_element_type=_F32)
        acc_ref[...] = acc_ref[...] * scale_ref[0:1, :] + pv * scale_ref[1:2, :]

    set_bias(0)
    chunk_probs(0, against_own_max=True)
    fold_in_stats()

    def attend(c, carry):
        set_bias(c)
        apply_values(c - 1)
        chunk_probs(c, against_own_max=False)

        @pl.when(jnp.max(risk_ref[...]) > MAX_EXP_ARG)
        def _():
            chunk_probs(c, against_own_max=True)

        fold_in_stats()
        return carry

    lax.fori_loop(1, nchunks, attend, 0)
    apply_values(nchunks - 1)

    l_all = jnp.concatenate([l_ref[hb:hb + 1, :] for hb in range(N_HEADS)], axis=1)
    o_t = acc_ref[...] / l_all
    blocks = []
    for j in range(N_HEADS // 2):
        mj = jnp.concatenate(
            [o_t[:HEAD_DIM, (2 * j) * tq:(2 * j + 1) * tq],
             o_t[HEAD_DIM:, (2 * j + 1) * tq:(2 * j + 2) * tq]], axis=0)
        blocks.append(mj.T)
    yb = jnp.concatenate(blocks, axis=1)
    out_ref[...] = (yb * zbs_ref[...]).astype(out_ref.dtype)


def _attn_prompt(qs, qi, wi, zbs, kibf, kbf, vt3, *, nb, t, tq, tk):
    nq = t // tq
    k_sel = min(TOP_K_MAX, t // 4)
    f0 = _seed_fraction(jnp.arange(1, t + 1), k_sel).reshape(nq, 1, tq)
    blk = lambda w: pl.BlockSpec((tq, w), lambda b, i: (b * nq + i, 0))
    res = lambda w: pl.BlockSpec((t, w), lambda b, i: (b, 0))
    stat = pltpu.VMEM((N_HEADS, tq), _F32)
    kernel = functools.partial(_attn_prompt_kernel, tq=tq, tk=tk, k_sel=k_sel)
    return pl.pallas_call(
        kernel,
        grid=(nb, nq),
        in_specs=[blk(D_B), blk(N_IDX_HEADS * D_IDX), blk(LANES), blk(D_B),
                  pl.BlockSpec((1, 1, tq), lambda b, i: (i, 0, 0)),
                  res(2 * D_IDX), res(KV_W),
                  pl.BlockSpec((t // tk, KV_W, tk), lambda b, i: (b, 0, 0))],
        out_specs=blk(D_B),
        out_shape=jax.ShapeDtypeStruct((nb * t, D_B), _MXU_DTYPE),
        scratch_shapes=[
            pltpu.VMEM((t // tk, tk, tq), _F32),
            pltpu.VMEM((N_HEADS * tq, LANES), _MXU_DTYPE),
            pltpu.VMEM((N_IDX_HEADS * tq, LANES), _MXU_DTYPE),
            pltpu.VMEM((tk, tq), _F32),
            pltpu.VMEM((2, tk, N_HEADS * tq), _MXU_DTYPE),
            pltpu.VMEM((KV_W, N_HEADS * tq), _F32),
            stat, stat,
            stat, stat, stat, stat,
            pltpu.VMEM((2, N_HEADS * tq), _F32),
        ],
        compiler_params=pltpu.CompilerParams(
            dimension_semantics=("arbitrary", "arbitrary"), vmem_limit_bytes=VMEM_LIMIT_BYTES),
        name="attn_prompt",
    )(qs, qi, wi, zbs, f0, kibf, kbf, vt3)


def _stream_pages(pt_ref, b, nb, npages, sources, buffers, sems):
    slot = lax.rem(b, 2)

    def copies(bb, sl, p):
        phys = pt_ref[bb * npages + p]
        dst = pl.ds(pl.multiple_of(p * PAGE_SIZE, PAGE_SIZE), PAGE_SIZE)
        return [pltpu.make_async_copy(src.at[phys], buf.at[sl, :, dst], sems.at[i, sl])
                for i, (src, buf) in enumerate(zip(sources, buffers))]

    def start(bb, sl):
        def body(p, carry):
            for cp in copies(bb, sl, p):
                cp.start()
            return carry
        lax.fori_loop(0, npages, body, 0)

    def wait(bb, sl):
        def body(p, carry):
            for cp in copies(bb, sl, p):
                cp.wait()
            return carry
        lax.fori_loop(0, npages, body, 0)

    @pl.when(b == 0)
    def _():
        past = npages * PAGE_SIZE
        for buf in buffers:
            for sl in range(2):
                buf[sl, :, past:] = jnp.zeros((buf.shape[1], buf.shape[2] - past), buf.dtype)
        start(0, 0)

    @pl.when(b + 1 < nb)
    def _():
        start(b + 1, 1 - slot)

    wait(b, slot)
    return slot


def _put_new_keys(buf, slot, past, rows):
    tq, feat = rows.shape
    tile = jnp.concatenate([rows, jnp.zeros((PAGE_SIZE - tq, feat), _F32)], axis=0)
    if feat < LANES:
        tile = jnp.concatenate([tile, jnp.zeros((PAGE_SIZE, LANES - feat), _F32)], axis=1)
    buf[slot, :, past:past + PAGE_SIZE] = tile.T[:feat, :].astype(buf.dtype)


def _sample_scores_kernel(pt_ref, qi_ref, wi_ref, kin_ref, ci_hbm, s_ref, ibuf, sems,
                          *, npages, tk, nchunks):
    b = pl.program_id(0)
    past = npages * PAGE_SIZE
    tq = qi_ref.shape[1]
    slot = _stream_pages(pt_ref, b, pl.num_programs(0), npages, [ci_hbm], [ibuf], sems)
    _put_new_keys(ibuf, slot, past, kin_ref[0])

    qi = qi_ref[0].astype(_F32)
    wi = wi_ref[0]
    qi8 = jnp.concatenate([qi[:, h * D_IDX:(h + 1) * D_IDX] for h in range(N_IDX_HEADS)],
                          axis=0).astype(_MXU_DTYPE)
    w_wide = [jnp.broadcast_to(wi[:, h:h + 1], (tq, tk)) for h in range(N_IDX_HEADS)]
    key_minus_query = (lax.broadcasted_iota(jnp.int32, (tq, tk), 1)
                       - lax.broadcasted_iota(jnp.int32, (tq, tk), 0))
    for c in range(nchunks):
        keys_t = ibuf[slot, :, c * tk:(c + 1) * tk].astype(_MXU_DTYPE)
        r = jnp.dot(qi8, keys_t, preferred_element_type=_F32)
        s = None
        for h in range(N_IDX_HEADS):
            t = jnp.maximum(r[h * tq:(h + 1) * tq, :], 0.0) * w_wide[h]
            s = t if s is None else s + t
        s_ref[0, c] = jnp.where(key_minus_query <= past - c * tk, s, _NEG_INF)


def _sample_threshold_kernel(s_ref, f0_ref, nadm_ref, out_ref, thr_ref, *, tk, nchunks, k_sel):
    rows = s_ref.shape[2]
    kf = jnp.float32(k_sel)
    groups = tk // LANES
    tile = lambda col: jnp.broadcast_to(col, (rows, tk))

    def fold_with(x, op):
        out = x[:, :LANES]
        for g in range(1, groups):
            out = op(out, x[:, g * LANES:(g + 1) * LANES])
        return out

    def over_chunks(fn, op, init):
        def body(c, acc):
            return op(acc, fold_with(fn(s_ref[0, c]), op))
        return lax.fori_loop(0, nchunks, body, jnp.full((rows, LANES), init, _F32))

    def count_ge(t):
        t_wide = tile(t)
        acc = over_chunks(lambda s: jnp.where(s >= t_wide, 1.0, 0.0), jnp.add, 0.0)
        return jnp.sum(acc, axis=1, keepdims=True)

    def max_below(x):
        x_wide = tile(x)
        acc = over_chunks(lambda s: jnp.where(s < x_wide, s, _NEG_INF), jnp.maximum, -jnp.inf)
        return jnp.max(acc, axis=1, keepdims=True)

    finite_min = over_chunks(lambda s: jnp.where(s > _NEG_INF, s, jnp.inf), jnp.minimum, jnp.inf)
    finite_max = over_chunks(lambda s: s, jnp.maximum, -jnp.inf)
    smin = jnp.min(finite_min, axis=1, keepdims=True)
    smax = jnp.max(finite_max, axis=1, keepdims=True)
    thr, clo, chi, short = _topk_threshold(count_ge, max_below, smin, smax, nadm_ref[...],
                                           f0_ref[...], k_sel)
    thr_ref[...] = thr
    need = jnp.logical_and(clo > kf, jnp.logical_not(short))
    any_ties = jnp.max(jnp.where(need, 1.0, 0.0)) > 0.5

    @pl.when(jnp.logical_not(any_ties))
    def _():
        out_ref[...] = s_ref[...]

    @pl.when(any_ties)
    def _():
        keep_wide = tile(jnp.where(need, kf - chi, _FLT_MAX))
        thr_wide = tile(thr)
        rk = lax.broadcasted_iota(jnp.int32, (tk, tk), 0)
        ck = lax.broadcasted_iota(jnp.int32, (tk, tk), 1)
        earlier = jnp.where(rk < ck, 1.0, 0.0).astype(_MXU_DTYPE)

        def drop(c, seen):
            s = s_ref[0, c]
            tie = s == thr_wide
            tm = jnp.where(tie, 1.0, 0.0)
            before = (jnp.dot(tm.astype(_MXU_DTYPE), earlier, preferred_element_type=_F32)
                      + tile(seen))
            out_ref[0, c] = jnp.where(jnp.where(tie, before, -1.0) >= keep_wide, _NEG_INF, s)
            return seen + jnp.sum(tm, axis=1, keepdims=True)

        lax.fori_loop(0, nchunks, drop, jnp.zeros((rows, 1), _F32))


def _sample_attend_kernel(pt_ref, qs_ref, zbs_ref, kn_ref, vn_ref, s_ref, thr_ref,
                          ck_hbm, cv_hbm, out_ref, kbuf, vbuf, sems, l_ref,
                          *, npages, tk, nchunks):
    b = pl.program_id(0)
    past = npages * PAGE_SIZE
    tq = qs_ref.shape[1]
    slot = _stream_pages(pt_ref, b, pl.num_programs(0), npages, [ck_hbm, cv_hbm], [kbuf, vbuf],
                         sems)
    _put_new_keys(kbuf, slot, past, kn_ref[0])
    _put_new_keys(vbuf, slot, past, vn_ref[0])

    qs = qs_ref[0].astype(_F32)
    lane = lax.broadcasted_iota(jnp.int32, (tq, LANES), 1)
    lo_half = lane < HEAD_DIM
    q8_rows = []
    for j in range(N_HEADS // 2):
        blk = qs[:, j * LANES:(j + 1) * LANES]
        q8_rows.append(jnp.where(lo_half, blk, 0.0))
        q8_rows.append(jnp.where(lo_half, 0.0, blk))
    q8 = jnp.concatenate(q8_rows, axis=0).astype(_MXU_DTYPE)
    thr_wide = jnp.broadcast_to(thr_ref[...], (tq, tk))

    mx = jnp.full((N_HEADS * tq, tk), -jnp.inf, _F32)
    for c in range(nchunks):
        keys_t = kbuf[slot, :, c * tk:(c + 1) * tk].astype(_MXU_DTYPE)
        lg = jnp.dot(q8, keys_t, preferred_element_type=_F32)
        sel = s_ref[0, c] >= thr_wide
        lg = jnp.concatenate(
            [jnp.where(sel, lg[h * tq:(h + 1) * tq, :], _NEG_INF) for h in range(N_HEADS)], axis=0)
        l_ref[c] = lg
        mx = jnp.maximum(mx, lg)
    m_wide = jnp.broadcast_to(jnp.max(mx, axis=1, keepdims=True), (N_HEADS * tq, tk))

    nt_dims = (((1,), (1,)), ((), ()))
    psum = jnp.zeros((N_HEADS * tq, tk), _F32)
    acc = jnp.zeros((N_HEADS * tq, KV_W), _F32)
    for c in range(nchunks):
        p = jnp.exp(l_ref[c] - m_wide)
        psum = psum + p
        vals_t = vbuf[slot, :, c * tk:(c + 1) * tk].astype(_MXU_DTYPE)
        acc = acc + lax.dot_general(p.astype(_MXU_DTYPE), vals_t, nt_dims,
                                    preferred_element_type=_F32)
    o = acc / jnp.sum(psum, axis=1, keepdims=True)
    blocks = []
    for j in range(N_HEADS // 2):
        blocks.append(jnp.where(lo_half, o[(2 * j) * tq:(2 * j + 1) * tq, :],
                                o[(2 * j + 1) * tq:(2 * j + 2) * tq, :]))
    out_ref[0] = jnp.concatenate(blocks, axis=1) * zbs_ref[0]


def _attn_sample(page_table, qs, qi, wi, zbs, k_new, v_new, ki_new, cache_kt, cache_vt, cache_it,
                 *, tk):
    nb, npages = page_table.shape
    tq = qs.shape[0] // nb
    past = npages * PAGE_SIZE
    nchunks = -(-(past + PAGE_SIZE) // tk)
    keys = nchunks * tk
    k_sel = min(TOP_K_MAX, (past + tq) // 4)
    group = LANES // tq
    assert LANES % tq == 0 and nb % group == 0
    three = lambda a: a.reshape(nb, tq, a.shape[-1])
    blk = lambda w: pl.BlockSpec((1, tq, w), lambda b, pt: (b, 0, 0))
    hbm = pl.BlockSpec(memory_space=pl.ANY)
    params = pltpu.CompilerParams(dimension_semantics=("arbitrary",),
                                  vmem_limit_bytes=VMEM_LIMIT_BYTES)
    pt_flat = page_table.reshape(-1)
    s_shape = jax.ShapeDtypeStruct((nb // group, nchunks, LANES, tk), _F32)
    s_blk = pl.BlockSpec((1, nchunks, tq, tk), lambda b, pt: (b // group, 0, b % group, 0))

    scores = pl.pallas_call(
        functools.partial(_sample_scores_kernel, npages=npages, tk=tk, nchunks=nchunks),
        grid_spec=pltpu.PrefetchScalarGridSpec(
            num_scalar_prefetch=1, grid=(nb,),
            in_specs=[blk(N_IDX_HEADS * D_IDX), blk(LANES), blk(D_IDX), hbm],
            out_specs=s_blk,
            scratch_shapes=[pltpu.VMEM((2, D_IDX, keys), cache_it.dtype),
                            pltpu.SemaphoreType.DMA((1, 2))]),
        out_shape=s_shape, compiler_params=params, name="sample_scores",
    )(pt_flat, three(qi), three(wi), three(ki_new), cache_it)

    n_adm = past + 1 + (jnp.arange(LANES) % tq)
    whole = pl.BlockSpec((1, nchunks, LANES, tk), lambda g: (g, 0, 0, 0))
    col = pl.BlockSpec((LANES, 1), lambda g: (0, 0))
    kept, thr = pl.pallas_call(
        functools.partial(_sample_threshold_kernel, tk=tk, nchunks=nchunks, k_sel=k_sel),
        grid=(nb // group,),
        in_specs=[whole, col, col],
        out_specs=(whole, pl.BlockSpec((LANES, 1), lambda g: (g, 0))),
        out_shape=(s_shape, jax.ShapeDtypeStruct((nb * tq, 1), _F32)),
        compiler_params=params, name="sample_threshold",
    )(scores, _seed_fraction(n_adm, k_sel).reshape(LANES, 1),
      n_adm.astype(_F32).reshape(LANES, 1))

    out = pl.pallas_call(
        functools.partial(_sample_attend_kernel, npages=npages, tk=tk, nchunks=nchunks),
        grid_spec=pltpu.PrefetchScalarGridSpec(
            num_scalar_prefetch=1, grid=(nb,),
            in_specs=[blk(D_B), blk(D_B), blk(KV_W), blk(KV_W), s_blk,
                      pl.BlockSpec((tq, 1), lambda b, pt: (b, 0)), hbm, hbm],
            out_specs=blk(D_B),
            scratch_shapes=[pltpu.VMEM((2, KV_W, keys), cache_kt.dtype),
                            pltpu.VMEM((2, KV_W, keys), cache_vt.dtype),
                            pltpu.SemaphoreType.DMA((2, 2)),
                            pltpu.VMEM((nchunks, N_HEADS * tq, tk), _F32)]),
        out_shape=jax.ShapeDtypeStruct((nb, tq, D_B), _F32),
        compiler_params=params, name="sample_attend",
    )(pt_flat, three(qs), three(zbs), three(k_new), three(v_new), kept, thr, cache_kt, cache_vt)
    return out.reshape(nb * tq, D_B)


def _outproj_kernel(yag_ref, ybg_ref, x_ref, w_ref, g_ref, out_ref):
    y = (jnp.dot(yag_ref[...], w_ref[:D_A, :], preferred_element_type=_F32)
         + jnp.dot(ybg_ref[...].astype(_MXU_DTYPE), w_ref[D_A:, :], preferred_element_type=_F32))
    ms = jnp.mean(y * y, axis=-1, keepdims=True)
    out_ref[...] = x_ref[...] + y * lax.rsqrt(ms + EPS) * g_ref[...]


def _outproj(yag, ybg, x2d, w_out_p, g_post, *, tm):
    r = x2d.shape[0]
    row = lambda w: pl.BlockSpec((tm, w), lambda i: (i, 0))
    const = lambda shape: pl.BlockSpec(shape, lambda i: (0,) * len(shape))
    return pl.pallas_call(
        _outproj_kernel,
        grid=(r // tm,),
        in_specs=[row(D_A), row(D_B), row(D_MODEL), const((D_A + D_B, D_MODEL)),
                  const((1, D_MODEL))],
        out_specs=row(D_MODEL),
        out_shape=jax.ShapeDtypeStruct((r, D_MODEL), _F32),
        compiler_params=pltpu.CompilerParams(
            dimension_semantics=("arbitrary",), vmem_limit_bytes=VMEM_LIMIT_BYTES),
        name="outproj",
    )(yag, ybg, x2d, w_out_p, g_post)


def _pad_in_weights(w):
    perm = _head_perm()
    nat = np.cumsum([0, D_A, D_A, D_A, D_B, KV_W, KV_W, D_B, N_IDX_HEADS * D_IDX, D_IDX,
                     N_IDX_HEADS])
    seg = lambda n: w[:, nat[n]:nat[n + 1]]
    pad = jnp.zeros((w.shape[0], C_END - C_WI - N_IDX_HEADS), w.dtype)
    return jnp.concatenate(
        [seg(0), seg(1), seg(2), seg(3)[:, perm], seg(4), seg(5), seg(6)[:, perm], seg(7),
         seg(8), seg(8), seg(9), pad], axis=1).astype(_MXU_DTYPE)


def _layer(x_prompt, x_sample, cache_k, cache_v, cache_idx_k, page_table, g_pre, w_in, ln_v_g,
           ln_v_b, w_s, b_s, w_out, g_post):
    nb, t, _ = x_prompt.shape
    nd, td, _ = x_sample.shape
    npages = page_table.shape[1]
    past = npages * PAGE_SIZE
    tm = 1024
    tq = 128
    tk = 1024
    tk_sample = 512

    w_pad = _pad_in_weights(w_in)
    perm = _head_perm()
    w_out_p = jnp.concatenate([w_out[:D_A], w_out[D_A:][perm]], axis=0).astype(_MXU_DTYPE)
    gpre = g_pre.reshape(1, D_MODEL)
    gpost = g_post.reshape(1, D_MODEL)
    lng = ln_v_g.reshape(1, D_A)
    lnb = ln_v_b.reshape(1, D_A)
    tril = jnp.tril(jnp.ones((CHUNK, CHUNK), dtype=bool))
    ws_tril = jnp.where(tril[None], w_s, jnp.zeros_like(w_s))

    xp = x_prompt.reshape(nb * t, D_MODEL)
    tpb = t // tm
    wmix_p = ws_tril.astype(_MXU_DTYPE)
    bmix_p = jnp.repeat(b_s.T, A_GROUP_DIM, axis=1)
    outs = _inproj(xp, _rope_tables(jnp.arange(t, dtype=jnp.int32)), lambda i: (i % tpb, 0),
                   gpre, w_pad, lng, lnb, wmix_p, bmix_p, tm=tm, state_rows=CHUNK,
                   state_shape=(nb, CHUNK, D_A), state_index=lambda i: (i // tpb, 0, 0))
    yag, zbs, qs, qi, k32, v32, ki32, wi, kbf, vt3, kibf, vstate_p = outs
    ybg = _attn_prompt(qs, qi, wi, zbs, kibf, kbf, vt3, nb=nb, t=t, tq=tq, tk=tk)
    y_prompt = _outproj(yag, ybg, xp, w_out_p, gpost, tm=tm).reshape(nb, t, D_MODEL)

    xs = x_sample.reshape(nd * td, D_MODEL)
    reps = CHUNK // td
    eye = jnp.eye(reps, dtype=w_s.dtype)
    wmix_s = jnp.stack([jnp.kron(eye, ws_tril[g, :td, :td]) for g in range(N_A_GROUPS)]
                       ).astype(_MXU_DTYPE)
    bmix_s = jnp.repeat(jnp.tile(b_s[:, :td], (1, reps)).T, A_GROUP_DIM, axis=1)
    tm_s = min(tm, nd * td)
    pos_s = past + (jnp.arange(tm_s, dtype=jnp.int32) % td)
    outs = _inproj(xs, _rope_tables(pos_s), lambda i: (0, 0), gpre, w_pad, lng, lnb, wmix_s,
                   bmix_s, tm=tm_s, state_rows=tm_s, state_shape=(nd * td, D_A),
                   state_index=lambda i: (i, 0))
    yag_s, zbs_s, qs_s, qi_s, k32_s, v32_s, ki32_s, wi_s, _, _, _, vstate_s = outs
    pool = cache_k.shape[0]
    ybg_s = _attn_sample(page_table, qs_s, qi_s, wi_s, zbs_s, k32_s, v32_s, ki32_s,
                         jnp.transpose(cache_k, (0, 2, 3, 1)).reshape(pool, KV_W, PAGE_SIZE),
                         jnp.transpose(cache_v, (0, 2, 3, 1)).reshape(pool, KV_W, PAGE_SIZE),
                         jnp.transpose(cache_idx_k, (0, 2, 1)), tk=tk_sample)
    y_sample = _outproj(yag_s, ybg_s, xs, w_out_p, gpost, tm=tm_s).reshape(nd, td, D_MODEL)

    return (y_prompt, y_sample,
            k32.reshape(nb, t, N_KV_HEADS, HEAD_DIM), v32.reshape(nb, t, N_KV_HEADS, HEAD_DIM),
            ki32.reshape(nb, t, D_IDX), vstate_p,
            k32_s.reshape(nd, td, N_KV_HEADS, HEAD_DIM), v32_s.reshape(nd, td, N_KV_HEADS, HEAD_DIM),
            ki32_s.reshape(nd, td, D_IDX), vstate_s.reshape(nd, td, D_A))


def kernel(x_prompt, x_sample, cache_k, cache_v, cache_idx_k, page_table, g_pre, w_in, ln_v_g,
           ln_v_b, w_s, b_s, w_out, g_post):
    xp, xs = x_prompt, x_sample
    per_layer = []
    for l in range(g_pre.shape[0]):
        outs = _layer(xp, xs, cache_k[l], cache_v[l], cache_idx_k[l], page_table, g_pre[l],
                      w_in[l], ln_v_g[l], ln_v_b[l], w_s[l], b_s[l], w_out[l], g_post[l])
        xp, xs = outs[0], outs[1]
        per_layer.append(outs[2:])
    stacked = tuple(jnp.stack(leaves, 0) for leaves in zip(*per_layer))
    return (xp, xs) + stacked
```

```python
import functools

import numpy as np
import jax
import jax.numpy as jnp
from jax import lax
from jax.scipy.special import ndtri
from jax.experimental import pallas as pl
from jax.experimental.pallas import tpu as pltpu

_F32 = jnp.float32
_MXU_DTYPE = jnp.bfloat16

D_MODEL = 1024
D_A = 512
D_B = 512
CHUNK = 128
N_A_GROUPS = 4
A_GROUP_DIM = D_A // N_A_GROUPS
HEAD_DIM = 64
N_HEADS = 8
N_KV_HEADS = 2
KV_W = N_KV_HEADS * HEAD_DIM
ROT_DIM = HEAD_DIM // 4
N_IDX_HEADS = 8
D_IDX = 64
TOP_K_MAX = 256
ROPE_THETA = 500000.0
EPS = 1e-6
PAGE_SIZE = 128

LANES = 128
VMEM_LIMIT_BYTES = 56 * 1024 * 1024

C_U, C_V, C_ZA, C_Q, C_K, C_VV, C_ZB, C_QI, C_KI, C_WI, C_END = (
    0, 512, 1024, 1536, 2048, 2176, 2304, 2816, 3328, 3456, 3584)

_FLT_MAX = float(np.finfo(np.float32).max)
_NEG_INF = float("-inf")
_INT_MIN = -2 ** 31


def _head_perm():
    c = np.arange(N_HEADS * HEAD_DIM)
    j = c // LANES
    half = (c % LANES) // HEAD_DIM
    d = c % HEAD_DIM
    return (j + 4 * half) * HEAD_DIM + d


def _rope_tables(pos):
    half = ROT_DIM // 2
    inv = jnp.power(jnp.float32(ROPE_THETA), -jnp.arange(half, dtype=_F32) * 2.0 / ROT_DIM)
    ang = pos.astype(_F32)[:, None] * inv[None, :]
    cos = jnp.cos(ang)
    sin = jnp.sin(ang)
    l64 = np.arange(LANES) % HEAD_DIM
    fidx = l64 % half
    in_rot = (l64 < ROT_DIM)[None, :]
    first = (l64 < half)[None, :]
    second = ((l64 >= half) & (l64 < ROT_DIM))[None, :]
    cos_t = jnp.where(in_rot, cos[:, fidx], 1.0)
    sin_a = jnp.where(first, -sin[:, fidx], 0.0)
    sin_b = jnp.where(second, sin[:, fidx], 0.0)
    return cos_t, sin_a, sin_b


def _silu(z):
    return z * (1.0 / (1.0 + jnp.exp(-z)))


def _inproj_kernel(x_ref, cos_ref, sa_ref, sb_ref, gpre_ref, win_ref, lng_ref, lnb_ref,
                   wmix_ref, bmix_ref,
                   yag_ref, zbs_ref, qs_ref, qi_ref, k32_ref, v32_ref, ki32_ref, wi_ref,
                   kbf_ref, vt_ref, kibf_ref, vstate_ref, *, tm, state_rows, kv_transposed):
    x = x_ref[...]
    ms = jnp.mean(x * x, axis=-1, keepdims=True)
    h = (x * lax.rsqrt(ms + EPS) * gpre_ref[...]).astype(_MXU_DTYPE)

    def proj(a, b):
        return jnp.dot(h, win_ref[:, a:b], preferred_element_type=_F32)

    cos_t = cos_ref[...]
    sin_a = sa_ref[...]
    sin_b = sb_ref[...]

    def rope(p):
        blocks = []
        for j in range(p.shape[1] // LANES):
            pj = p[:, j * LANES:(j + 1) * LANES]
            blocks.append(pj * cos_t
                          + pltpu.roll(pj, LANES - ROT_DIM // 2, 1) * sin_a
                          + pltpu.roll(pj, ROT_DIM // 2, 1) * sin_b)
        return blocks[0] if len(blocks) == 1 else jnp.concatenate(blocks, axis=1)

    u = proj(C_U, C_V)
    v = proj(C_V, C_ZA)
    mu = jnp.mean(v, axis=-1, keepdims=True)
    vc = v - mu
    var = jnp.mean(vc * vc, axis=-1, keepdims=True)
    vn = vc * lax.rsqrt(var + EPS) * lng_ref[...] + lnb_ref[...]
    vstate_ref[...] = vn[tm - state_rows:, :].reshape(vstate_ref.shape)
    vnb = vn.astype(_MXU_DTYPE)
    bmix = bmix_ref[...]
    rows = []
    for c in range(tm // CHUNK):
        cols = []
        for g in range(N_A_GROUPS):
            blk = vnb[c * CHUNK:(c + 1) * CHUNK, g * A_GROUP_DIM:(g + 1) * A_GROUP_DIM]
            cols.append(jnp.dot(wmix_ref[g], blk, preferred_element_type=_F32)
                        + bmix[:, g * A_GROUP_DIM:(g + 1) * A_GROUP_DIM])
        rows.append(jnp.concatenate(cols, axis=1))
    mixed = jnp.concatenate(rows, axis=0)
    za = proj(C_ZA, C_Q)
    yag_ref[...] = ((u * mixed) * _silu(za)).astype(yag_ref.dtype)

    q = rope(proj(C_Q, C_K))
    qs_ref[...] = (q * (HEAD_DIM ** -0.5)).astype(qs_ref.dtype)
    k = rope(proj(C_K, C_VV))
    kbf_ref[...] = k.astype(kbf_ref.dtype)
    vv = proj(C_VV, C_ZB)
    vv_t = vv.T
    vt_ref[0] = vv_t.astype(vt_ref.dtype)
    if kv_transposed:
        k32_ref[0] = k.T
        v32_ref[0] = vv_t
    else:
        k32_ref[...] = k
        v32_ref[...] = vv
    zb = proj(C_ZB, C_QI)
    zbs_ref[...] = _silu(zb)
    qi = rope(proj(C_QI, C_KI))
    qi_ref[...] = qi.astype(qi_ref.dtype)
    ki = rope(proj(C_KI, C_WI))
    if kv_transposed:
        ki32_ref[0] = ki.T[:D_IDX, :]
    else:
        ki32_ref[...] = ki[:, :D_IDX]
    kibf_ref[...] = ki.astype(kibf_ref.dtype)
    wi_ref[...] = proj(C_WI, C_END) * (N_IDX_HEADS ** -0.5)


def _inproj(x2d, tables, table_index, gpre, w_pad, lng, lnb, wmix, bmix, *, tm, state_rows,
            state_shape, state_index, seqs=None):
    r = x2d.shape[0]
    nt = r // tm
    if seqs is None:
        kv = lambda w: (jax.ShapeDtypeStruct((r, w), _F32), pl.BlockSpec((tm, w), lambda i: (i, 0)))
    else:
        tps = nt // seqs
        kv = lambda w: (jax.ShapeDtypeStruct((seqs, w, r // seqs), _F32),
                        pl.BlockSpec((1, w, tm), lambda i: (i // tps, 0, i % tps)))
    cos_t, sin_a, sin_b = tables
    row = lambda w: pl.BlockSpec((tm, w), lambda i: (i, 0))
    const = lambda shape: pl.BlockSpec(shape, lambda i: (0,) * len(shape))
    tab = pl.BlockSpec((tm, LANES), table_index)
    out_shapes = (
        jax.ShapeDtypeStruct((r, D_A), _MXU_DTYPE),
        jax.ShapeDtypeStruct((r, D_B), _F32),
        jax.ShapeDtypeStruct((r, D_B), _MXU_DTYPE),
        jax.ShapeDtypeStruct((r, N_IDX_HEADS * D_IDX), _MXU_DTYPE),
        kv(KV_W)[0],
        kv(KV_W)[0],
        kv(D_IDX)[0],
        jax.ShapeDtypeStruct((r, LANES), _F32),
        jax.ShapeDtypeStruct((r, KV_W), _MXU_DTYPE),
        jax.ShapeDtypeStruct((nt, KV_W, tm), _MXU_DTYPE),
        jax.ShapeDtypeStruct((r, 2 * D_IDX), _MXU_DTYPE),
        jax.ShapeDtypeStruct(state_shape, _F32),
    )
    out_specs = (
        row(D_A), row(D_B), row(D_B), row(N_IDX_HEADS * D_IDX), kv(KV_W)[1], kv(KV_W)[1],
        kv(D_IDX)[1], row(LANES), row(KV_W),
        pl.BlockSpec((1, KV_W, tm), lambda i: (i, 0, 0)),
        row(2 * D_IDX),
        pl.BlockSpec((1,) * (len(state_shape) - 2) + (state_rows, D_A), state_index),
    )
    return pl.pallas_call(
        functools.partial(_inproj_kernel, tm=tm, state_rows=state_rows,
                          kv_transposed=seqs is not None),
        grid=(nt,),
        in_specs=[row(D_MODEL), tab, tab, tab, const((1, D_MODEL)), const((D_MODEL, C_END)),
                  const((1, D_A)), const((1, D_A)), const((N_A_GROUPS, CHUNK, CHUNK)),
                  const((CHUNK, D_A))],
        out_specs=out_specs,
        out_shape=out_shapes,
        compiler_params=pltpu.CompilerParams(
            dimension_semantics=("arbitrary",), vmem_limit_bytes=VMEM_LIMIT_BYTES),
        name="inproj",
    )(x2d, cos_t, sin_a, sin_b, gpre, w_pad, lng, lnb, wmix, bmix)


def _f32_to_key(x):
    b = lax.bitcast_convert_type(x, jnp.int32)
    return jnp.where(b >= 0, b, b ^ jnp.int32(0x7FFFFFFF))


def _key_to_f32(k):
    return lax.bitcast_convert_type(jnp.where(k >= 0, k, k ^ jnp.int32(0x7FFFFFFF)), _F32)


N_INTERP_PASSES = 10
MAX_EXP_ARG = 40.0


def _topk_threshold(count_ge, max_below, smin, smax, n_adm, f0, k_sel):
    kf = jnp.float32(k_sel)
    log_k = jnp.log(kf - 0.5)
    lo0 = _f32_to_key(smin)
    hi0 = _f32_to_key(smax) + 1
    short = n_adm <= kf
    done0 = jnp.where(jnp.logical_or(short, hi0 <= lo0 + 1), 1.0, 0.0)
    g_hi0 = jnp.full(smin.shape, np.log(0.5), _F32) - log_k

    def all_done(done):
        return jnp.min(done) > 0.5

    def probe(carry, frac):
        it, lo, hi, clo, chi, glo, ghi, last, done = carry
        xlo = _key_to_f32(lo)
        xhi = _key_to_f32(hi)
        cand = jnp.clip(_f32_to_key(xlo + (xhi - xlo) * frac), lo + 1, hi - 1)
        cnt = count_ge(_key_to_f32(cand))
        g = jnp.log(jnp.maximum(cnt, 0.5)) - log_k
        active = done < 0.5
        new_lo = jnp.logical_and(active, cnt >= kf)
        new_hi = jnp.logical_and(active, cnt < kf)
        ghi = jnp.where(jnp.logical_and(new_lo, last == 1), ghi * 0.5, ghi)
        glo = jnp.where(jnp.logical_and(new_hi, last == 2), glo * 0.5, glo)
        lo = jnp.where(new_lo, cand, lo)
        clo = jnp.where(new_lo, cnt, clo)
        glo = jnp.where(new_lo, g, glo)
        hi = jnp.where(new_hi, cand, hi)
        chi = jnp.where(new_hi, cnt, chi)
        ghi = jnp.where(new_hi, g, ghi)
        last = jnp.where(new_lo, 1, jnp.where(new_hi, 2, last))
        done = jnp.where(jnp.logical_or(clo == kf, hi <= lo + 1), 1.0, done)
        return it + 1, lo, hi, clo, chi, glo, ghi, last, done

    def interp_body(_, carry):
        return probe(carry, carry[5] / (carry[5] - carry[6]))

    init = (jnp.int32(0), lo0, hi0, n_adm, jnp.zeros_like(n_adm),
            jnp.log(jnp.maximum(n_adm, 1.0)) - log_k, g_hi0, jnp.zeros(smin.shape, jnp.int32),
            done0)
    state = lax.fori_loop(1, N_INTERP_PASSES, interp_body, probe(init, f0))
    _, lo, hi, clo, chi, _, _, _, done = state

    def peel_cond(carry):
        return jnp.logical_not(all_done(carry[-1]))

    def peel_body(carry):
        lo, hi, clo, chi, done = carry
        top = max_below(_key_to_f32(hi))
        cnt = count_ge(top)
        active = done < 0.5
        hit = jnp.logical_and(active, cnt >= kf)
        miss = jnp.logical_and(active, cnt < kf)
        lo = jnp.where(hit, _f32_to_key(top), lo)
        clo = jnp.where(hit, cnt, clo)
        hi = jnp.where(miss, _f32_to_key(top), hi)
        chi = jnp.where(miss, cnt, chi)
        done = jnp.where(hit, 1.0, done)
        return lo, hi, clo, chi, done

    lo, _, clo, chi, _ = lax.while_loop(peel_cond, peel_body, (lo, hi, clo, chi, done))
    thr = jnp.where(short, -_FLT_MAX, _key_to_f32(lo))
    return thr, clo, chi, short


def _seed_fraction(n_adm, k_sel):
    n = n_adm.astype(_F32)
    z_k = ndtri(1.0 - jnp.minimum(k_sel / n, 0.999))
    z_n = ndtri(1.0 - 1.0 / (n + 1.0))
    return (0.5 + 0.5 * z_k / z_n).astype(_F32)


def _attn_prompt_kernel(qs_ref, qi_ref, wi_ref, zbs_ref, f0_ref, ki_ref, k_ref, vt_ref, out_ref,
                        s_ref, q8_ref, qi8_ref, bias_ref, pt_ref, acc_ref, m_ref, l_ref,
                        mnew_ref, beta_ref, psum_ref, risk_ref, scale_ref, *, tq, tk, k_sel):
    i = pl.program_id(1)
    ratio = tk // tq
    nchunks = lax.div(i + ratio, ratio)
    kf = jnp.float32(k_sel)

    lane = lax.broadcasted_iota(jnp.int32, (tq, LANES), 1)
    lo_half = lane < HEAD_DIM
    for j in range(N_HEADS // 2):
        for src, dst in ((qs_ref, q8_ref), (qi_ref, qi8_ref)):
            blk = src[:, j * LANES:(j + 1) * LANES].astype(_F32)
            dst[(2 * j) * tq:(2 * j + 1) * tq, :] = jnp.where(lo_half, blk, 0.0).astype(dst.dtype)
            dst[(2 * j + 1) * tq:(2 * j + 2) * tq, :] = jnp.where(lo_half, 0.0, blk).astype(dst.dtype)
    w_t = wi_ref[...].T

    row = lax.broadcasted_iota(jnp.int32, (tk, tq), 0)
    col = lax.broadcasted_iota(jnp.int32, (tk, tq), 1)
    key_minus_query = row - col
    nt_dims = (((1,), (1,)), ((), ()))
    fold = lambda x: x.reshape(tk // 32, 32, tq)

    def scores(c, carry):
        mn, mx = carry
        k0 = pl.multiple_of(c * tk, tk)
        rt = lax.dot_general(ki_ref[pl.ds(k0, tk), :], qi8_ref[...], nt_dims,
                             preferred_element_type=_F32)
        s = None
        for hb in range(N_IDX_HEADS):
            t = jnp.maximum(rt[:, hb * tq:(hb + 1) * tq], 0.0) * w_t[hb:hb + 1, :]
            s = t if s is None else s + t
        s_ref[c] = jnp.where(key_minus_query <= i * tq - c * tk, s, _NEG_INF)
        return jnp.minimum(mn, fold(s).min(axis=0)), jnp.maximum(mx, fold(s).max(axis=0))

    mn, mx = lax.fori_loop(0, nchunks, scores,
                           (jnp.full((32, tq), jnp.inf, _F32), jnp.full((32, tq), -jnp.inf, _F32)))
    smin = jnp.min(mn, axis=0, keepdims=True)
    smax = jnp.max(mx, axis=0, keepdims=True)

    def count_ge(t):
        def body(c, acc):
            return acc + fold(jnp.where(s_ref[c] >= t, 1.0, 0.0)).sum(axis=0)
        acc = lax.fori_loop(0, nchunks, body, jnp.zeros((32, tq), _F32))
        return jnp.sum(acc, axis=0, keepdims=True)

    def max_below(x):
        def body(c, acc):
            s = s_ref[c]
            return jnp.maximum(acc, fold(jnp.where(s < x, s, _NEG_INF)).max(axis=0))
        acc = lax.fori_loop(0, nchunks, body, jnp.full((32, tq), -jnp.inf, _F32))
        return jnp.max(acc, axis=0, keepdims=True)

    n_adm = (i * tq + 1 + lax.broadcasted_iota(jnp.int32, (1, tq), 1)).astype(_F32)
    thr, clo, chi, short = _topk_threshold(count_ge, max_below, smin, smax, n_adm, f0_ref[0],
                                           k_sel)

    need = jnp.logical_and(clo > kf, jnp.logical_not(short))

    @pl.when(jnp.max(jnp.where(need, 1.0, 0.0)) > 0.5)
    def _():
        keep = jnp.where(need, kf - chi, _FLT_MAX)
        rk = lax.broadcasted_iota(jnp.int32, (tk, tk), 0)
        ck = lax.broadcasted_iota(jnp.int32, (tk, tk), 1)
        earlier = jnp.where(rk > ck, 1.0, 0.0).astype(_MXU_DTYPE)

        def drop(c, seen):
            s = s_ref[c]
            tie = s == thr
            tm = jnp.where(tie, 1.0, 0.0)
            before = jnp.dot(earlier, tm.astype(_MXU_DTYPE), preferred_element_type=_F32) + seen
            s_ref[c] = jnp.where(jnp.where(tie, before, -1.0) >= keep, _NEG_INF, s)
            return seen + jnp.sum(fold(tm).sum(axis=0), axis=0, keepdims=True)

        lax.fori_loop(0, nchunks, drop, jnp.zeros((1, tq), _F32))

    m_ref[...] = jnp.full(m_ref.shape, -_FLT_MAX, _F32)
    l_ref[...] = jnp.zeros(l_ref.shape, _F32)
    acc_ref[...] = jnp.zeros(acc_ref.shape, _F32)
    wide = lambda a: jnp.concatenate([a[hb:hb + 1, :] for hb in range(N_HEADS)], axis=1)

    def set_bias(c):
        bias_ref[...] = jnp.where(s_ref[c] >= thr, 0.0, _NEG_INF)

    def chunk_probs(c, against_own_max):
        k_chunk = k_ref[pl.ds(pl.multiple_of(c * tk, tk), tk), :]
        slot = lax.rem(c, 2)
        for j in range(N_HEADS // 2):
            l2 = lax.dot_general(k_chunk, q8_ref[(2 * j) * tq:(2 * j + 2) * tq, :], nt_dims,
                                 preferred_element_type=_F32)
            for half in range(2):
                hb = 2 * j + half
                lh = l2[:, half * tq:(half + 1) * tq] + bias_ref[...]
                m_old = m_ref[hb:hb + 1, :]
                m_new = jnp.maximum(m_old, jnp.max(fold(lh).max(axis=0), axis=0, keepdims=True))
                if against_own_max:
                    p = jnp.exp(lh - m_new)
                    beta_ref[hb:hb + 1, :] = jnp.ones((1, tq), _F32)
                else:
                    p = jnp.exp(lh - m_old)
                    beta_ref[hb:hb + 1, :] = jnp.exp(m_old - m_new)
                    risk_ref[hb:hb + 1, :] = m_new - m_old
                pt_ref[slot, :, hb * tq:(hb + 1) * tq] = p.astype(pt_ref.dtype)
                psum_ref[hb:hb + 1, :] = jnp.sum(fold(p).sum(axis=0), axis=0, keepdims=True)
                mnew_ref[hb:hb + 1, :] = m_new

    def fold_in_stats():
        m_new = mnew_ref[...]
        alpha = jnp.exp(m_ref[...] - m_new)
        beta = beta_ref[...]
        l_ref[...] = l_ref[...] * alpha + psum_ref[...] * beta
        m_ref[...] = m_new
        scale_ref[0:1, :] = wide(alpha)
        scale_ref[1:2, :] = wide(beta)

    def apply_values(c):
        pv = jnp.dot(vt_ref[c], pt_ref[lax.rem(c, 2)], preferred_element_type=_F32)
        acc_ref[...] = acc_ref[...] * scale_ref[0:1, :] + pv * scale_ref[1:2, :]

    set_bias(0)
    chunk_probs(0, against_own_max=True)
    fold_in_stats()

    def attend(c, carry):
        set_bias(c)
        apply_values(c - 1)
        chunk_probs(c, against_own_max=False)

        @pl.when(jnp.max(risk_ref[...]) > MAX_EXP_ARG)
        def _():
            chunk_probs(c, against_own_max=True)

        fold_in_stats()
        return carry

    lax.fori_loop(1, nchunks, attend, 0)
    apply_values(nchunks - 1)

    l_all = jnp.concatenate([l_ref[hb:hb + 1, :] for hb in range(N_HEADS)], axis=1)
    o_t = acc_ref[...] / l_all
    blocks = []
    for j in range(N_HEADS // 2):
        mj = jnp.concatenate(
            [o_t[:HEAD_DIM, (2 * j) * tq:(2 * j + 1) * tq],
             o_t[HEAD_DIM:, (2 * j + 1) * tq:(2 * j + 2) * tq]], axis=0)
        blocks.append(mj.T)
    yb = jnp.concatenate(blocks, axis=1)
    out_ref[...] = (yb * zbs_ref[...]).astype(out_ref.dtype)


def _attn_prompt(qs, qi, wi, zbs, kibf, kbf, vt3, *, nb, t, tq, tk):
    nq = t // tq
    k_sel = min(TOP_K_MAX, t // 4)
    f0 = _seed_fraction(jnp.arange(1, t + 1), k_sel).reshape(nq, 1, tq)
    blk = lambda w: pl.BlockSpec((tq, w), lambda b, i: (b * nq + i, 0))
    res = lambda w: pl.BlockSpec((t, w), lambda b, i: (b, 0))
    stat = pltpu.VMEM((N_HEADS, tq), _F32)
    kernel = functools.partial(_attn_prompt_kernel, tq=tq, tk=tk, k_sel=k_sel)
    return pl.pallas_call(
        kernel,
        grid=(nb, nq),
        in_specs=[blk(D_B), blk(N_IDX_HEADS * D_IDX), blk(LANES), blk(D_B),
                  pl.BlockSpec((1, 1, tq), lambda b, i: (i, 0, 0)),
                  res(2 * D_IDX), res(KV_W),
                  pl.BlockSpec((t // tk, KV_W, tk), lambda b, i: (b, 0, 0))],
        out_specs=blk(D_B),
        out_shape=jax.ShapeDtypeStruct((nb * t, D_B), _MXU_DTYPE),
        scratch_shapes=[
            pltpu.VMEM((t // tk, tk, tq), _F32),
            pltpu.VMEM((N_HEADS * tq, LANES), _MXU_DTYPE),
            pltpu.VMEM((N_IDX_HEADS * tq, LANES), _MXU_DTYPE),
            pltpu.VMEM((tk, tq), _F32),
            pltpu.VMEM((2, tk, N_HEADS * tq), _MXU_DTYPE),
            pltpu.VMEM((KV_W, N_HEADS * tq), _F32),
            stat, stat,
            stat, stat, stat, stat,
            pltpu.VMEM((2, N_HEADS * tq), _F32),
        ],
        compiler_params=pltpu.CompilerParams(
            dimension_semantics=("arbitrary", "arbitrary"), vmem_limit_bytes=VMEM_LIMIT_BYTES),
        name="attn_prompt",
    )(qs, qi, wi, zbs, f0, kibf, kbf, vt3)


def _stream_pages(pt_ref, b, nb, npages, sources, buffers, sems):
    slot = lax.rem(b, 2)

    def copies(bb, sl, p):
        phys = pt_ref[bb * npages + p]
        dst = pl.ds(pl.multiple_of(p * PAGE_SIZE, PAGE_SIZE), PAGE_SIZE)
        return [pltpu.make_async_copy(src.at[phys], buf.at[sl, :, dst], sems.at[i, sl])
                for i, (src, buf) in enumerate(zip(sources, buffers))]

    def start(bb, sl):
        def body(p, carry):
            for cp in copies(bb, sl, p):
                cp.start()
            return carry
        lax.fori_loop(0, npages, body, 0)

    def wait(bb, sl):
        def body(p, carry):
            for cp in copies(bb, sl, p):
                cp.wait()
            return carry
        lax.fori_loop(0, npages, body, 0)

    @pl.when(b == 0)
    def _():
        past = npages * PAGE_SIZE
        for buf in buffers:
            for sl in range(2):
                buf[sl, :, past:] = jnp.zeros((buf.shape[1], buf.shape[2] - past), buf.dtype)
        start(0, 0)

    @pl.when(b + 1 < nb)
    def _():
        start(b + 1, 1 - slot)

    wait(b, slot)
    return slot


def _put_new_keys(buf, slot, past, rows):
    tq, feat = rows.shape
    tile = jnp.concatenate([rows, jnp.zeros((PAGE_SIZE - tq, feat), _F32)], axis=0)
    if feat < LANES:
        tile = jnp.concatenate([tile, jnp.zeros((PAGE_SIZE, LANES - feat), _F32)], axis=1)
    buf[slot, :, past:past + PAGE_SIZE] = tile.T[:feat, :].astype(buf.dtype)


def _sample_scores_kernel(pt_ref, qi_ref, wi_ref, kin_ref, ci_hbm, s_ref, ibuf, sems,
                          *, npages, tk, nchunks):
    b = pl.program_id(0)
    past = npages * PAGE_SIZE
    tq = qi_ref.shape[1]
    slot = _stream_pages(pt_ref, b, pl.num_programs(0), npages, [ci_hbm], [ibuf], sems)
    _put_new_keys(ibuf, slot, past, kin_ref[0])

    qi = qi_ref[0].astype(_F32)
    wi = wi_ref[0]
    qi8 = jnp.concatenate([qi[:, h * D_IDX:(h + 1) * D_IDX] for h in range(N_IDX_HEADS)],
                          axis=0).astype(_MXU_DTYPE)
    w_wide = [jnp.broadcast_to(wi[:, h:h + 1], (tq, tk)) for h in range(N_IDX_HEADS)]
    key_minus_query = (lax.broadcasted_iota(jnp.int32, (tq, tk), 1)
                       - lax.broadcasted_iota(jnp.int32, (tq, tk), 0))
    for c in range(nchunks):
        keys_t = ibuf[slot, :, c * tk:(c + 1) * tk].astype(_MXU_DTYPE)
        r = jnp.dot(qi8, keys_t, preferred_element_type=_F32)
        s = None
        for h in range(N_IDX_HEADS):
            t = jnp.maximum(r[h * tq:(h + 1) * tq, :], 0.0) * w_wide[h]
            s = t if s is None else s + t
        s_ref[0, c] = jnp.where(key_minus_query <= past - c * tk, s, _NEG_INF)


def _sample_threshold_kernel(s_ref, f0_ref, nadm_ref, out_ref, thr_ref, *, tk, nchunks, k_sel):
    rows = s_ref.shape[2]
    kf = jnp.float32(k_sel)
    groups = tk // LANES
    tile = lambda col: jnp.broadcast_to(col, (rows, tk))

    def fold_with(x, op):
        out = x[:, :LANES]
        for g in range(1, groups):
            out = op(out, x[:, g * LANES:(g + 1) * LANES])
        return out

    def over_chunks(fn, op, init):
        def body(c, acc):
            return op(acc, fold_with(fn(s_ref[0, c]), op))
        return lax.fori_loop(0, nchunks, body, jnp.full((rows, LANES), init, _F32))

    def count_ge(t):
        t_wide = tile(t)
        acc = over_chunks(lambda s: jnp.where(s >= t_wide, 1.0, 0.0), jnp.add, 0.0)
        return jnp.sum(acc, axis=1, keepdims=True)

    def max_below(x):
        x_wide = tile(x)
        acc = over_chunks(lambda s: jnp.where(s < x_wide, s, _NEG_INF), jnp.maximum, -jnp.inf)
        return jnp.max(acc, axis=1, keepdims=True)

    finite_min = over_chunks(lambda s: jnp.where(s > _NEG_INF, s, jnp.inf), jnp.minimum, jnp.inf)
    finite_max = over_chunks(lambda s: s, jnp.maximum, -jnp.inf)
    smin = jnp.min(finite_min, axis=1, keepdims=True)
    smax = jnp.max(finite_max, axis=1, keepdims=True)
    thr, clo, chi, short = _topk_threshold(count_ge, max_below, smin, smax, nadm_ref[...],
                                           f0_ref[...], k_sel)
    thr_ref[...] = thr
    need = jnp.logical_and(clo > kf, jnp.logical_not(short))
    any_ties = jnp.max(jnp.where(need, 1.0, 0.0)) > 0.5

    @pl.when(jnp.logical_not(any_ties))
    def _():
        out_ref[...] = s_ref[...]

    @pl.when(any_ties)
    def _():
        keep_wide = tile(jnp.where(need, kf - chi, _FLT_MAX))
        thr_wide = tile(thr)
        rk = lax.broadcasted_iota(jnp.int32, (tk, tk), 0)
        ck = lax.broadcasted_iota(jnp.int32, (tk, tk), 1)
        earlier = jnp.where(rk < ck, 1.0, 0.0).astype(_MXU_DTYPE)

        def drop(c, seen):
            s = s_ref[0, c]
            tie = s == thr_wide
            tm = jnp.where(tie, 1.0, 0.0)
            before = (jnp.dot(tm.astype(_MXU_DTYPE), earlier, preferred_element_type=_F32)
                      + tile(seen))
            out_ref[0, c] = jnp.where(jnp.where(tie, before, -1.0) >= keep_wide, _NEG_INF, s)
            return seen + jnp.sum(tm, axis=1, keepdims=True)

        lax.fori_loop(0, nchunks, drop, jnp.zeros((rows, 1), _F32))


def _sample_attend_kernel(pt_ref, qs_ref, zbs_ref, kn_ref, vn_ref, s_ref, thr_ref,
                          ck_hbm, cv_hbm, out_ref, kbuf, vbuf, sems, l_ref,
                          *, npages, tk, nchunks):
    b = pl.program_id(0)
    past = npages * PAGE_SIZE
    tq = qs_ref.shape[1]
    slot = _stream_pages(pt_ref, b, pl.num_programs(0), npages, [ck_hbm, cv_hbm], [kbuf, vbuf],
                         sems)
    _put_new_keys(kbuf, slot, past, kn_ref[0])
    _put_new_keys(vbuf, slot, past, vn_ref[0])

    qs = qs_ref[0].astype(_F32)
    lane = lax.broadcasted_iota(jnp.int32, (tq, LANES), 1)
    lo_half = lane < HEAD_DIM
    q8_rows = []
    for j in range(N_HEADS // 2):
        blk = qs[:, j * LANES:(j + 1) * LANES]
        q8_rows.append(jnp.where(lo_half, blk, 0.0))
        q8_rows.append(jnp.where(lo_half, 0.0, blk))
    q8 = jnp.concatenate(q8_rows, axis=0).astype(_MXU_DTYPE)
    thr_wide = jnp.broadcast_to(thr_ref[...], (tq, tk))

    mx = jnp.full((N_HEADS * tq, tk), -jnp.inf, _F32)
    for c in range(nchunks):
        keys_t = kbuf[slot, :, c * tk:(c + 1) * tk].astype(_MXU_DTYPE)
        lg = jnp.dot(q8, keys_t, preferred_element_type=_F32)
        sel = s_ref[0, c] >= thr_wide
        lg = jnp.concatenate(
            [jnp.where(sel, lg[h * tq:(h + 1) * tq, :], _NEG_INF) for h in range(N_HEADS)], axis=0)
        l_ref[c] = lg
        mx = jnp.maximum(mx, lg)
    m_wide = jnp.broadcast_to(jnp.max(mx, axis=1, keepdims=True), (N_HEADS * tq, tk))

    nt_dims = (((1,), (1,)), ((), ()))
    psum = jnp.zeros((N_HEADS * tq, tk), _F32)
    acc = jnp.zeros((N_HEADS * tq, KV_W), _F32)
    for c in range(nchunks):
        p = jnp.exp(l_ref[c] - m_wide)
        psum = psum + p
        vals_t = vbuf[slot, :, c * tk:(c + 1) * tk].astype(_MXU_DTYPE)
        acc = acc + lax.dot_general(p.astype(_MXU_DTYPE), vals_t, nt_dims,
                                    preferred_element_type=_F32)
    o = acc / jnp.sum(psum, axis=1, keepdims=True)
    blocks = []
    for j in range(N_HEADS // 2):
        blocks.append(jnp.where(lo_half, o[(2 * j) * tq:(2 * j + 1) * tq, :],
                                o[(2 * j + 1) * tq:(2 * j + 2) * tq, :]))
    out_ref[0] = jnp.concatenate(blocks, axis=1) * zbs_ref[0]


def _attn_sample(page_table, qs, qi, wi, zbs, k_new, v_new, ki_new, cache_kt, cache_vt, cache_it,
                 *, tk):
    nb, npages = page_table.shape
    tq = qs.shape[0] // nb
    past = npages * PAGE_SIZE
    nchunks = -(-(past + PAGE_SIZE) // tk)
    keys = nchunks * tk
    k_sel = min(TOP_K_MAX, (past + tq) // 4)
    group = LANES // tq
    assert LANES % tq == 0 and nb % group == 0
    three = lambda a: a.reshape(nb, tq, a.shape[-1])
    blk = lambda w: pl.BlockSpec((1, tq, w), lambda b, pt: (b, 0, 0))
    hbm = pl.BlockSpec(memory_space=pl.ANY)
    params = pltpu.CompilerParams(dimension_semantics=("arbitrary",),
                                  vmem_limit_bytes=VMEM_LIMIT_BYTES)
    pt_flat = page_table.reshape(-1)
    s_shape = jax.ShapeDtypeStruct((nb // group, nchunks, LANES, tk), _F32)
    s_blk = pl.BlockSpec((1, nchunks, tq, tk), lambda b, pt: (b // group, 0, b % group, 0))

    scores = pl.pallas_call(
        functools.partial(_sample_scores_kernel, npages=npages, tk=tk, nchunks=nchunks),
        grid_spec=pltpu.PrefetchScalarGridSpec(
            num_scalar_prefetch=1, grid=(nb,),
            in_specs=[blk(N_IDX_HEADS * D_IDX), blk(LANES), blk(D_IDX), hbm],
            out_specs=s_blk,
            scratch_shapes=[pltpu.VMEM((2, D_IDX, keys), cache_it.dtype),
                            pltpu.SemaphoreType.DMA((1, 2))]),
        out_shape=s_shape, compiler_params=params, name="sample_scores",
    )(pt_flat, three(qi), three(wi), three(ki_new), cache_it)

    n_adm = past + 1 + (jnp.arange(LANES) % tq)
    whole = pl.BlockSpec((1, nchunks, LANES, tk), lambda g: (g, 0, 0, 0))
    col = pl.BlockSpec((LANES, 1), lambda g: (0, 0))
    kept, thr = pl.pallas_call(
        functools.partial(_sample_threshold_kernel, tk=tk, nchunks=nchunks, k_sel=k_sel),
        grid=(nb // group,),
        in_specs=[whole, col, col],
        out_specs=(whole, pl.BlockSpec((LANES, 1), lambda g: (g, 0))),
        out_shape=(s_shape, jax.ShapeDtypeStruct((nb * tq, 1), _F32)),
        compiler_params=params, name="sample_threshold",
    )(scores, _seed_fraction(n_adm, k_sel).reshape(LANES, 1),
      n_adm.astype(_F32).reshape(LANES, 1))

    out = pl.pallas_call(
        functools.partial(_sample_attend_kernel, npages=npages, tk=tk, nchunks=nchunks),
        grid_spec=pltpu.PrefetchScalarGridSpec(
            num_scalar_prefetch=1, grid=(nb,),
            in_specs=[blk(D_B), blk(D_B), blk(KV_W), blk(KV_W), s_blk,
                      pl.BlockSpec((tq, 1), lambda b, pt: (b, 0)), hbm, hbm],
            out_specs=blk(D_B),
            scratch_shapes=[pltpu.VMEM((2, KV_W, keys), cache_kt.dtype),
                            pltpu.VMEM((2, KV_W, keys), cache_vt.dtype),
                            pltpu.SemaphoreType.DMA((2, 2)),
                            pltpu.VMEM((nchunks, N_HEADS * tq, tk), _F32)]),
        out_shape=jax.ShapeDtypeStruct((nb, tq, D_B), _F32),
        compiler_params=params, name="sample_attend",
    )(pt_flat, three(qs), three(zbs), three(k_new), three(v_new), kept, thr, cache_kt, cache_vt)
    return out.reshape(nb * tq, D_B)


def _outproj_kernel(yag_ref, ybg_ref, x_ref, w_ref, g_ref, out_ref):
    y = (jnp.dot(yag_ref[...], w_ref[:D_A, :], preferred_element_type=_F32)
         + jnp.dot(ybg_ref[...].astype(_MXU_DTYPE), w_ref[D_A:, :], preferred_element_type=_F32))
    ms = jnp.mean(y * y, axis=-1, keepdims=True)
    out_ref[...] = x_ref[...] + y * lax.rsqrt(ms + EPS) * g_ref[...]


def _outproj(yag, ybg, x2d, w_out_p, g_post, *, tm):
    r = x2d.shape[0]
    row = lambda w: pl.BlockSpec((tm, w), lambda i: (i, 0))
    const = lambda shape: pl.BlockSpec(shape, lambda i: (0,) * len(shape))
    return pl.pallas_call(
        _outproj_kernel,
        grid=(r // tm,),
        in_specs=[row(D_A), row(D_B), row(D_MODEL), const((D_A + D_B, D_MODEL)),
                  const((1, D_MODEL))],
        out_specs=row(D_MODEL),
        out_shape=jax.ShapeDtypeStruct((r, D_MODEL), _F32),
        compiler_params=pltpu.CompilerParams(
            dimension_semantics=("arbitrary",), vmem_limit_bytes=VMEM_LIMIT_BYTES),
        name="outproj",
    )(yag, ybg, x2d, w_out_p, g_post)


def _pad_in_weights(w):
    perm = _head_perm()
    nat = np.cumsum([0, D_A, D_A, D_A, D_B, KV_W, KV_W, D_B, N_IDX_HEADS * D_IDX, D_IDX,
                     N_IDX_HEADS])
    seg = lambda n: w[:, nat[n]:nat[n + 1]]
    pad = jnp.zeros((w.shape[0], C_END - C_WI - N_IDX_HEADS), w.dtype)
    return jnp.concatenate(
        [seg(0), seg(1), seg(2), seg(3)[:, perm], seg(4), seg(5), seg(6)[:, perm], seg(7),
         seg(8), seg(8), seg(9), pad], axis=1).astype(_MXU_DTYPE)


def _layer(x_prompt, x_sample, cache_k, cache_v, cache_idx_k, page_table, g_pre, w_in, ln_v_g,
           ln_v_b, w_s, b_s, w_out, g_post):
    nb, t, _ = x_prompt.shape
    nd, td, _ = x_sample.shape
    npages = page_table.shape[1]
    past = npages * PAGE_SIZE
    tm = 1024
    tq = 128
    tk = 1024
    tk_sample = 512

    w_pad = _pad_in_weights(w_in)
    perm = _head_perm()
    w_out_p = jnp.concatenate([w_out[:D_A], w_out[D_A:][perm]], axis=0).astype(_MXU_DTYPE)
    gpre = g_pre.reshape(1, D_MODEL)
    gpost = g_post.reshape(1, D_MODEL)
    lng = ln_v_g.reshape(1, D_A)
    lnb = ln_v_b.reshape(1, D_A)
    tril = jnp.tril(jnp.ones((CHUNK, CHUNK), dtype=bool))
    ws_tril = jnp.where(tril[None], w_s, jnp.zeros_like(w_s))

    xp = x_prompt.reshape(nb * t, D_MODEL)
    tpb = t // tm
    wmix_p = ws_tril.astype(_MXU_DTYPE)
    bmix_p = jnp.repeat(b_s.T, A_GROUP_DIM, axis=1)
    outs = _inproj(xp, _rope_tables(jnp.arange(t, dtype=jnp.int32)), lambda i: (i % tpb, 0),
                   gpre, w_pad, lng, lnb, wmix_p, bmix_p, tm=tm, state_rows=CHUNK,
                   state_shape=(nb, CHUNK, D_A), state_index=lambda i: (i // tpb, 0, 0), seqs=nb)
    yag, zbs, qs, qi, k_t, v_t, ki_t, wi, kbf, vt3, kibf, vstate_p = outs
    ybg = _attn_prompt(qs, qi, wi, zbs, kibf, kbf, vt3, nb=nb, t=t, tq=tq, tk=tk)
    y_prompt = _outproj(yag, ybg, xp, w_out_p, gpost, tm=tm).reshape(nb, t, D_MODEL)

    xs = x_sample.reshape(nd * td, D_MODEL)
    reps = CHUNK // td
    eye = jnp.eye(reps, dtype=w_s.dtype)
    wmix_s = jnp.stack([jnp.kron(eye, ws_tril[g, :td, :td]) for g in range(N_A_GROUPS)]
                       ).astype(_MXU_DTYPE)
    bmix_s = jnp.repeat(jnp.tile(b_s[:, :td], (1, reps)).T, A_GROUP_DIM, axis=1)
    tm_s = min(tm, nd * td)
    pos_s = past + (jnp.arange(tm_s, dtype=jnp.int32) % td)
    outs = _inproj(xs, _rope_tables(pos_s), lambda i: (0, 0), gpre, w_pad, lng, lnb, wmix_s,
                   bmix_s, tm=tm_s, state_rows=tm_s, state_shape=(nd * td, D_A),
                   state_index=lambda i: (i, 0))
    yag_s, zbs_s, qs_s, qi_s, k32_s, v32_s, ki32_s, wi_s, _, _, _, vstate_s = outs
    pool = cache_k.shape[0]
    ybg_s = _attn_sample(page_table, qs_s, qi_s, wi_s, zbs_s, k32_s, v32_s, ki32_s,
                         jnp.transpose(cache_k, (0, 2, 3, 1)).reshape(pool, KV_W, PAGE_SIZE),
                         jnp.transpose(cache_v, (0, 2, 3, 1)).reshape(pool, KV_W, PAGE_SIZE),
                         jnp.transpose(cache_idx_k, (0, 2, 1)), tk=tk_sample)
    y_sample = _outproj(yag_s, ybg_s, xs, w_out_p, gpost, tm=tm_s).reshape(nd, td, D_MODEL)

    return (y_prompt, y_sample,
            jnp.transpose(k_t.reshape(nb, N_KV_HEADS, HEAD_DIM, t), (0, 3, 1, 2)),
            jnp.transpose(v_t.reshape(nb, N_KV_HEADS, HEAD_DIM, t), (0, 3, 1, 2)),
            jnp.transpose(ki_t, (0, 2, 1)), vstate_p,
            k32_s.reshape(nd, td, N_KV_HEADS, HEAD_DIM), v32_s.reshape(nd, td, N_KV_HEADS, HEAD_DIM),
            ki32_s.reshape(nd, td, D_IDX), vstate_s.reshape(nd, td, D_A))


def kernel(x_prompt, x_sample, cache_k, cache_v, cache_idx_k, page_table, g_pre, w_in, ln_v_g,
           ln_v_b, w_s, b_s, w_out, g_post):
    xp, xs = x_prompt, x_sample
    per_layer = []
    for l in range(g_pre.shape[0]):
        outs = _layer(xp, xs, cache_k[l], cache_v[l], cache_idx_k[l], page_table, g_pre[l],
                      w_in[l], ln_v_g[l], ln_v_b[l], w_s[l], b_s[l], w_out[l], g_post[l])
        xp, xs = outs[0], outs[1]
        per_layer.append(outs[2:])
    stacked = tuple(jnp.stack(leaves, 0) for leaves in zip(*per_layer))
    return (xp, xs) + stacked
```

```python
import functools

import numpy as np
import jax
import jax.numpy as jnp
from jax import lax
from jax.scipy.special import ndtri
from jax.experimental import pallas as pl
from jax.experimental.pallas import tpu as pltpu

_F32 = jnp.float32
_MXU_DTYPE = jnp.bfloat16

D_MODEL = 1024
D_A = 512
D_B = 512
CHUNK = 128
N_A_GROUPS = 4
A_GROUP_DIM = D_A // N_A_GROUPS
HEAD_DIM = 64
N_HEADS = 8
N_KV_HEADS = 2
KV_W = N_KV_HEADS * HEAD_DIM
ROT_DIM = HEAD_DIM // 4
N_IDX_HEADS = 8
D_IDX = 64
TOP_K_MAX = 256
ROPE_THETA = 500000.0
EPS = 1e-6
PAGE_SIZE = 128

LANES = 128
VMEM_LIMIT_BYTES = 56 * 1024 * 1024

C_U, C_V, C_ZA, C_Q, C_K, C_VV, C_ZB, C_QI, C_KI, C_WI, C_END = (
    0, 512, 1024, 1536, 2048, 2176, 2304, 2816, 3328, 3456, 3584)

_FLT_MAX = float(np.finfo(np.float32).max)
_NEG_INF = float("-inf")
_INT_MIN = -2 ** 31


def _head_perm():
    c = np.arange(N_HEADS * HEAD_DIM)
    j = c // LANES
    half = (c % LANES) // HEAD_DIM
    d = c % HEAD_DIM
    return (j + 4 * half) * HEAD_DIM + d


def _rope_tables(pos):
    half = ROT_DIM // 2
    inv = jnp.power(jnp.float32(ROPE_THETA), -jnp.arange(half, dtype=_F32) * 2.0 / ROT_DIM)
    ang = pos.astype(_F32)[:, None] * inv[None, :]
    cos = jnp.cos(ang)
    sin = jnp.sin(ang)
    l64 = np.arange(LANES) % HEAD_DIM
    fidx = l64 % half
    in_rot = (l64 < ROT_DIM)[None, :]
    first = (l64 < half)[None, :]
    second = ((l64 >= half) & (l64 < ROT_DIM))[None, :]
    cos_t = jnp.where(in_rot, cos[:, fidx], 1.0)
    sin_a = jnp.where(first, -sin[:, fidx], 0.0)
    sin_b = jnp.where(second, sin[:, fidx], 0.0)
    return cos_t, sin_a, sin_b


def _silu(z):
    return z * (1.0 / (1.0 + jnp.exp(-z)))


def _inproj_kernel(x_ref, cos_ref, sa_ref, sb_ref, gpre_ref, win_ref, lng_ref, lnb_ref,
                   wmix_ref, bmix_ref,
                   yag_ref, zbs_ref, qs_ref, qi_ref, k32_ref, v32_ref, ki32_ref, wi_ref,
                   kbf_ref, vt_ref, kibf_ref, vstate_ref, *, tm, state_rows, kv_transposed):
    x = x_ref[...]
    ms = jnp.mean(x * x, axis=-1, keepdims=True)
    h = (x * lax.rsqrt(ms + EPS) * gpre_ref[...]).astype(_MXU_DTYPE)

    def proj(a, b):
        return jnp.dot(h, win_ref[:, a:b], preferred_element_type=_F32)

    cos_t = cos_ref[...]
    sin_a = sa_ref[...]
    sin_b = sb_ref[...]

    def rope(p):
        blocks = []
        for j in range(p.shape[1] // LANES):
            pj = p[:, j * LANES:(j + 1) * LANES]
            blocks.append(pj * cos_t
                          + pltpu.roll(pj, LANES - ROT_DIM // 2, 1) * sin_a
                          + pltpu.roll(pj, ROT_DIM // 2, 1) * sin_b)
        return blocks[0] if len(blocks) == 1 else jnp.concatenate(blocks, axis=1)

    u = proj(C_U, C_V)
    v = proj(C_V, C_ZA)
    mu = jnp.mean(v, axis=-1, keepdims=True)
    vc = v - mu
    var = jnp.mean(vc * vc, axis=-1, keepdims=True)
    vn = vc * lax.rsqrt(var + EPS) * lng_ref[...] + lnb_ref[...]
    vstate_ref[...] = vn[tm - state_rows:, :].reshape(vstate_ref.shape)
    vnb = vn.astype(_MXU_DTYPE)
    bmix = bmix_ref[...]
    rows = []
    for c in range(tm // CHUNK):
        cols = []
        for g in range(N_A_GROUPS):
            blk = vnb[c * CHUNK:(c + 1) * CHUNK, g * A_GROUP_DIM:(g + 1) * A_GROUP_DIM]
            cols.append(jnp.dot(wmix_ref[g], blk, preferred_element_type=_F32)
                        + bmix[:, g * A_GROUP_DIM:(g + 1) * A_GROUP_DIM])
        rows.append(jnp.concatenate(cols, axis=1))
    mixed = jnp.concatenate(rows, axis=0)
    za = proj(C_ZA, C_Q)
    yag_ref[...] = ((u * mixed) * _silu(za)).astype(yag_ref.dtype)

    q = rope(proj(C_Q, C_K))
    qs_ref[...] = (q * (HEAD_DIM ** -0.5)).astype(qs_ref.dtype)
    k = rope(proj(C_K, C_VV))
    kbf_ref[...] = k.astype(kbf_ref.dtype)
    vv = proj(C_VV, C_ZB)
    vv_t = vv.T
    vt_ref[0] = vv_t.astype(vt_ref.dtype)
    if kv_transposed:
        k32_ref[0] = k.T
        v32_ref[0] = vv_t
    else:
        k32_ref[...] = k
        v32_ref[...] = vv
    zb = proj(C_ZB, C_QI)
    zbs_ref[...] = _silu(zb)
    qi = rope(proj(C_QI, C_KI))
    qi_ref[...] = qi.astype(qi_ref.dtype)
    ki = rope(proj(C_KI, C_WI))
    if kv_transposed:
        ki32_ref[0] = ki.T[:D_IDX, :]
    else:
        ki32_ref[...] = ki[:, :D_IDX]
    kibf_ref[...] = ki.astype(kibf_ref.dtype)
    wi_ref[...] = proj(C_WI, C_END) * (N_IDX_HEADS ** -0.5)


def _inproj(x2d, tables, table_index, gpre, w_pad, lng, lnb, wmix, bmix, *, tm, state_rows,
            state_shape, state_index, seqs=None):
    r = x2d.shape[0]
    nt = r // tm
    if seqs is None:
        kv = lambda w: (jax.ShapeDtypeStruct((r, w), _F32), pl.BlockSpec((tm, w), lambda i: (i, 0)))
    else:
        tps = nt // seqs
        kv = lambda w: (jax.ShapeDtypeStruct((seqs, w, r // seqs), _F32),
                        pl.BlockSpec((1, w, tm), lambda i: (i // tps, 0, i % tps)))
    cos_t, sin_a, sin_b = tables
    row = lambda w: pl.BlockSpec((tm, w), lambda i: (i, 0))
    const = lambda shape: pl.BlockSpec(shape, lambda i: (0,) * len(shape))
    tab = pl.BlockSpec((tm, LANES), table_index)
    out_shapes = (
        jax.ShapeDtypeStruct((r, D_A), _MXU_DTYPE),
        jax.ShapeDtypeStruct((r, D_B), _F32),
        jax.ShapeDtypeStruct((r, D_B), _MXU_DTYPE),
        jax.ShapeDtypeStruct((r, N_IDX_HEADS * D_IDX), _MXU_DTYPE),
        kv(KV_W)[0],
        kv(KV_W)[0],
        kv(D_IDX)[0],
        jax.ShapeDtypeStruct((r, LANES), _F32),
        jax.ShapeDtypeStruct((r, KV_W), _MXU_DTYPE),
        jax.ShapeDtypeStruct((nt, KV_W, tm), _MXU_DTYPE),
        jax.ShapeDtypeStruct((r, 2 * D_IDX), _MXU_DTYPE),
        jax.ShapeDtypeStruct(state_shape, _F32),
    )
    out_specs = (
        row(D_A), row(D_B), row(D_B), row(N_IDX_HEADS * D_IDX), kv(KV_W)[1], kv(KV_W)[1],
        kv(D_IDX)[1], row(LANES), row(KV_W),
        pl.BlockSpec((1, KV_W, tm), lambda i: (i, 0, 0)),
        row(2 * D_IDX),
        pl.BlockSpec((1,) * (len(state_shape) - 2) + (state_rows, D_A), state_index),
    )
    return pl.pallas_call(
        functools.partial(_inproj_kernel, tm=tm, state_rows=state_rows,
                          kv_transposed=seqs is not None),
        grid=(nt,),
        in_specs=[row(D_MODEL), tab, tab, tab, const((1, D_MODEL)), const((D_MODEL, C_END)),
                  const((1, D_A)), const((1, D_A)), const((N_A_GROUPS, CHUNK, CHUNK)),
                  const((CHUNK, D_A))],
        out_specs=out_specs,
        out_shape=out_shapes,
        compiler_params=pltpu.CompilerParams(
            dimension_semantics=("arbitrary",), vmem_limit_bytes=VMEM_LIMIT_BYTES),
        name="inproj",
    )(x2d, cos_t, sin_a, sin_b, gpre, w_pad, lng, lnb, wmix, bmix)


def _f32_to_key(x):
    b = lax.bitcast_convert_type(x, jnp.int32)
    return jnp.where(b >= 0, b, b ^ jnp.int32(0x7FFFFFFF))


def _key_to_f32(k):
    return lax.bitcast_convert_type(jnp.where(k >= 0, k, k ^ jnp.int32(0x7FFFFFFF)), _F32)


N_INTERP_PASSES = 10
HUGE_SUM = 1e30


def _topk_threshold(count_ge, max_below, smin, smax, n_adm, f0, k_sel):
    kf = jnp.float32(k_sel)
    log_k = jnp.log(kf - 0.5)
    lo0 = _f32_to_key(smin)
    hi0 = _f32_to_key(smax) + 1
    short = n_adm <= kf
    done0 = jnp.where(jnp.logical_or(short, hi0 <= lo0 + 1), 1.0, 0.0)
    g_hi0 = jnp.full(smin.shape, np.log(0.5), _F32) - log_k

    def all_done(done):
        return jnp.min(done) > 0.5

    def probe(carry, frac):
        it, lo, hi, clo, chi, glo, ghi, last, done = carry
        xlo = _key_to_f32(lo)
        xhi = _key_to_f32(hi)
        cand = jnp.clip(_f32_to_key(xlo + (xhi - xlo) * frac), lo + 1, hi - 1)
        cnt = count_ge(_key_to_f32(cand))
        g = jnp.log(jnp.maximum(cnt, 0.5)) - log_k
        active = done < 0.5
        new_lo = jnp.logical_and(active, cnt >= kf)
        new_hi = jnp.logical_and(active, cnt < kf)
        ghi = jnp.where(jnp.logical_and(new_lo, last == 1), ghi * 0.5, ghi)
        glo = jnp.where(jnp.logical_and(new_hi, last == 2), glo * 0.5, glo)
        lo = jnp.where(new_lo, cand, lo)
        clo = jnp.where(new_lo, cnt, clo)
        glo = jnp.where(new_lo, g, glo)
        hi = jnp.where(new_hi, cand, hi)
        chi = jnp.where(new_hi, cnt, chi)
        ghi = jnp.where(new_hi, g, ghi)
        last = jnp.where(new_lo, 1, jnp.where(new_hi, 2, last))
        done = jnp.where(jnp.logical_or(clo == kf, hi <= lo + 1), 1.0, done)
        return it + 1, lo, hi, clo, chi, glo, ghi, last, done

    def interp_body(_, carry):
        return probe(carry, carry[5] / (carry[5] - carry[6]))

    init = (jnp.int32(0), lo0, hi0, n_adm, jnp.zeros_like(n_adm),
            jnp.log(jnp.maximum(n_adm, 1.0)) - log_k, g_hi0, jnp.zeros(smin.shape, jnp.int32),
            done0)
    state = lax.fori_loop(1, N_INTERP_PASSES, interp_body, probe(init, f0))
    _, lo, hi, clo, chi, _, _, _, done = state

    def peel_cond(carry):
        return jnp.logical_not(all_done(carry[-1]))

    def peel_body(carry):
        lo, hi, clo, chi, done = carry
        top = max_below(_key_to_f32(hi))
        cnt = count_ge(top)
        active = done < 0.5
        hit = jnp.logical_and(active, cnt >= kf)
        miss = jnp.logical_and(active, cnt < kf)
        lo = jnp.where(hit, _f32_to_key(top), lo)
        clo = jnp.where(hit, cnt, clo)
        hi = jnp.where(miss, _f32_to_key(top), hi)
        chi = jnp.where(miss, cnt, chi)
        done = jnp.where(hit, 1.0, done)
        return lo, hi, clo, chi, done

    lo, _, clo, chi, _ = lax.while_loop(peel_cond, peel_body, (lo, hi, clo, chi, done))
    thr = jnp.where(short, -_FLT_MAX, _key_to_f32(lo))
    return thr, clo, chi, short


def _seed_fraction(n_adm, k_sel):
    n = n_adm.astype(_F32)
    z_k = ndtri(1.0 - jnp.minimum(k_sel / n, 0.999))
    z_n = ndtri(1.0 - 1.0 / (n + 1.0))
    return (0.5 + 0.5 * z_k / z_n).astype(_F32)


def _attn_prompt_kernel(qs_ref, qi_ref, wi_ref, zbs_ref, f0_ref, ki_ref, k_ref, vt_ref, out_ref,
                        s_ref, q8_ref, qi8_ref, bias_ref, pt_ref, acc_ref, m_ref, l_ref,
                        scale_ref, *, tq, tk, k_sel):
    i = pl.program_id(1)
    ratio = tk // tq
    nchunks = lax.div(i + ratio, ratio)
    kf = jnp.float32(k_sel)

    lane = lax.broadcasted_iota(jnp.int32, (tq, LANES), 1)
    lo_half = lane < HEAD_DIM
    for j in range(N_HEADS // 2):
        for src, dst in ((qs_ref, q8_ref), (qi_ref, qi8_ref)):
            blk = src[:, j * LANES:(j + 1) * LANES].astype(_F32)
            dst[(2 * j) * tq:(2 * j + 1) * tq, :] = jnp.where(lo_half, blk, 0.0).astype(dst.dtype)
            dst[(2 * j + 1) * tq:(2 * j + 2) * tq, :] = jnp.where(lo_half, 0.0, blk).astype(dst.dtype)
    w_t = wi_ref[...].T

    row = lax.broadcasted_iota(jnp.int32, (tk, tq), 0)
    col = lax.broadcasted_iota(jnp.int32, (tk, tq), 1)
    key_minus_query = row - col
    nt_dims = (((1,), (1,)), ((), ()))
    fold = lambda x: x.reshape(tk // 32, 32, tq)

    def scores(c, carry):
        mn, mx = carry
        k0 = pl.multiple_of(c * tk, tk)
        rt = lax.dot_general(ki_ref[pl.ds(k0, tk), :], qi8_ref[...], nt_dims,
                             preferred_element_type=_F32)
        s = None
        for hb in range(N_IDX_HEADS):
            t = jnp.maximum(rt[:, hb * tq:(hb + 1) * tq], 0.0) * w_t[hb:hb + 1, :]
            s = t if s is None else s + t
        s_ref[c] = jnp.where(key_minus_query <= i * tq - c * tk, s, _NEG_INF)
        return jnp.minimum(mn, fold(s).min(axis=0)), jnp.maximum(mx, fold(s).max(axis=0))

    mn, mx = lax.fori_loop(0, nchunks, scores,
                           (jnp.full((32, tq), jnp.inf, _F32), jnp.full((32, tq), -jnp.inf, _F32)))
    smin = jnp.min(mn, axis=0, keepdims=True)
    smax = jnp.max(mx, axis=0, keepdims=True)

    def count_ge(t):
        def body(c, acc):
            return acc + fold(jnp.where(s_ref[c] >= t, 1.0, 0.0)).sum(axis=0)
        acc = lax.fori_loop(0, nchunks, body, jnp.zeros((32, tq), _F32))
        return jnp.sum(acc, axis=0, keepdims=True)

    def max_below(x):
        def body(c, acc):
            s = s_ref[c]
            return jnp.maximum(acc, fold(jnp.where(s < x, s, _NEG_INF)).max(axis=0))
        acc = lax.fori_loop(0, nchunks, body, jnp.full((32, tq), -jnp.inf, _F32))
        return jnp.max(acc, axis=0, keepdims=True)

    n_adm = (i * tq + 1 + lax.broadcasted_iota(jnp.int32, (1, tq), 1)).astype(_F32)
    thr, clo, chi, short = _topk_threshold(count_ge, max_below, smin, smax, n_adm, f0_ref[0],
                                           k_sel)

    need = jnp.logical_and(clo > kf, jnp.logical_not(short))

    @pl.when(jnp.max(jnp.where(need, 1.0, 0.0)) > 0.5)
    def _():
        keep = jnp.where(need, kf - chi, _FLT_MAX)
        rk = lax.broadcasted_iota(jnp.int32, (tk, tk), 0)
        ck = lax.broadcasted_iota(jnp.int32, (tk, tk), 1)
        earlier = jnp.where(rk > ck, 1.0, 0.0).astype(_MXU_DTYPE)

        def drop(c, seen):
            s = s_ref[c]
            tie = s == thr
            tm = jnp.where(tie, 1.0, 0.0)
            before = jnp.dot(earlier, tm.astype(_MXU_DTYPE), preferred_element_type=_F32) + seen
            s_ref[c] = jnp.where(jnp.where(tie, before, -1.0) >= keep, _NEG_INF, s)
            return seen + jnp.sum(fold(tm).sum(axis=0), axis=0, keepdims=True)

        lax.fori_loop(0, nchunks, drop, jnp.zeros((1, tq), _F32))

    def chunk_probs(c, running_max):
        bias_ref[...] = jnp.where(s_ref[c] >= thr, 0.0, _NEG_INF)
        k_chunk = k_ref[pl.ds(pl.multiple_of(c * tk, tk), tk), :]
        slot = lax.rem(c, 2)
        for j in range(N_HEADS // 2):
            l2 = lax.dot_general(k_chunk, q8_ref[(2 * j) * tq:(2 * j + 2) * tq, :], nt_dims,
                                 preferred_element_type=_F32)
            for half in range(2):
                hb = 2 * j + half
                lh = l2[:, half * tq:(half + 1) * tq] + bias_ref[...]
                m_old = m_ref[hb:hb + 1, :]
                if running_max:
                    m_new = jnp.maximum(m_old,
                                        jnp.max(fold(lh).max(axis=0), axis=0, keepdims=True))
                    alpha = jnp.exp(m_old - m_new)
                    m_ref[hb:hb + 1, :] = m_new
                    scale_ref[0:1, hb * tq:(hb + 1) * tq] = alpha
                    p = jnp.exp(lh - m_new)
                    l_ref[hb:hb + 1, :] = (alpha * l_ref[hb:hb + 1, :]
                                           + jnp.sum(fold(p).sum(axis=0), axis=0, keepdims=True))
                else:
                    p = jnp.exp(lh - m_old)
                    l_ref[hb:hb + 1, :] = (l_ref[hb:hb + 1, :]
                                           + jnp.sum(fold(p).sum(axis=0), axis=0, keepdims=True))
                pt_ref[slot, :, hb * tq:(hb + 1) * tq] = p.astype(pt_ref.dtype)

    def values(c):
        return jnp.dot(vt_ref[c], pt_ref[lax.rem(c, 2)], preferred_element_type=_F32)

    def reset():
        m_ref[...] = jnp.full(m_ref.shape, -_FLT_MAX, _F32)
        l_ref[...] = jnp.zeros(l_ref.shape, _F32)
        acc_ref[...] = jnp.zeros(acc_ref.shape, _F32)

    reset()
    chunk_probs(0, running_max=True)

    def attend(c, carry):
        acc_ref[...] = acc_ref[...] + values(c - 1)
        chunk_probs(c, running_max=False)
        return carry

    lax.fori_loop(1, nchunks, attend, 0)
    acc_ref[...] = acc_ref[...] + values(nchunks - 1)

    sane = jnp.logical_and(jnp.max(l_ref[...]) < HUGE_SUM,
                           jnp.max(jnp.abs(acc_ref[...])) < HUGE_SUM)
    sane = jnp.logical_and(sane, jnp.min(m_ref[...]) > -_FLT_MAX)

    @pl.when(jnp.logical_not(sane))
    def _():
        reset()

        def attend_rescaling(c, carry):
            chunk_probs(c, running_max=True)
            acc_ref[...] = acc_ref[...] * scale_ref[0:1, :] + values(c)
            return carry

        lax.fori_loop(0, nchunks, attend_rescaling, 0)

    l_all = jnp.concatenate([l_ref[hb:hb + 1, :] for hb in range(N_HEADS)], axis=1)
    o_t = acc_ref[...] / l_all
    blocks = []
    for j in range(N_HEADS // 2):
        mj = jnp.concatenate(
            [o_t[:HEAD_DIM, (2 * j) * tq:(2 * j + 1) * tq],
             o_t[HEAD_DIM:, (2 * j + 1) * tq:(2 * j + 2) * tq]], axis=0)
        blocks.append(mj.T)
    yb = jnp.concatenate(blocks, axis=1)
    out_ref[...] = (yb * zbs_ref[...]).astype(out_ref.dtype)


def _attn_prompt(qs, qi, wi, zbs, kibf, kbf, vt3, *, nb, t, tq, tk):
    nq = t // tq
    k_sel = min(TOP_K_MAX, t // 4)
    f0 = _seed_fraction(jnp.arange(1, t + 1), k_sel).reshape(nq, 1, tq)
    blk = lambda w: pl.BlockSpec((tq, w), lambda b, i: (b * nq + i, 0))
    res = lambda w: pl.BlockSpec((t, w), lambda b, i: (b, 0))
    stat = pltpu.VMEM((N_HEADS, tq), _F32)
    kernel = functools.partial(_attn_prompt_kernel, tq=tq, tk=tk, k_sel=k_sel)
    return pl.pallas_call(
        kernel,
        grid=(nb, nq),
        in_specs=[blk(D_B), blk(N_IDX_HEADS * D_IDX), blk(LANES), blk(D_B),
                  pl.BlockSpec((1, 1, tq), lambda b, i: (i, 0, 0)),
                  res(2 * D_IDX), res(KV_W),
                  pl.BlockSpec((t // tk, KV_W, tk), lambda b, i: (b, 0, 0))],
        out_specs=blk(D_B),
        out_shape=jax.ShapeDtypeStruct((nb * t, D_B), _MXU_DTYPE),
        scratch_shapes=[
            pltpu.VMEM((t // tk, tk, tq), _F32),
            pltpu.VMEM((N_HEADS * tq, LANES), _MXU_DTYPE),
            pltpu.VMEM((N_IDX_HEADS * tq, LANES), _MXU_DTYPE),
            pltpu.VMEM((tk, tq), _F32),
            pltpu.VMEM((2, tk, N_HEADS * tq), _MXU_DTYPE),
            pltpu.VMEM((KV_W, N_HEADS * tq), _F32),
            stat, stat,
            pltpu.VMEM((1, N_HEADS * tq), _F32),
        ],
        compiler_params=pltpu.CompilerParams(
            dimension_semantics=("arbitrary", "arbitrary"), vmem_limit_bytes=VMEM_LIMIT_BYTES),
        name="attn_prompt",
    )(qs, qi, wi, zbs, f0, kibf, kbf, vt3)


def _stream_pages(pt_ref, b, nb, npages, sources, buffers, sems):
    slot = lax.rem(b, 2)

    def copies(bb, sl, p):
        phys = pt_ref[bb * npages + p]
        dst = pl.ds(pl.multiple_of(p * PAGE_SIZE, PAGE_SIZE), PAGE_SIZE)
        return [pltpu.make_async_copy(src.at[phys], buf.at[sl, :, dst], sems.at[i, sl])
                for i, (src, buf) in enumerate(zip(sources, buffers))]

    def start(bb, sl):
        def body(p, carry):
            for cp in copies(bb, sl, p):
                cp.start()
            return carry
        lax.fori_loop(0, npages, body, 0)

    def wait(bb, sl):
        def body(p, carry):
            for cp in copies(bb, sl, p):
                cp.wait()
            return carry
        lax.fori_loop(0, npages, body, 0)

    @pl.when(b == 0)
    def _():
        past = npages * PAGE_SIZE
        for buf in buffers:
            for sl in range(2):
                buf[sl, :, past:] = jnp.zeros((buf.shape[1], buf.shape[2] - past), buf.dtype)
        start(0, 0)

    @pl.when(b + 1 < nb)
    def _():
        start(b + 1, 1 - slot)

    wait(b, slot)
    return slot


def _put_new_keys(buf, slot, past, rows):
    tq, feat = rows.shape
    tile = jnp.concatenate([rows, jnp.zeros((PAGE_SIZE - tq, feat), _F32)], axis=0)
    if feat < LANES:
        tile = jnp.concatenate([tile, jnp.zeros((PAGE_SIZE, LANES - feat), _F32)], axis=1)
    buf[slot, :, past:past + PAGE_SIZE] = tile.T[:feat, :].astype(buf.dtype)


def _sample_scores_kernel(pt_ref, qi_ref, wi_ref, kin_ref, ci_hbm, s_ref, ibuf, sems,
                          *, npages, tk, nchunks):
    b = pl.program_id(0)
    past = npages * PAGE_SIZE
    tq = qi_ref.shape[1]
    slot = _stream_pages(pt_ref, b, pl.num_programs(0), npages, [ci_hbm], [ibuf], sems)
    _put_new_keys(ibuf, slot, past, kin_ref[0])

    qi = qi_ref[0].astype(_F32)
    wi = wi_ref[0]
    qi8 = jnp.concatenate([qi[:, h * D_IDX:(h + 1) * D_IDX] for h in range(N_IDX_HEADS)],
                          axis=0).astype(_MXU_DTYPE)
    w_wide = [jnp.broadcast_to(wi[:, h:h + 1], (tq, tk)) for h in range(N_IDX_HEADS)]
    key_minus_query = (lax.broadcasted_iota(jnp.int32, (tq, tk), 1)
                       - lax.broadcasted_iota(jnp.int32, (tq, tk), 0))
    for c in range(nchunks):
        keys_t = ibuf[slot, :, c * tk:(c + 1) * tk].astype(_MXU_DTYPE)
        r = jnp.dot(qi8, keys_t, preferred_element_type=_F32)
        s = None
        for h in range(N_IDX_HEADS):
            t = jnp.maximum(r[h * tq:(h + 1) * tq, :], 0.0) * w_wide[h]
            s = t if s is None else s + t
        s_ref[0, c] = jnp.where(key_minus_query <= past - c * tk, s, _NEG_INF)


def _sample_threshold_kernel(s_ref, f0_ref, nadm_ref, out_ref, thr_ref, *, tk, nchunks, k_sel):
    rows = s_ref.shape[2]
    kf = jnp.float32(k_sel)
    groups = tk // LANES
    tile = lambda col: jnp.broadcast_to(col, (rows, tk))

    def fold_with(x, op):
        out = x[:, :LANES]
        for g in range(1, groups):
            out = op(out, x[:, g * LANES:(g + 1) * LANES])
        return out

    def over_chunks(fn, op, init):
        def body(c, acc):
            return op(acc, fold_with(fn(s_ref[0, c]), op))
        return lax.fori_loop(0, nchunks, body, jnp.full((rows, LANES), init, _F32))

    def count_ge(t):
        t_wide = tile(t)
        acc = over_chunks(lambda s: jnp.where(s >= t_wide, 1.0, 0.0), jnp.add, 0.0)
        return jnp.sum(acc, axis=1, keepdims=True)

    def max_below(x):
        x_wide = tile(x)
        acc = over_chunks(lambda s: jnp.where(s < x_wide, s, _NEG_INF), jnp.maximum, -jnp.inf)
        return jnp.max(acc, axis=1, keepdims=True)

    finite_min = over_chunks(lambda s: jnp.where(s > _NEG_INF, s, jnp.inf), jnp.minimum, jnp.inf)
    finite_max = over_chunks(lambda s: s, jnp.maximum, -jnp.inf)
    smin = jnp.min(finite_min, axis=1, keepdims=True)
    smax = jnp.max(finite_max, axis=1, keepdims=True)
    thr, clo, chi, short = _topk_threshold(count_ge, max_below, smin, smax, nadm_ref[...],
                                           f0_ref[...], k_sel)
    thr_ref[...] = thr
    need = jnp.logical_and(clo > kf, jnp.logical_not(short))
    any_ties = jnp.max(jnp.where(need, 1.0, 0.0)) > 0.5

    @pl.when(jnp.logical_not(any_ties))
    def _():
        out_ref[...] = s_ref[...]

    @pl.when(any_ties)
    def _():
        keep_wide = tile(jnp.where(need, kf - chi, _FLT_MAX))
        thr_wide = tile(thr)
        rk = lax.broadcasted_iota(jnp.int32, (tk, tk), 0)
        ck = lax.broadcasted_iota(jnp.int32, (tk, tk), 1)
        earlier = jnp.where(rk < ck, 1.0, 0.0).astype(_MXU_DTYPE)

        def drop(c, seen):
            s = s_ref[0, c]
            tie = s == thr_wide
            tm = jnp.where(tie, 1.0, 0.0)
            before = (jnp.dot(tm.astype(_MXU_DTYPE), earlier, preferred_element_type=_F32)
                      + tile(seen))
            out_ref[0, c] = jnp.where(jnp.where(tie, before, -1.0) >= keep_wide, _NEG_INF, s)
            return seen + jnp.sum(tm, axis=1, keepdims=True)

        lax.fori_loop(0, nchunks, drop, jnp.zeros((rows, 1), _F32))


def _sample_attend_kernel(pt_ref, qs_ref, zbs_ref, kn_ref, vn_ref, s_ref, thr_ref,
                          ck_hbm, cv_hbm, out_ref, kbuf, vbuf, sems, l_ref,
                          *, npages, tk, nchunks):
    b = pl.program_id(0)
    past = npages * PAGE_SIZE
    tq = qs_ref.shape[1]
    slot = _stream_pages(pt_ref, b, pl.num_programs(0), npages, [ck_hbm, cv_hbm], [kbuf, vbuf],
                         sems)
    _put_new_keys(kbuf, slot, past, kn_ref[0])
    _put_new_keys(vbuf, slot, past, vn_ref[0])

    qs = qs_ref[0].astype(_F32)
    lane = lax.broadcasted_iota(jnp.int32, (tq, LANES), 1)
    lo_half = lane < HEAD_DIM
    q8_rows = []
    for j in range(N_HEADS // 2):
        blk = qs[:, j * LANES:(j + 1) * LANES]
        q8_rows.append(jnp.where(lo_half, blk, 0.0))
        q8_rows.append(jnp.where(lo_half, 0.0, blk))
    q8 = jnp.concatenate(q8_rows, axis=0).astype(_MXU_DTYPE)
    thr_wide = jnp.broadcast_to(thr_ref[...], (tq, tk))

    mx = jnp.full((N_HEADS * tq, tk), -jnp.inf, _F32)
    for c in range(nchunks):
        keys_t = kbuf[slot, :, c * tk:(c + 1) * tk].astype(_MXU_DTYPE)
        lg = jnp.dot(q8, keys_t, preferred_element_type=_F32)
        sel = s_ref[0, c] >= thr_wide
        lg = jnp.concatenate(
            [jnp.where(sel, lg[h * tq:(h + 1) * tq, :], _NEG_INF) for h in range(N_HEADS)], axis=0)
        l_ref[c] = lg
        mx = jnp.maximum(mx, lg)
    m_wide = jnp.broadcast_to(jnp.max(mx, axis=1, keepdims=True), (N_HEADS * tq, tk))

    nt_dims = (((1,), (1,)), ((), ()))
    psum = jnp.zeros((N_HEADS * tq, tk), _F32)
    acc = jnp.zeros((N_HEADS * tq, KV_W), _F32)
    for c in range(nchunks):
        p = jnp.exp(l_ref[c] - m_wide)
        psum = psum + p
        vals_t = vbuf[slot, :, c * tk:(c + 1) * tk].astype(_MXU_DTYPE)
        acc = acc + lax.dot_general(p.astype(_MXU_DTYPE), vals_t, nt_dims,
                                    preferred_element_type=_F32)
    o = acc / jnp.sum(psum, axis=1, keepdims=True)
    blocks = []
    for j in range(N_HEADS // 2):
        blocks.append(jnp.where(lo_half, o[(2 * j) * tq:(2 * j + 1) * tq, :],
                                o[(2 * j + 1) * tq:(2 * j + 2) * tq, :]))
    out_ref[0] = jnp.concatenate(blocks, axis=1) * zbs_ref[0]


def _attn_sample(page_table, qs, qi, wi, zbs, k_new, v_new, ki_new, cache_kt, cache_vt, cache_it,
                 *, tk):
    nb, npages = page_table.shape
    tq = qs.shape[0] // nb
    past = npages * PAGE_SIZE
    nchunks = -(-(past + PAGE_SIZE) // tk)
    keys = nchunks * tk
    k_sel = min(TOP_K_MAX, (past + tq) // 4)
    group = LANES // tq
    assert LANES % tq == 0 and nb % group == 0
    three = lambda a: a.reshape(nb, tq, a.shape[-1])
    blk = lambda w: pl.BlockSpec((1, tq, w), lambda b, pt: (b, 0, 0))
    hbm = pl.BlockSpec(memory_space=pl.ANY)
    params = pltpu.CompilerParams(dimension_semantics=("arbitrary",),
                                  vmem_limit_bytes=VMEM_LIMIT_BYTES)
    pt_flat = page_table.reshape(-1)
    s_shape = jax.ShapeDtypeStruct((nb // group, nchunks, LANES, tk), _F32)
    s_blk = pl.BlockSpec((1, nchunks, tq, tk), lambda b, pt: (b // group, 0, b % group, 0))

    scores = pl.pallas_call(
        functools.partial(_sample_scores_kernel, npages=npages, tk=tk, nchunks=nchunks),
        grid_spec=pltpu.PrefetchScalarGridSpec(
            num_scalar_prefetch=1, grid=(nb,),
            in_specs=[blk(N_IDX_HEADS * D_IDX), blk(LANES), blk(D_IDX), hbm],
            out_specs=s_blk,
            scratch_shapes=[pltpu.VMEM((2, D_IDX, keys), cache_it.dtype),
                            pltpu.SemaphoreType.DMA((1, 2))]),
        out_shape=s_shape, compiler_params=params, name="sample_scores",
    )(pt_flat, three(qi), three(wi), three(ki_new), cache_it)

    n_adm = past + 1 + (jnp.arange(LANES) % tq)
    whole = pl.BlockSpec((1, nchunks, LANES, tk), lambda g: (g, 0, 0, 0))
    col = pl.BlockSpec((LANES, 1), lambda g: (0, 0))
    kept, thr = pl.pallas_call(
        functools.partial(_sample_threshold_kernel, tk=tk, nchunks=nchunks, k_sel=k_sel),
        grid=(nb // group,),
        in_specs=[whole, col, col],
        out_specs=(whole, pl.BlockSpec((LANES, 1), lambda g: (g, 0))),
        out_shape=(s_shape, jax.ShapeDtypeStruct((nb * tq, 1), _F32)),
        compiler_params=params, name="sample_threshold",
    )(scores, _seed_fraction(n_adm, k_sel).reshape(LANES, 1),
      n_adm.astype(_F32).reshape(LANES, 1))

    out = pl.pallas_call(
        functools.partial(_sample_attend_kernel, npages=npages, tk=tk, nchunks=nchunks),
        grid_spec=pltpu.PrefetchScalarGridSpec(
            num_scalar_prefetch=1, grid=(nb,),
            in_specs=[blk(D_B), blk(D_B), blk(KV_W), blk(KV_W), s_blk,
                      pl.BlockSpec((tq, 1), lambda b, pt: (b, 0)), hbm, hbm],
            out_specs=blk(D_B),
            scratch_shapes=[pltpu.VMEM((2, KV_W, keys), cache_kt.dtype),
                            pltpu.VMEM((2, KV_W, keys), cache_vt.dtype),
                            pltpu.SemaphoreType.DMA((2, 2)),
                            pltpu.VMEM((nchunks, N_HEADS * tq, tk), _F32)]),
        out_shape=jax.ShapeDtypeStruct((nb, tq, D_B), _F32),
        compiler_params=params, name="sample_attend",
    )(pt_flat, three(qs), three(zbs), three(k_new), three(v_new), kept, thr, cache_kt, cache_vt)
    return out.reshape(nb * tq, D_B)


def _outproj_kernel(yag_ref, ybg_ref, x_ref, w_ref, g_ref, out_ref):
    y = (jnp.dot(yag_ref[...], w_ref[:D_A, :], preferred_element_type=_F32)
         + jnp.dot(ybg_ref[...].astype(_MXU_DTYPE), w_ref[D_A:, :], preferred_element_type=_F32))
    ms = jnp.mean(y * y, axis=-1, keepdims=True)
    out_ref[...] = x_ref[...] + y * lax.rsqrt(ms + EPS) * g_ref[...]


def _outproj(yag, ybg, x2d, w_out_p, g_post, *, tm):
    r = x2d.shape[0]
    row = lambda w: pl.BlockSpec((tm, w), lambda i: (i, 0))
    const = lambda shape: pl.BlockSpec(shape, lambda i: (0,) * len(shape))
    return pl.pallas_call(
        _outproj_kernel,
        grid=(r // tm,),
        in_specs=[row(D_A), row(D_B), row(D_MODEL), const((D_A + D_B, D_MODEL)),
                  const((1, D_MODEL))],
        out_specs=row(D_MODEL),
        out_shape=jax.ShapeDtypeStruct((r, D_MODEL), _F32),
        compiler_params=pltpu.CompilerParams(
            dimension_semantics=("arbitrary",), vmem_limit_bytes=VMEM_LIMIT_BYTES),
        name="outproj",
    )(yag, ybg, x2d, w_out_p, g_post)


def _pad_in_weights(w):
    perm = _head_perm()
    nat = np.cumsum([0, D_A, D_A, D_A, D_B, KV_W, KV_W, D_B, N_IDX_HEADS * D_IDX, D_IDX,
                     N_IDX_HEADS])
    seg = lambda n: w[:, nat[n]:nat[n + 1]]
    pad = jnp.zeros((w.shape[0], C_END - C_WI - N_IDX_HEADS), w.dtype)
    return jnp.concatenate(
        [seg(0), seg(1), seg(2), seg(3)[:, perm], seg(4), seg(5), seg(6)[:, perm], seg(7),
         seg(8), seg(8), seg(9), pad], axis=1).astype(_MXU_DTYPE)


def _layer(x_prompt, x_sample, cache_k, cache_v, cache_idx_k, page_table, g_pre, w_in, ln_v_g,
           ln_v_b, w_s, b_s, w_out, g_post):
    nb, t, _ = x_prompt.shape
    nd, td, _ = x_sample.shape
    npages = page_table.shape[1]
    past = npages * PAGE_SIZE
    tm = 1024
    tq = 128
    tk = 1024
    tk_sample = 512

    w_pad = _pad_in_weights(w_in)
    perm = _head_perm()
    w_out_p = jnp.concatenate([w_out[:D_A], w_out[D_A:][perm]], axis=0).astype(_MXU_DTYPE)
    gpre = g_pre.reshape(1, D_MODEL)
    gpost = g_post.reshape(1, D_MODEL)
    lng = ln_v_g.reshape(1, D_A)
    lnb = ln_v_b.reshape(1, D_A)
    tril = jnp.tril(jnp.ones((CHUNK, CHUNK), dtype=bool))
    ws_tril = jnp.where(tril[None], w_s, jnp.zeros_like(w_s))

    xp = x_prompt.reshape(nb * t, D_MODEL)
    tpb = t // tm
    wmix_p = ws_tril.astype(_MXU_DTYPE)
    bmix_p = jnp.repeat(b_s.T, A_GROUP_DIM, axis=1)
    outs = _inproj(xp, _rope_tables(jnp.arange(t, dtype=jnp.int32)), lambda i: (i % tpb, 0),
                   gpre, w_pad, lng, lnb, wmix_p, bmix_p, tm=tm, state_rows=CHUNK,
                   state_shape=(nb, CHUNK, D_A), state_index=lambda i: (i // tpb, 0, 0), seqs=nb)
    yag, zbs, qs, qi, k_t, v_t, ki_t, wi, kbf, vt3, kibf, vstate_p = outs
    ybg = _attn_prompt(qs, qi, wi, zbs, kibf, kbf, vt3, nb=nb, t=t, tq=tq, tk=tk)
    y_prompt = _outproj(yag, ybg, xp, w_out_p, gpost, tm=tm).reshape(nb, t, D_MODEL)

    xs = x_sample.reshape(nd * td, D_MODEL)
    reps = CHUNK // td
    eye = jnp.eye(reps, dtype=w_s.dtype)
    wmix_s = jnp.stack([jnp.kron(eye, ws_tril[g, :td, :td]) for g in range(N_A_GROUPS)]
                       ).astype(_MXU_DTYPE)
    bmix_s = jnp.repeat(jnp.tile(b_s[:, :td], (1, reps)).T, A_GROUP_DIM, axis=1)
    tm_s = min(tm, nd * td)
    pos_s = past + (jnp.arange(tm_s, dtype=jnp.int32) % td)
    outs = _inproj(xs, _rope_tables(pos_s), lambda i: (0, 0), gpre, w_pad, lng, lnb, wmix_s,
                   bmix_s, tm=tm_s, state_rows=tm_s, state_shape=(nd * td, D_A),
                   state_index=lambda i: (i, 0))
    yag_s, zbs_s, qs_s, qi_s, k32_s, v32_s, ki32_s, wi_s, _, _, _, vstate_s = outs
    pool = cache_k.shape[0]
    ybg_s = _attn_sample(page_table, qs_s, qi_s, wi_s, zbs_s, k32_s, v32_s, ki32_s,
                         jnp.transpose(cache_k, (0, 2, 3, 1)).reshape(pool, KV_W, PAGE_SIZE),
                         jnp.transpose(cache_v, (0, 2, 3, 1)).reshape(pool, KV_W, PAGE_SIZE),
                         jnp.transpose(cache_idx_k, (0, 2, 1)), tk=tk_sample)
    y_sample = _outproj(yag_s, ybg_s, xs, w_out_p, gpost, tm=tm_s).reshape(nd, td, D_MODEL)

    return (y_prompt, y_sample,
            jnp.transpose(k_t.reshape(nb, N_KV_HEADS, HEAD_DIM, t), (0, 3, 1, 2)),
            jnp.transpose(v_t.reshape(nb, N_KV_HEADS, HEAD_DIM, t), (0, 3, 1, 2)),
            jnp.transpose(ki_t, (0, 2, 1)), vstate_p,
            k32_s.reshape(nd, td, N_KV_HEADS, HEAD_DIM), v32_s.reshape(nd, td, N_KV_HEADS, HEAD_DIM),
            ki32_s.reshape(nd, td, D_IDX), vstate_s.reshape(nd, td, D_A))


def kernel(x_prompt, x_sample, cache_k, cache_v, cache_idx_k, page_table, g_pre, w_in, ln_v_g,
           ln_v_b, w_s, b_s, w_out, g_post):
    xp, xs = x_prompt, x_sample
    per_layer = []
    for l in range(g_pre.shape[0]):
        outs = _layer(xp, xs, cache_k[l], cache_v[l], cache_idx_k[l], page_table, g_pre[l],
                      w_in[l], ln_v_g[l], ln_v_b[l], w_s[l], b_s[l], w_out[l], g_post[l])
        xp, xs = outs[0], outs[1]
        per_layer.append(outs[2:])
    stacked = tuple(jnp.stack(leaves, 0) for leaves in zip(*per_layer))
    return (xp, xs) + stacked
```

```python
import functools

import numpy as np
import jax
import jax.numpy as jnp
from jax import lax
from jax.scipy.special import ndtri
from jax.experimental import pallas as pl
from jax.experimental.pallas import tpu as pltpu

_F32 = jnp.float32
_MXU_DTYPE = jnp.bfloat16

D_MODEL = 1024
D_A = 512
D_B = 512
CHUNK = 128
N_A_GROUPS = 4
A_GROUP_DIM = D_A // N_A_GROUPS
HEAD_DIM = 64
N_HEADS = 8
N_KV_HEADS = 2
KV_W = N_KV_HEADS * HEAD_DIM
ROT_DIM = HEAD_DIM // 4
N_IDX_HEADS = 8
D_IDX = 64
TOP_K_MAX = 256
ROPE_THETA = 500000.0
EPS = 1e-6
PAGE_SIZE = 128

LANES = 128
VMEM_LIMIT_BYTES = 56 * 1024 * 1024

C_U, C_V, C_ZA, C_Q, C_K, C_VV, C_ZB, C_QI, C_KI, C_WI, C_END = (
    0, 512, 1024, 1536, 2048, 2176, 2304, 2816, 3328, 3456, 3584)

_FLT_MAX = float(np.finfo(np.float32).max)
_NEG_INF = float("-inf")
_INT_MIN = -2 ** 31


def _head_perm():
    c = np.arange(N_HEADS * HEAD_DIM)
    j = c // LANES
    half = (c % LANES) // HEAD_DIM
    d = c % HEAD_DIM
    return (j + 4 * half) * HEAD_DIM + d


def _rope_tables(pos):
    half = ROT_DIM // 2
    inv = jnp.power(jnp.float32(ROPE_THETA), -jnp.arange(half, dtype=_F32) * 2.0 / ROT_DIM)
    ang = pos.astype(_F32)[:, None] * inv[None, :]
    cos = jnp.cos(ang)
    sin = jnp.sin(ang)
    l64 = np.arange(LANES) % HEAD_DIM
    fidx = l64 % half
    in_rot = (l64 < ROT_DIM)[None, :]
    first = (l64 < half)[None, :]
    second = ((l64 >= half) & (l64 < ROT_DIM))[None, :]
    cos_t = jnp.where(in_rot, cos[:, fidx], 1.0)
    sin_a = jnp.where(first, -sin[:, fidx], 0.0)
    sin_b = jnp.where(second, sin[:, fidx], 0.0)
    return cos_t, sin_a, sin_b


def _silu(z):
    return z * (1.0 / (1.0 + jnp.exp(-z)))


def _inproj_kernel(x_ref, cos_ref, sa_ref, sb_ref, gpre_ref, win_ref, lng_ref, lnb_ref,
                   wmix_ref, bmix_ref,
                   yag_ref, zbs_ref, qs_ref, qi_ref, k32_ref, v32_ref, ki32_ref, wi_ref,
                   kbf_ref, vt_ref, kibf_ref, vstate_ref, *, tm, state_rows, kv_transposed):
    x = x_ref[...]
    ms = jnp.mean(x * x, axis=-1, keepdims=True)
    h = (x * lax.rsqrt(ms + EPS) * gpre_ref[...]).astype(_MXU_DTYPE)

    def proj(a, b):
        return jnp.dot(h, win_ref[:, a:b], preferred_element_type=_F32)

    cos_t = cos_ref[...]
    sin_a = sa_ref[...]
    sin_b = sb_ref[...]

    def rope(p):
        blocks = []
        for j in range(p.shape[1] // LANES):
            pj = p[:, j * LANES:(j + 1) * LANES]
            blocks.append(pj * cos_t
                          + pltpu.roll(pj, LANES - ROT_DIM // 2, 1) * sin_a
                          + pltpu.roll(pj, ROT_DIM // 2, 1) * sin_b)
        return blocks[0] if len(blocks) == 1 else jnp.concatenate(blocks, axis=1)

    u = proj(C_U, C_V)
    v = proj(C_V, C_ZA)
    mu = jnp.mean(v, axis=-1, keepdims=True)
    vc = v - mu
    var = jnp.mean(vc * vc, axis=-1, keepdims=True)
    vn = vc * lax.rsqrt(var + EPS) * lng_ref[...] + lnb_ref[...]
    vstate_ref[...] = vn[tm - state_rows:, :].reshape(vstate_ref.shape)
    vnb = vn.astype(_MXU_DTYPE)
    bmix = bmix_ref[...]
    rows = []
    for c in range(tm // CHUNK):
        cols = []
        for g in range(N_A_GROUPS):
            blk = vnb[c * CHUNK:(c + 1) * CHUNK, g * A_GROUP_DIM:(g + 1) * A_GROUP_DIM]
            cols.append(jnp.dot(wmix_ref[g], blk, preferred_element_type=_F32)
                        + bmix[:, g * A_GROUP_DIM:(g + 1) * A_GROUP_DIM])
        rows.append(jnp.concatenate(cols, axis=1))
    mixed = jnp.concatenate(rows, axis=0)
    za = proj(C_ZA, C_Q)
    yag_ref[...] = ((u * mixed) * _silu(za)).astype(yag_ref.dtype)

    q = rope(proj(C_Q, C_K))
    qs_ref[...] = (q * (HEAD_DIM ** -0.5)).astype(qs_ref.dtype)
    k = rope(proj(C_K, C_VV))
    kbf_ref[...] = k.astype(kbf_ref.dtype)
    vv = proj(C_VV, C_ZB)
    vv_t = vv.T
    vt_ref[0] = vv_t.astype(vt_ref.dtype)
    if kv_transposed:
        k32_ref[0] = k.T
        v32_ref[0] = vv_t
    else:
        k32_ref[...] = k
        v32_ref[...] = vv
    zb = proj(C_ZB, C_QI)
    zbs_ref[...] = _silu(zb)
    qi = rope(proj(C_QI, C_KI))
    qi_ref[...] = qi.astype(qi_ref.dtype)
    ki = rope(proj(C_KI, C_WI))
    if kv_transposed:
        ki32_ref[0] = ki.T[:D_IDX, :]
    else:
        ki32_ref[...] = ki[:, :D_IDX]
    kibf_ref[...] = ki.astype(kibf_ref.dtype)
    wi_ref[...] = proj(C_WI, C_END) * (N_IDX_HEADS ** -0.5)


def _inproj(x2d, tables, table_index, gpre, w_pad, lng, lnb, wmix, bmix, *, tm, state_rows,
            state_shape, state_index, seqs=None):
    r = x2d.shape[0]
    nt = r // tm
    if seqs is None:
        kv = lambda w: (jax.ShapeDtypeStruct((r, w), _F32), pl.BlockSpec((tm, w), lambda i: (i, 0)))
    else:
        tps = nt // seqs
        kv = lambda w: (jax.ShapeDtypeStruct((seqs, w, r // seqs), _F32),
                        pl.BlockSpec((1, w, tm), lambda i: (i // tps, 0, i % tps)))
    cos_t, sin_a, sin_b = tables
    row = lambda w: pl.BlockSpec((tm, w), lambda i: (i, 0))
    const = lambda shape: pl.BlockSpec(shape, lambda i: (0,) * len(shape))
    tab = pl.BlockSpec((tm, LANES), table_index)
    out_shapes = (
        jax.ShapeDtypeStruct((r, D_A), _MXU_DTYPE),
        jax.ShapeDtypeStruct((r, D_B), _F32),
        jax.ShapeDtypeStruct((r, D_B), _MXU_DTYPE),
        jax.ShapeDtypeStruct((r, N_IDX_HEADS * D_IDX), _MXU_DTYPE),
        kv(KV_W)[0],
        kv(KV_W)[0],
        kv(D_IDX)[0],
        jax.ShapeDtypeStruct((r, LANES), _F32),
        jax.ShapeDtypeStruct((r, KV_W), _MXU_DTYPE),
        jax.ShapeDtypeStruct((nt, KV_W, tm), _MXU_DTYPE),
        jax.ShapeDtypeStruct((r, 2 * D_IDX), _MXU_DTYPE),
        jax.ShapeDtypeStruct(state_shape, _F32),
    )
    out_specs = (
        row(D_A), row(D_B), row(D_B), row(N_IDX_HEADS * D_IDX), kv(KV_W)[1], kv(KV_W)[1],
        kv(D_IDX)[1], row(LANES), row(KV_W),
        pl.BlockSpec((1, KV_W, tm), lambda i: (i, 0, 0)),
        row(2 * D_IDX),
        pl.BlockSpec((1,) * (len(state_shape) - 2) + (state_rows, D_A), state_index),
    )
    return pl.pallas_call(
        functools.partial(_inproj_kernel, tm=tm, state_rows=state_rows,
                          kv_transposed=seqs is not None),
        grid=(nt,),
        in_specs=[row(D_MODEL), tab, tab, tab, const((1, D_MODEL)), const((D_MODEL, C_END)),
                  const((1, D_A)), const((1, D_A)), const((N_A_GROUPS, CHUNK, CHUNK)),
                  const((CHUNK, D_A))],
        out_specs=out_specs,
        out_shape=out_shapes,
        compiler_params=pltpu.CompilerParams(
            dimension_semantics=("arbitrary",), vmem_limit_bytes=VMEM_LIMIT_BYTES),
        name="inproj",
    )(x2d, cos_t, sin_a, sin_b, gpre, w_pad, lng, lnb, wmix, bmix)


def _f32_to_key(x):
    b = lax.bitcast_convert_type(x, jnp.int32)
    return jnp.where(b >= 0, b, b ^ jnp.int32(0x7FFFFFFF))


def _key_to_f32(k):
    return lax.bitcast_convert_type(jnp.where(k >= 0, k, k ^ jnp.int32(0x7FFFFFFF)), _F32)


N_INTERP_PASSES = 10
HUGE_SUM = 1e30


def _topk_threshold(count_ge, max_below, smin, smax, n_adm, f0, k_sel):
    kf = jnp.float32(k_sel)
    log_k = jnp.log(kf - 0.5)
    lo0 = _f32_to_key(smin)
    hi0 = _f32_to_key(smax) + 1
    short = n_adm <= kf
    done0 = jnp.where(jnp.logical_or(short, hi0 <= lo0 + 1), 1.0, 0.0)
    g_hi0 = jnp.full(smin.shape, np.log(0.5), _F32) - log_k

    def all_done(done):
        return jnp.min(done) > 0.5

    def probe(carry, frac):
        it, lo, hi, clo, chi, glo, ghi, last, done = carry
        xlo = _key_to_f32(lo)
        xhi = _key_to_f32(hi)
        cand = jnp.clip(_f32_to_key(xlo + (xhi - xlo) * frac), lo + 1, hi - 1)
        cnt = count_ge(_key_to_f32(cand))
        g = jnp.log(jnp.maximum(cnt, 0.5)) - log_k
        active = done < 0.5
        new_lo = jnp.logical_and(active, cnt >= kf)
        new_hi = jnp.logical_and(active, cnt < kf)
        ghi = jnp.where(jnp.logical_and(new_lo, last == 1), ghi * 0.5, ghi)
        glo = jnp.where(jnp.logical_and(new_hi, last == 2), glo * 0.5, glo)
        lo = jnp.where(new_lo, cand, lo)
        clo = jnp.where(new_lo, cnt, clo)
        glo = jnp.where(new_lo, g, glo)
        hi = jnp.where(new_hi, cand, hi)
        chi = jnp.where(new_hi, cnt, chi)
        ghi = jnp.where(new_hi, g, ghi)
        last = jnp.where(new_lo, 1, jnp.where(new_hi, 2, last))
        done = jnp.where(jnp.logical_or(clo == kf, hi <= lo + 1), 1.0, done)
        return it + 1, lo, hi, clo, chi, glo, ghi, last, done

    def interp_body(_, carry):
        return probe(carry, carry[5] / (carry[5] - carry[6]))

    init = (jnp.int32(0), lo0, hi0, n_adm, jnp.zeros_like(n_adm),
            jnp.log(jnp.maximum(n_adm, 1.0)) - log_k, g_hi0, jnp.zeros(smin.shape, jnp.int32),
            done0)
    state = lax.fori_loop(1, N_INTERP_PASSES, interp_body, probe(init, f0))
    _, lo, hi, clo, chi, _, _, _, done = state

    def peel_cond(carry):
        return jnp.logical_not(all_done(carry[-1]))

    def peel_body(carry):
        lo, hi, clo, chi, done = carry
        top = max_below(_key_to_f32(hi))
        cnt = count_ge(top)
        active = done < 0.5
        hit = jnp.logical_and(active, cnt >= kf)
        miss = jnp.logical_and(active, cnt < kf)
        lo = jnp.where(hit, _f32_to_key(top), lo)
        clo = jnp.where(hit, cnt, clo)
        hi = jnp.where(miss, _f32_to_key(top), hi)
        chi = jnp.where(miss, cnt, chi)
        done = jnp.where(hit, 1.0, done)
        return lo, hi, clo, chi, done

    lo, _, clo, chi, _ = lax.while_loop(peel_cond, peel_body, (lo, hi, clo, chi, done))
    thr = jnp.where(short, -_FLT_MAX, _key_to_f32(lo))
    return thr, clo, chi, short


def _seed_fraction(n_adm, k_sel):
    n = n_adm.astype(_F32)
    z_k = ndtri(1.0 - jnp.minimum(k_sel / n, 0.999))
    z_n = ndtri(1.0 - 1.0 / (n + 1.0))
    return (0.5 + 0.5 * z_k / z_n).astype(_F32)


def _attn_prompt_kernel(qs_ref, qi_ref, wi_ref, zbs_ref, f0_ref, qin_ref, win_ref, ki_ref, k_ref,
                        vt_ref, out_ref,
                        s2_ref, q8_ref, qi8_ref, bias_ref, pt_ref, acc_ref, m_ref, l_ref,
                        scale_ref, range_ref, *, tq, tk, k_sel):
    i = pl.program_id(1)
    i_next = jnp.minimum(i + 1, pl.num_programs(1) - 1)
    ratio = tk // tq
    nchunks = lax.div(i + ratio, ratio)
    nchunks_next = lax.div(i_next + ratio, ratio)
    kf = jnp.float32(k_sel)
    cur = lax.rem(i, 2)
    s_ref = s2_ref.at[cur]
    s_next_ref = s2_ref.at[1 - cur]

    lane = lax.broadcasted_iota(jnp.int32, (tq, LANES), 1)
    lo_half = lane < HEAD_DIM
    def stack_heads(src, dst):
        for j in range(N_HEADS // 2):
            blk = src[:, j * LANES:(j + 1) * LANES].astype(_F32)
            dst[(2 * j) * tq:(2 * j + 1) * tq, :] = jnp.where(lo_half, blk, 0.0).astype(dst.dtype)
            dst[(2 * j + 1) * tq:(2 * j + 2) * tq, :] = jnp.where(lo_half, 0.0, blk).astype(dst.dtype)

    stack_heads(qs_ref, q8_ref)

    row = lax.broadcasted_iota(jnp.int32, (tk, tq), 0)
    col = lax.broadcasted_iota(jnp.int32, (tk, tq), 1)
    key_minus_query = row - col
    nt_dims = (((1,), (1,)), ((), ()))
    fold = lambda x: x.reshape(tk // 32, 32, tq)

    def score_chunk(c, carry, blk, dst_ref, w_t):
        mn, mx = carry
        k0 = pl.multiple_of(c * tk, tk)
        rt = lax.dot_general(ki_ref[pl.ds(k0, tk), :], qi8_ref[...], nt_dims,
                             preferred_element_type=_F32)
        s = None
        for hb in range(N_IDX_HEADS):
            t = jnp.maximum(rt[:, hb * tq:(hb + 1) * tq], 0.0) * w_t[hb:hb + 1, :]
            s = t if s is None else s + t
        dst_ref[c] = jnp.where(key_minus_query <= blk * tq - c * tk, s, _NEG_INF)
        return jnp.minimum(mn, fold(s).min(axis=0)), jnp.maximum(mx, fold(s).max(axis=0))

    no_range = (jnp.full((32, tq), jnp.inf, _F32), jnp.full((32, tq), -jnp.inf, _F32))

    @pl.when(i == 0)
    def _():
        stack_heads(qi_ref, qi8_ref)
        w_t = wi_ref[...].T
        mn, mx = lax.fori_loop(0, nchunks,
                               lambda c, carry: score_chunk(c, carry, i, s_ref, w_t), no_range)
        range_ref[cur, 0] = mn
        range_ref[cur, 1] = mx

    smin = jnp.min(range_ref[cur, 0], axis=0, keepdims=True)
    smax = jnp.max(range_ref[cur, 1], axis=0, keepdims=True)

    def count_ge(t):
        def body(c, acc):
            return acc + fold(jnp.where(s_ref[c] >= t, 1.0, 0.0)).sum(axis=0)
        acc = lax.fori_loop(0, nchunks, body, jnp.zeros((32, tq), _F32))
        return jnp.sum(acc, axis=0, keepdims=True)

    def max_below(x):
        def body(c, acc):
            s = s_ref[c]
            return jnp.maximum(acc, fold(jnp.where(s < x, s, _NEG_INF)).max(axis=0))
        acc = lax.fori_loop(0, nchunks, body, jnp.full((32, tq), -jnp.inf, _F32))
        return jnp.max(acc, axis=0, keepdims=True)

    n_adm = (i * tq + 1 + lax.broadcasted_iota(jnp.int32, (1, tq), 1)).astype(_F32)
    thr, clo, chi, short = _topk_threshold(count_ge, max_below, smin, smax, n_adm, f0_ref[0],
                                           k_sel)

    need = jnp.logical_and(clo > kf, jnp.logical_not(short))

    @pl.when(jnp.max(jnp.where(need, 1.0, 0.0)) > 0.5)
    def _():
        keep = jnp.where(need, kf - chi, _FLT_MAX)
        rk = lax.broadcasted_iota(jnp.int32, (tk, tk), 0)
        ck = lax.broadcasted_iota(jnp.int32, (tk, tk), 1)
        earlier = jnp.where(rk > ck, 1.0, 0.0).astype(_MXU_DTYPE)

        def drop(c, seen):
            s = s_ref[c]
            tie = s == thr
            tm = jnp.where(tie, 1.0, 0.0)
            before = jnp.dot(earlier, tm.astype(_MXU_DTYPE), preferred_element_type=_F32) + seen
            s_ref[c] = jnp.where(jnp.where(tie, before, -1.0) >= keep, _NEG_INF, s)
            return seen + jnp.sum(fold(tm).sum(axis=0), axis=0, keepdims=True)

        lax.fori_loop(0, nchunks, drop, jnp.zeros((1, tq), _F32))

    def chunk_probs(c, running_max):
        bias_ref[...] = jnp.where(s_ref[c] >= thr, 0.0, _NEG_INF)
        k_chunk = k_ref[pl.ds(pl.multiple_of(c * tk, tk), tk), :]
        slot = lax.rem(c, 2)
        for j in range(N_HEADS // 2):
            l2 = lax.dot_general(k_chunk, q8_ref[(2 * j) * tq:(2 * j + 2) * tq, :], nt_dims,
                                 preferred_element_type=_F32)
            for half in range(2):
                hb = 2 * j + half
                lh = l2[:, half * tq:(half + 1) * tq] + bias_ref[...]
                m_old = m_ref[hb:hb + 1, :]
                if running_max:
                    m_new = jnp.maximum(m_old,
                                        jnp.max(fold(lh).max(axis=0), axis=0, keepdims=True))
                    alpha = jnp.exp(m_old - m_new)
                    m_ref[hb:hb + 1, :] = m_new
                    scale_ref[0:1, hb * tq:(hb + 1) * tq] = alpha
                    p = jnp.exp(lh - m_new)
                    l_ref[hb:hb + 1, :] = (alpha * l_ref[hb:hb + 1, :]
                                           + jnp.sum(fold(p).sum(axis=0), axis=0, keepdims=True))
                else:
                    p = jnp.exp(lh - m_old)
                    l_ref[hb:hb + 1, :] = (l_ref[hb:hb + 1, :]
                                           + jnp.sum(fold(p).sum(axis=0), axis=0, keepdims=True))
                pt_ref[slot, :, hb * tq:(hb + 1) * tq] = p.astype(pt_ref.dtype)

    def values(c):
        return jnp.dot(vt_ref[c], pt_ref[lax.rem(c, 2)], preferred_element_type=_F32)

    def reset():
        m_ref[...] = jnp.full(m_ref.shape, -_FLT_MAX, _F32)
        l_ref[...] = jnp.zeros(l_ref.shape, _F32)
        acc_ref[...] = jnp.zeros(acc_ref.shape, _F32)

    stack_heads(qin_ref, qi8_ref)
    w_t_next = win_ref[...].T
    score_next = lambda c, carry: score_chunk(c, carry, i_next, s_next_ref, w_t_next)

    reset()
    chunk_probs(0, running_max=True)
    next_range = score_next(0, no_range)

    def attend(c, next_range):
        acc_ref[...] = acc_ref[...] + values(c - 1)
        chunk_probs(c, running_max=False)
        return score_next(c, next_range)

    mn, mx = lax.fori_loop(1, nchunks, attend, next_range)
    acc_ref[...] = acc_ref[...] + values(nchunks - 1)
    range_ref[1 - cur, 0] = mn
    range_ref[1 - cur, 1] = mx

    @pl.when(nchunks_next > nchunks)
    def _():
        mn, mx = score_next(nchunks, (range_ref[1 - cur, 0], range_ref[1 - cur, 1]))
        range_ref[1 - cur, 0] = mn
        range_ref[1 - cur, 1] = mx

    sane = jnp.logical_and(jnp.max(l_ref[...]) < HUGE_SUM,
                           jnp.max(jnp.abs(acc_ref[...])) < HUGE_SUM)
    sane = jnp.logical_and(sane, jnp.min(m_ref[...]) > -_FLT_MAX)

    @pl.when(jnp.logical_not(sane))
    def _():
        reset()

        def attend_rescaling(c, carry):
            chunk_probs(c, running_max=True)
            acc_ref[...] = acc_ref[...] * scale_ref[0:1, :] + values(c)
            return carry

        lax.fori_loop(0, nchunks, attend_rescaling, 0)

    l_all = jnp.concatenate([l_ref[hb:hb + 1, :] for hb in range(N_HEADS)], axis=1)
    o_t = acc_ref[...] / l_all
    blocks = []
    for j in range(N_HEADS // 2):
        mj = jnp.concatenate(
            [o_t[:HEAD_DIM, (2 * j) * tq:(2 * j + 1) * tq],
             o_t[HEAD_DIM:, (2 * j + 1) * tq:(2 * j + 2) * tq]], axis=0)
        blocks.append(mj.T)
    yb = jnp.concatenate(blocks, axis=1)
    out_ref[...] = (yb * zbs_ref[...]).astype(out_ref.dtype)


def _attn_prompt(qs, qi, wi, zbs, kibf, kbf, vt3, *, nb, t, tq, tk):
    nq = t // tq
    k_sel = min(TOP_K_MAX, t // 4)
    f0 = _seed_fraction(jnp.arange(1, t + 1), k_sel).reshape(nq, 1, tq)
    blk = lambda w: pl.BlockSpec((tq, w), lambda b, i: (b * nq + i, 0))
    nxt = lambda w: pl.BlockSpec((tq, w), lambda b, i: (b * nq + jnp.minimum(i + 1, nq - 1), 0))
    res = lambda w: pl.BlockSpec((t, w), lambda b, i: (b, 0))
    stat = pltpu.VMEM((N_HEADS, tq), _F32)
    kernel = functools.partial(_attn_prompt_kernel, tq=tq, tk=tk, k_sel=k_sel)
    return pl.pallas_call(
        kernel,
        grid=(nb, nq),
        in_specs=[blk(D_B), blk(N_IDX_HEADS * D_IDX), blk(LANES), blk(D_B),
                  pl.BlockSpec((1, 1, tq), lambda b, i: (i, 0, 0)),
                  nxt(N_IDX_HEADS * D_IDX), nxt(LANES),
                  res(2 * D_IDX), res(KV_W),
                  pl.BlockSpec((t // tk, KV_W, tk), lambda b, i: (b, 0, 0))],
        out_specs=blk(D_B),
        out_shape=jax.ShapeDtypeStruct((nb * t, D_B), _MXU_DTYPE),
        scratch_shapes=[
            pltpu.VMEM((2, t // tk, tk, tq), _F32),
            pltpu.VMEM((N_HEADS * tq, LANES), _MXU_DTYPE),
            pltpu.VMEM((N_IDX_HEADS * tq, LANES), _MXU_DTYPE),
            pltpu.VMEM((tk, tq), _F32),
            pltpu.VMEM((2, tk, N_HEADS * tq), _MXU_DTYPE),
            pltpu.VMEM((KV_W, N_HEADS * tq), _F32),
            stat, stat,
            pltpu.VMEM((1, N_HEADS * tq), _F32),
            pltpu.VMEM((2, 2, 32, tq), _F32),
        ],
        compiler_params=pltpu.CompilerParams(
            dimension_semantics=("arbitrary", "arbitrary"), vmem_limit_bytes=VMEM_LIMIT_BYTES),
        name="attn_prompt",
    )(qs, qi, wi, zbs, f0, qi, wi, kibf, kbf, vt3)


def _stream_pages(pt_ref, b, nb, npages, sources, buffers, sems):
    slot = lax.rem(b, 2)

    def copies(bb, sl, p):
        phys = pt_ref[bb * npages + p]
        dst = pl.ds(pl.multiple_of(p * PAGE_SIZE, PAGE_SIZE), PAGE_SIZE)
        return [pltpu.make_async_copy(src.at[phys], buf.at[sl, :, dst], sems.at[i, sl])
                for i, (src, buf) in enumerate(zip(sources, buffers))]

    def start(bb, sl):
        def body(p, carry):
            for cp in copies(bb, sl, p):
                cp.start()
            return carry
        lax.fori_loop(0, npages, body, 0)

    def wait(bb, sl):
        def body(p, carry):
            for cp in copies(bb, sl, p):
                cp.wait()
            return carry
        lax.fori_loop(0, npages, body, 0)

    @pl.when(b == 0)
    def _():
        past = npages * PAGE_SIZE
        for buf in buffers:
            for sl in range(2):
                buf[sl, :, past:] = jnp.zeros((buf.shape[1], buf.shape[2] - past), buf.dtype)
        start(0, 0)

    @pl.when(b + 1 < nb)
    def _():
        start(b + 1, 1 - slot)

    wait(b, slot)
    return slot


def _put_new_keys(buf, slot, past, rows):
    tq, feat = rows.shape
    tile = jnp.concatenate([rows, jnp.zeros((PAGE_SIZE - tq, feat), _F32)], axis=0)
    if feat < LANES:
        tile = jnp.concatenate([tile, jnp.zeros((PAGE_SIZE, LANES - feat), _F32)], axis=1)
    buf[slot, :, past:past + PAGE_SIZE] = tile.T[:feat, :].astype(buf.dtype)


def _sample_scores_kernel(pt_ref, qi_ref, wi_ref, kin_ref, ci_hbm, s_ref, ibuf, sems,
                          *, npages, tk, nchunks):
    b = pl.program_id(0)
    past = npages * PAGE_SIZE
    tq = qi_ref.shape[1]
    slot = _stream_pages(pt_ref, b, pl.num_programs(0), npages, [ci_hbm], [ibuf], sems)
    _put_new_keys(ibuf, slot, past, kin_ref[0])

    qi = qi_ref[0].astype(_F32)
    wi = wi_ref[0]
    qi8 = jnp.concatenate([qi[:, h * D_IDX:(h + 1) * D_IDX] for h in range(N_IDX_HEADS)],
                          axis=0).astype(_MXU_DTYPE)
    w_wide = [jnp.broadcast_to(wi[:, h:h + 1], (tq, tk)) for h in range(N_IDX_HEADS)]
    key_minus_query = (lax.broadcasted_iota(jnp.int32, (tq, tk), 1)
                       - lax.broadcasted_iota(jnp.int32, (tq, tk), 0))
    for c in range(nchunks):
        keys_t = ibuf[slot, :, c * tk:(c + 1) * tk].astype(_MXU_DTYPE)
        r = jnp.dot(qi8, keys_t, preferred_element_type=_F32)
        s = None
        for h in range(N_IDX_HEADS):
            t = jnp.maximum(r[h * tq:(h + 1) * tq, :], 0.0) * w_wide[h]
            s = t if s is None else s + t
        s_ref[0, c] = jnp.where(key_minus_query <= past - c * tk, s, _NEG_INF)


def _sample_threshold_kernel(s_ref, f0_ref, nadm_ref, out_ref, thr_ref, *, tk, nchunks, k_sel):
    rows = s_ref.shape[2]
    kf = jnp.float32(k_sel)
    groups = tk // LANES
    tile = lambda col: jnp.broadcast_to(col, (rows, tk))

    def fold_with(x, op):
        out = x[:, :LANES]
        for g in range(1, groups):
            out = op(out, x[:, g * LANES:(g + 1) * LANES])
        return out

    def over_chunks(fn, op, init):
        def body(c, acc):
            return op(acc, fold_with(fn(s_ref[0, c]), op))
        return lax.fori_loop(0, nchunks, body, jnp.full((rows, LANES), init, _F32))

    def count_ge(t):
        t_wide = tile(t)
        acc = over_chunks(lambda s: jnp.where(s >= t_wide, 1.0, 0.0), jnp.add, 0.0)
        return jnp.sum(acc, axis=1, keepdims=True)

    def max_below(x):
        x_wide = tile(x)
        acc = over_chunks(lambda s: jnp.where(s < x_wide, s, _NEG_INF), jnp.maximum, -jnp.inf)
        return jnp.max(acc, axis=1, keepdims=True)

    finite_min = over_chunks(lambda s: jnp.where(s > _NEG_INF, s, jnp.inf), jnp.minimum, jnp.inf)
    finite_max = over_chunks(lambda s: s, jnp.maximum, -jnp.inf)
    smin = jnp.min(finite_min, axis=1, keepdims=True)
    smax = jnp.max(finite_max, axis=1, keepdims=True)
    thr, clo, chi, short = _topk_threshold(count_ge, max_below, smin, smax, nadm_ref[...],
                                           f0_ref[...], k_sel)
    thr_ref[...] = thr
    need = jnp.logical_and(clo > kf, jnp.logical_not(short))
    any_ties = jnp.max(jnp.where(need, 1.0, 0.0)) > 0.5

    @pl.when(jnp.logical_not(any_ties))
    def _():
        out_ref[...] = s_ref[...]

    @pl.when(any_ties)
    def _():
        keep_wide = tile(jnp.where(need, kf - chi, _FLT_MAX))
        thr_wide = tile(thr)
        rk = lax.broadcasted_iota(jnp.int32, (tk, tk), 0)
        ck = lax.broadcasted_iota(jnp.int32, (tk, tk), 1)
        earlier = jnp.where(rk < ck, 1.0, 0.0).astype(_MXU_DTYPE)

        def drop(c, seen):
            s = s_ref[0, c]
            tie = s == thr_wide
            tm = jnp.where(tie, 1.0, 0.0)
            before = (jnp.dot(tm.astype(_MXU_DTYPE), earlier, preferred_element_type=_F32)
                      + tile(seen))
            out_ref[0, c] = jnp.where(jnp.where(tie, before, -1.0) >= keep_wide, _NEG_INF, s)
            return seen + jnp.sum(tm, axis=1, keepdims=True)

        lax.fori_loop(0, nchunks, drop, jnp.zeros((rows, 1), _F32))


def _sample_attend_kernel(pt_ref, qs_ref, zbs_ref, kn_ref, vn_ref, s_ref, thr_ref,
                          ck_hbm, cv_hbm, out_ref, kbuf, vbuf, sems, l_ref,
                          *, npages, tk, nchunks):
    b = pl.program_id(0)
    past = npages * PAGE_SIZE
    tq = qs_ref.shape[1]
    slot = _stream_pages(pt_ref, b, pl.num_programs(0), npages, [ck_hbm, cv_hbm], [kbuf, vbuf],
                         sems)
    _put_new_keys(kbuf, slot, past, kn_ref[0])
    _put_new_keys(vbuf, slot, past, vn_ref[0])

    qs = qs_ref[0].astype(_F32)
    lane = lax.broadcasted_iota(jnp.int32, (tq, LANES), 1)
    lo_half = lane < HEAD_DIM
    q8_rows = []
    for j in range(N_HEADS // 2):
        blk = qs[:, j * LANES:(j + 1) * LANES]
        q8_rows.append(jnp.where(lo_half, blk, 0.0))
        q8_rows.append(jnp.where(lo_half, 0.0, blk))
    q8 = jnp.concatenate(q8_rows, axis=0).astype(_MXU_DTYPE)
    thr_wide = jnp.broadcast_to(thr_ref[...], (tq, tk))

    mx = jnp.full((N_HEADS * tq, tk), -jnp.inf, _F32)
    for c in range(nchunks):
        keys_t = kbuf[slot, :, c * tk:(c + 1) * tk].astype(_MXU_DTYPE)
        lg = jnp.dot(q8, keys_t, preferred_element_type=_F32)
        sel = s_ref[0, c] >= thr_wide
        lg = jnp.concatenate(
            [jnp.where(sel, lg[h * tq:(h + 1) * tq, :], _NEG_INF) for h in range(N_HEADS)], axis=0)
        l_ref[c] = lg
        mx = jnp.maximum(mx, lg)
    m_wide = jnp.broadcast_to(jnp.max(mx, axis=1, keepdims=True), (N_HEADS * tq, tk))

    nt_dims = (((1,), (1,)), ((), ()))
    psum = jnp.zeros((N_HEADS * tq, tk), _F32)
    acc = jnp.zeros((N_HEADS * tq, KV_W), _F32)
    for c in range(nchunks):
        p = jnp.exp(l_ref[c] - m_wide)
        psum = psum + p
        vals_t = vbuf[slot, :, c * tk:(c + 1) * tk].astype(_MXU_DTYPE)
        acc = acc + lax.dot_general(p.astype(_MXU_DTYPE), vals_t, nt_dims,
                                    preferred_element_type=_F32)
    o = acc / jnp.sum(psum, axis=1, keepdims=True)
    blocks = []
    for j in range(N_HEADS // 2):
        blocks.append(jnp.where(lo_half, o[(2 * j) * tq:(2 * j + 1) * tq, :],
                                o[(2 * j + 1) * tq:(2 * j + 2) * tq, :]))
    out_ref[0] = jnp.concatenate(blocks, axis=1) * zbs_ref[0]


def _attn_sample(page_table, qs, qi, wi, zbs, k_new, v_new, ki_new, cache_kt, cache_vt, cache_it,
                 *, tk):
    nb, npages = page_table.shape
    tq = qs.shape[0] // nb
    past = npages * PAGE_SIZE
    nchunks = -(-(past + PAGE_SIZE) // tk)
    keys = nchunks * tk
    k_sel = min(TOP_K_MAX, (past + tq) // 4)
    group = LANES // tq
    assert LANES % tq == 0 and nb % group == 0
    three = lambda a: a.reshape(nb, tq, a.shape[-1])
    blk = lambda w: pl.BlockSpec((1, tq, w), lambda b, pt: (b, 0, 0))
    hbm = pl.BlockSpec(memory_space=pl.ANY)
    params = pltpu.CompilerParams(dimension_semantics=("arbitrary",),
                                  vmem_limit_bytes=VMEM_LIMIT_BYTES)
    pt_flat = page_table.reshape(-1)
    s_shape = jax.ShapeDtypeStruct((nb // group, nchunks, LANES, tk), _F32)
    s_blk = pl.BlockSpec((1, nchunks, tq, tk), lambda b, pt: (b // group, 0, b % group, 0))

    scores = pl.pallas_call(
        functools.partial(_sample_scores_kernel, npages=npages, tk=tk, nchunks=nchunks),
        grid_spec=pltpu.PrefetchScalarGridSpec(
            num_scalar_prefetch=1, grid=(nb,),
            in_specs=[blk(N_IDX_HEADS * D_IDX), blk(LANES), blk(D_IDX), hbm],
            out_specs=s_blk,
            scratch_shapes=[pltpu.VMEM((2, D_IDX, keys), cache_it.dtype),
                            pltpu.SemaphoreType.DMA((1, 2))]),
        out_shape=s_shape, compiler_params=params, name="sample_scores",
    )(pt_flat, three(qi), three(wi), three(ki_new), cache_it)

    n_adm = past + 1 + (jnp.arange(LANES) % tq)
    whole = pl.BlockSpec((1, nchunks, LANES, tk), lambda g: (g, 0, 0, 0))
    col = pl.BlockSpec((LANES, 1), lambda g: (0, 0))
    kept, thr = pl.pallas_call(
        functools.partial(_sample_threshold_kernel, tk=tk, nchunks=nchunks, k_sel=k_sel),
        grid=(nb // group,),
        in_specs=[whole, col, col],
        out_specs=(whole, pl.BlockSpec((LANES, 1), lambda g: (g, 0))),
        out_shape=(s_shape, jax.ShapeDtypeStruct((nb * tq, 1), _F32)),
        compiler_params=params, name="sample_threshold",
    )(scores, _seed_fraction(n_adm, k_sel).reshape(LANES, 1),
      n_adm.astype(_F32).reshape(LANES, 1))

    out = pl.pallas_call(
        functools.partial(_sample_attend_kernel, npages=npages, tk=tk, nchunks=nchunks),
        grid_spec=pltpu.PrefetchScalarGridSpec(
            num_scalar_prefetch=1, grid=(nb,),
            in_specs=[blk(D_B), blk(D_B), blk(KV_W), blk(KV_W), s_blk,
                      pl.BlockSpec((tq, 1), lambda b, pt: (b, 0)), hbm, hbm],
            out_specs=blk(D_B),
            scratch_shapes=[pltpu.VMEM((2, KV_W, keys), cache_kt.dtype),
                            pltpu.VMEM((2, KV_W, keys), cache_vt.dtype),
                            pltpu.SemaphoreType.DMA((2, 2)),
                            pltpu.VMEM((nchunks, N_HEADS * tq, tk), _F32)]),
        out_shape=jax.ShapeDtypeStruct((nb, tq, D_B), _F32),
        compiler_params=params, name="sample_attend",
    )(pt_flat, three(qs), three(zbs), three(k_new), three(v_new), kept, thr, cache_kt, cache_vt)
    return out.reshape(nb * tq, D_B)


def _outproj_kernel(yag_ref, ybg_ref, x_ref, w_ref, g_ref, out_ref):
    y = (jnp.dot(yag_ref[...], w_ref[:D_A, :], preferred_element_type=_F32)
         + jnp.dot(ybg_ref[...].astype(_MXU_DTYPE), w_ref[D_A:, :], preferred_element_type=_F32))
    ms = jnp.mean(y * y, axis=-1, keepdims=True)
    out_ref[...] = x_ref[...] + y * lax.rsqrt(ms + EPS) * g_ref[...]


def _outproj(yag, ybg, x2d, w_out_p, g_post, *, tm):
    r = x2d.shape[0]
    row = lambda w: pl.BlockSpec((tm, w), lambda i: (i, 0))
    const = lambda shape: pl.BlockSpec(shape, lambda i: (0,) * len(shape))
    return pl.pallas_call(
        _outproj_kernel,
        grid=(r // tm,),
        in_specs=[row(D_A), row(D_B), row(D_MODEL), const((D_A + D_B, D_MODEL)),
                  const((1, D_MODEL))],
        out_specs=row(D_MODEL),
        out_shape=jax.ShapeDtypeStruct((r, D_MODEL), _F32),
        compiler_params=pltpu.CompilerParams(
            dimension_semantics=("arbitrary",), vmem_limit_bytes=VMEM_LIMIT_BYTES),
        name="outproj",
    )(yag, ybg, x2d, w_out_p, g_post)


def _pad_in_weights(w):
    perm = _head_perm()
    nat = np.cumsum([0, D_A, D_A, D_A, D_B, KV_W, KV_W, D_B, N_IDX_HEADS * D_IDX, D_IDX,
                     N_IDX_HEADS])
    seg = lambda n: w[:, nat[n]:nat[n + 1]]
    pad = jnp.zeros((w.shape[0], C_END - C_WI - N_IDX_HEADS), w.dtype)
    return jnp.concatenate(
        [seg(0), seg(1), seg(2), seg(3)[:, perm], seg(4), seg(5), seg(6)[:, perm], seg(7),
         seg(8), seg(8), seg(9), pad], axis=1).astype(_MXU_DTYPE)


def _layer(x_prompt, x_sample, cache_k, cache_v, cache_idx_k, page_table, g_pre, w_in, ln_v_g,
           ln_v_b, w_s, b_s, w_out, g_post):
    nb, t, _ = x_prompt.shape
    nd, td, _ = x_sample.shape
    npages = page_table.shape[1]
    past = npages * PAGE_SIZE
    tm = 1024
    tq = 128
    tk = 1024
    tk_sample = 512

    w_pad = _pad_in_weights(w_in)
    perm = _head_perm()
    w_out_p = jnp.concatenate([w_out[:D_A], w_out[D_A:][perm]], axis=0).astype(_MXU_DTYPE)
    gpre = g_pre.reshape(1, D_MODEL)
    gpost = g_post.reshape(1, D_MODEL)
    lng = ln_v_g.reshape(1, D_A)
    lnb = ln_v_b.reshape(1, D_A)
    tril = jnp.tril(jnp.ones((CHUNK, CHUNK), dtype=bool))
    ws_tril = jnp.where(tril[None], w_s, jnp.zeros_like(w_s))

    xp = x_prompt.reshape(nb * t, D_MODEL)
    tpb = t // tm
    wmix_p = ws_tril.astype(_MXU_DTYPE)
    bmix_p = jnp.repeat(b_s.T, A_GROUP_DIM, axis=1)
    outs = _inproj(xp, _rope_tables(jnp.arange(t, dtype=jnp.int32)), lambda i: (i % tpb, 0),
                   gpre, w_pad, lng, lnb, wmix_p, bmix_p, tm=tm, state_rows=CHUNK,
                   state_shape=(nb, CHUNK, D_A), state_index=lambda i: (i // tpb, 0, 0), seqs=nb)
    yag, zbs, qs, qi, k_t, v_t, ki_t, wi, kbf, vt3, kibf, vstate_p = outs
    ybg = _attn_prompt(qs, qi, wi, zbs, kibf, kbf, vt3, nb=nb, t=t, tq=tq, tk=tk)
    y_prompt = _outproj(yag, ybg, xp, w_out_p, gpost, tm=tm).reshape(nb, t, D_MODEL)

    xs = x_sample.reshape(nd * td, D_MODEL)
    reps = CHUNK // td
    eye = jnp.eye(reps, dtype=w_s.dtype)
    wmix_s = jnp.stack([jnp.kron(eye, ws_tril[g, :td, :td]) for g in range(N_A_GROUPS)]
                       ).astype(_MXU_DTYPE)
    bmix_s = jnp.repeat(jnp.tile(b_s[:, :td], (1, reps)).T, A_GROUP_DIM, axis=1)
    tm_s = min(tm, nd * td)
    pos_s = past + (jnp.arange(tm_s, dtype=jnp.int32) % td)
    outs = _inproj(xs, _rope_tables(pos_s), lambda i: (0, 0), gpre, w_pad, lng, lnb, wmix_s,
                   bmix_s, tm=tm_s, state_rows=tm_s, state_shape=(nd * td, D_A),
                   state_index=lambda i: (i, 0))
    yag_s, zbs_s, qs_s, qi_s, k32_s, v32_s, ki32_s, wi_s, _, _, _, vstate_s = outs
    pool = cache_k.shape[0]
    ybg_s = _attn_sample(page_table, qs_s, qi_s, wi_s, zbs_s, k32_s, v32_s, ki32_s,
                         jnp.transpose(cache_k, (0, 2, 3, 1)).reshape(pool, KV_W, PAGE_SIZE),
                         jnp.transpose(cache_v, (0, 2, 3, 1)).reshape(pool, KV_W, PAGE_SIZE),
                         jnp.transpose(cache_idx_k, (0, 2, 1)), tk=tk_sample)
    y_sample = _outproj(yag_s, ybg_s, xs, w_out_p, gpost, tm=tm_s).reshape(nd, td, D_MODEL)

    return (y_prompt, y_sample,
            jnp.transpose(k_t.reshape(nb, N_KV_HEADS, HEAD_DIM, t), (0, 3, 1, 2)),
            jnp.transpose(v_t.reshape(nb, N_KV_HEADS, HEAD_DIM, t), (0, 3, 1, 2)),
            jnp.transpose(ki_t, (0, 2, 1)), vstate_p,
            k32_s.reshape(nd, td, N_KV_HEADS, HEAD_DIM), v32_s.reshape(nd, td, N_KV_HEADS, HEAD_DIM),
            ki32_s.reshape(nd, td, D_IDX), vstate_s.reshape(nd, td, D_A))


def kernel(x_prompt, x_sample, cache_k, cache_v, cache_idx_k, page_table, g_pre, w_in, ln_v_g,
           ln_v_b, w_s, b_s, w_out, g_post):
    xp, xs = x_prompt, x_sample
    per_layer = []
    for l in range(g_pre.shape[0]):
        outs = _layer(xp, xs, cache_k[l], cache_v[l], cache_idx_k[l], page_table, g_pre[l],
                      w_in[l], ln_v_g[l], ln_v_b[l], w_s[l], b_s[l], w_out[l], g_post[l])
        xp, xs = outs[0], outs[1]
        per_layer.append(outs[2:])
    stacked = tuple(jnp.stack(leaves, 0) for leaves in zip(*per_layer))
    return (xp, xs) + stacked
```

```python
import functools

import numpy as np
import jax
import jax.numpy as jnp
from jax import lax
from jax.scipy.special import ndtri
from jax.experimental import pallas as pl
from jax.experimental.pallas import tpu as pltpu

_F32 = jnp.float32
_MXU_DTYPE = jnp.bfloat16

D_MODEL = 1024
D_A = 512
D_B = 512
CHUNK = 128
N_A_GROUPS = 4
A_GROUP_DIM = D_A // N_A_GROUPS
HEAD_DIM = 64
N_HEADS = 8
N_KV_HEADS = 2
KV_W = N_KV_HEADS * HEAD_DIM
ROT_DIM = HEAD_DIM // 4
N_IDX_HEADS = 8
D_IDX = 64
TOP_K_MAX = 256
ROPE_THETA = 500000.0
EPS = 1e-6
PAGE_SIZE = 128

LANES = 128
VMEM_LIMIT_BYTES = 56 * 1024 * 1024

C_U, C_V, C_ZA, C_Q, C_K, C_VV, C_ZB, C_QI, C_KI, C_WI, C_END = (
    0, 512, 1024, 1536, 2048, 2176, 2304, 2816, 3328, 3456, 3584)

_FLT_MAX = float(np.finfo(np.float32).max)
_NEG_INF = float("-inf")
_INT_MIN = -2 ** 31


def _head_perm():
    c = np.arange(N_HEADS * HEAD_DIM)
    j = c // LANES
    half = (c % LANES) // HEAD_DIM
    d = c % HEAD_DIM
    return (j + 4 * half) * HEAD_DIM + d


def _rope_tables(pos):
    half = ROT_DIM // 2
    inv = jnp.power(jnp.float32(ROPE_THETA), -jnp.arange(half, dtype=_F32) * 2.0 / ROT_DIM)
    ang = pos.astype(_F32)[:, None] * inv[None, :]
    cos = jnp.cos(ang)
    sin = jnp.sin(ang)
    l64 = np.arange(LANES) % HEAD_DIM
    fidx = l64 % half
    in_rot = (l64 < ROT_DIM)[None, :]
    first = (l64 < half)[None, :]
    second = ((l64 >= half) & (l64 < ROT_DIM))[None, :]
    cos_t = jnp.where(in_rot, cos[:, fidx], 1.0)
    sin_a = jnp.where(first, -sin[:, fidx], 0.0)
    sin_b = jnp.where(second, sin[:, fidx], 0.0)
    return cos_t, sin_a, sin_b


def _silu(z):
    return z * (1.0 / (1.0 + jnp.exp(-z)))


def _inproj_kernel(x_ref, cos_ref, sa_ref, sb_ref, gpre_ref, win_ref, lng_ref, lnb_ref,
                   wmix_ref, bmix_ref,
                   yag_ref, zbs_ref, qs_ref, qi_ref, k32_ref, v32_ref, ki32_ref, wi_ref,
                   kbf_ref, vt_ref, kibf_ref, vstate_ref, *, tm, state_rows, kv_transposed):
    x = x_ref[...]
    ms = jnp.mean(x * x, axis=-1, keepdims=True)
    h = (x * lax.rsqrt(ms + EPS) * gpre_ref[...]).astype(_MXU_DTYPE)

    def proj(a, b):
        return jnp.dot(h, win_ref[:, a:b], preferred_element_type=_F32)

    cos_t = cos_ref[...]
    sin_a = sa_ref[...]
    sin_b = sb_ref[...]

    def rope(p):
        blocks = []
        for j in range(p.shape[1] // LANES):
            pj = p[:, j * LANES:(j + 1) * LANES]
            blocks.append(pj * cos_t
                          + pltpu.roll(pj, LANES - ROT_DIM // 2, 1) * sin_a
                          + pltpu.roll(pj, ROT_DIM // 2, 1) * sin_b)
        return blocks[0] if len(blocks) == 1 else jnp.concatenate(blocks, axis=1)

    u = proj(C_U, C_V)
    v = proj(C_V, C_ZA)
    mu = jnp.mean(v, axis=-1, keepdims=True)
    vc = v - mu
    var = jnp.mean(vc * vc, axis=-1, keepdims=True)
    vn = vc * lax.rsqrt(var + EPS) * lng_ref[...] + lnb_ref[...]
    vstate_ref[...] = vn[tm - state_rows:, :].reshape(vstate_ref.shape)
    vnb = vn.astype(_MXU_DTYPE)
    bmix = bmix_ref[...]
    rows = []
    for c in range(tm // CHUNK):
        cols = []
        for g in range(N_A_GROUPS):
            blk = vnb[c * CHUNK:(c + 1) * CHUNK, g * A_GROUP_DIM:(g + 1) * A_GROUP_DIM]
            cols.append(jnp.dot(wmix_ref[g], blk, preferred_element_type=_F32)
                        + bmix[:, g * A_GROUP_DIM:(g + 1) * A_GROUP_DIM])
        rows.append(jnp.concatenate(cols, axis=1))
    mixed = jnp.concatenate(rows, axis=0)
    za = proj(C_ZA, C_Q)
    yag_ref[...] = ((u * mixed) * _silu(za)).astype(yag_ref.dtype)

    q = rope(proj(C_Q, C_K))
    qs_ref[...] = (q * (HEAD_DIM ** -0.5)).astype(qs_ref.dtype)
    k = rope(proj(C_K, C_VV))
    kbf_ref[...] = k.astype(kbf_ref.dtype)
    vv = proj(C_VV, C_ZB)
    vv_t = vv.T
    vt_ref[0] = vv_t.astype(vt_ref.dtype)
    if kv_transposed:
        k32_ref[0] = k.T
        v32_ref[0] = vv_t
    else:
        k32_ref[...] = k
        v32_ref[...] = vv
    zb = proj(C_ZB, C_QI)
    zbs_ref[...] = _silu(zb)
    qi = rope(proj(C_QI, C_KI))
    qi_ref[...] = qi.astype(qi_ref.dtype)
    ki = rope(proj(C_KI, C_WI))
    if kv_transposed:
        ki32_ref[0] = ki.T[:D_IDX, :]
    else:
        ki32_ref[...] = ki[:, :D_IDX]
    kibf_ref[...] = ki.astype(kibf_ref.dtype)
    wi_ref[...] = proj(C_WI, C_END) * (N_IDX_HEADS ** -0.5)


def _inproj(x2d, tables, table_index, gpre, w_pad, lng, lnb, wmix, bmix, *, tm, state_rows,
            state_shape, state_index, seqs=None):
    r = x2d.shape[0]
    nt = r // tm
    if seqs is None:
        kv = lambda w: (jax.ShapeDtypeStruct((r, w), _F32), pl.BlockSpec((tm, w), lambda i: (i, 0)))
    else:
        tps = nt // seqs
        kv = lambda w: (jax.ShapeDtypeStruct((seqs, w, r // seqs), _F32),
                        pl.BlockSpec((1, w, tm), lambda i: (i // tps, 0, i % tps)))
    cos_t, sin_a, sin_b = tables
    row = lambda w: pl.BlockSpec((tm, w), lambda i: (i, 0))
    const = lambda shape: pl.BlockSpec(shape, lambda i: (0,) * len(shape))
    tab = pl.BlockSpec((tm, LANES), table_index)
    out_shapes = (
        jax.ShapeDtypeStruct((r, D_A), _MXU_DTYPE),
        jax.ShapeDtypeStruct((r, D_B), _F32),
        jax.ShapeDtypeStruct((r, D_B), _MXU_DTYPE),
        jax.ShapeDtypeStruct((r, N_IDX_HEADS * D_IDX), _MXU_DTYPE),
        kv(KV_W)[0],
        kv(KV_W)[0],
        kv(D_IDX)[0],
        jax.ShapeDtypeStruct((r, LANES), _F32),
        jax.ShapeDtypeStruct((r, KV_W), _MXU_DTYPE),
        jax.ShapeDtypeStruct((nt, KV_W, tm), _MXU_DTYPE),
        jax.ShapeDtypeStruct((r, 2 * D_IDX), _MXU_DTYPE),
        jax.ShapeDtypeStruct(state_shape, _F32),
    )
    out_specs = (
        row(D_A), row(D_B), row(D_B), row(N_IDX_HEADS * D_IDX), kv(KV_W)[1], kv(KV_W)[1],
        kv(D_IDX)[1], row(LANES), row(KV_W),
        pl.BlockSpec((1, KV_W, tm), lambda i: (i, 0, 0)),
        row(2 * D_IDX),
        pl.BlockSpec((1,) * (len(state_shape) - 2) + (state_rows, D_A), state_index),
    )
    return pl.pallas_call(
        functools.partial(_inproj_kernel, tm=tm, state_rows=state_rows,
                          kv_transposed=seqs is not None),
        grid=(nt,),
        in_specs=[row(D_MODEL), tab, tab, tab, const((1, D_MODEL)), const((D_MODEL, C_END)),
                  const((1, D_A)), const((1, D_A)), const((N_A_GROUPS, CHUNK, CHUNK)),
                  const((CHUNK, D_A))],
        out_specs=out_specs,
        out_shape=out_shapes,
        compiler_params=pltpu.CompilerParams(
            dimension_semantics=("arbitrary",), vmem_limit_bytes=VMEM_LIMIT_BYTES),
        name="inproj",
    )(x2d, cos_t, sin_a, sin_b, gpre, w_pad, lng, lnb, wmix, bmix)


def _f32_to_key(x):
    b = lax.bitcast_convert_type(x, jnp.int32)
    return jnp.where(b >= 0, b, b ^ jnp.int32(0x7FFFFFFF))


def _key_to_f32(k):
    return lax.bitcast_convert_type(jnp.where(k >= 0, k, k ^ jnp.int32(0x7FFFFFFF)), _F32)


N_INTERP_PASSES = 10
HUGE_SUM = 1e30
PAGE_LOOP_UNROLL = 8


def _topk_threshold(count_ge, max_below, smin, smax, n_adm, f0, k_sel):
    kf = jnp.float32(k_sel)
    log_k = jnp.log(kf - 0.5)
    lo0 = _f32_to_key(smin)
    hi0 = _f32_to_key(smax) + 1
    short = n_adm <= kf
    done0 = jnp.where(jnp.logical_or(short, hi0 <= lo0 + 1), 1.0, 0.0)
    g_hi0 = jnp.full(smin.shape, np.log(0.5), _F32) - log_k

    def all_done(done):
        return jnp.min(done) > 0.5

    def probe(carry, frac):
        it, lo, hi, clo, chi, glo, ghi, last, done = carry
        xlo = _key_to_f32(lo)
        xhi = _key_to_f32(hi)
        cand = jnp.clip(_f32_to_key(xlo + (xhi - xlo) * frac), lo + 1, hi - 1)
        cnt = count_ge(_key_to_f32(cand))
        g = jnp.log(jnp.maximum(cnt, 0.5)) - log_k
        active = done < 0.5
        new_lo = jnp.logical_and(active, cnt >= kf)
        new_hi = jnp.logical_and(active, cnt < kf)
        ghi = jnp.where(jnp.logical_and(new_lo, last == 1), ghi * 0.5, ghi)
        glo = jnp.where(jnp.logical_and(new_hi, last == 2), glo * 0.5, glo)
        lo = jnp.where(new_lo, cand, lo)
        clo = jnp.where(new_lo, cnt, clo)
        glo = jnp.where(new_lo, g, glo)
        hi = jnp.where(new_hi, cand, hi)
        chi = jnp.where(new_hi, cnt, chi)
        ghi = jnp.where(new_hi, g, ghi)
        last = jnp.where(new_lo, 1, jnp.where(new_hi, 2, last))
        done = jnp.where(jnp.logical_or(clo == kf, hi <= lo + 1), 1.0, done)
        return it + 1, lo, hi, clo, chi, glo, ghi, last, done

    def interp_body(_, carry):
        return probe(carry, carry[5] / (carry[5] - carry[6]))

    init = (jnp.int32(0), lo0, hi0, n_adm, jnp.zeros_like(n_adm),
            jnp.log(jnp.maximum(n_adm, 1.0)) - log_k, g_hi0, jnp.zeros(smin.shape, jnp.int32),
            done0)
    state = lax.fori_loop(1, N_INTERP_PASSES, interp_body, probe(init, f0))
    _, lo, hi, clo, chi, _, _, _, done = state

    def peel_cond(carry):
        return jnp.logical_not(all_done(carry[-1]))

    def peel_body(carry):
        lo, hi, clo, chi, done = carry
        top = max_below(_key_to_f32(hi))
        cnt = count_ge(top)
        active = done < 0.5
        hit = jnp.logical_and(active, cnt >= kf)
        miss = jnp.logical_and(active, cnt < kf)
        lo = jnp.where(hit, _f32_to_key(top), lo)
        clo = jnp.where(hit, cnt, clo)
        hi = jnp.where(miss, _f32_to_key(top), hi)
        chi = jnp.where(miss, cnt, chi)
        done = jnp.where(hit, 1.0, done)
        return lo, hi, clo, chi, done

    lo, _, clo, chi, _ = lax.while_loop(peel_cond, peel_body, (lo, hi, clo, chi, done))
    thr = jnp.where(short, -_FLT_MAX, _key_to_f32(lo))
    return thr, clo, chi, short


def _seed_fraction(n_adm, k_sel):
    n = n_adm.astype(_F32)
    z_k = ndtri(1.0 - jnp.minimum(k_sel / n, 0.999))
    z_n = ndtri(1.0 - 1.0 / (n + 1.0))
    return (0.5 + 0.5 * z_k / z_n).astype(_F32)


def _attn_prompt_kernel(qs_ref, qi_ref, wi_ref, zbs_ref, f0_ref, qin_ref, win_ref, ki_ref, k_ref,
                        vt_ref, out_ref,
                        s2_ref, q8_ref, qi8_ref, bias_ref, pt_ref, acc_ref, m_ref, l_ref,
                        scale_ref, range_ref, ties_ref, *, tq, tk, k_sel):
    i = pl.program_id(1)
    i_next = jnp.minimum(i + 1, pl.num_programs(1) - 1)
    ratio = tk // tq
    nchunks = lax.div(i + ratio, ratio)
    nchunks_next = lax.div(i_next + ratio, ratio)
    kf = jnp.float32(k_sel)
    cur = lax.rem(i, 2)
    s_ref = s2_ref.at[cur]
    s_next_ref = s2_ref.at[1 - cur]

    lane = lax.broadcasted_iota(jnp.int32, (tq, LANES), 1)
    lo_half = lane < HEAD_DIM
    def stack_heads(src, dst):
        for j in range(N_HEADS // 2):
            blk = src[:, j * LANES:(j + 1) * LANES].astype(_F32)
            dst[(2 * j) * tq:(2 * j + 1) * tq, :] = jnp.where(lo_half, blk, 0.0).astype(dst.dtype)
            dst[(2 * j + 1) * tq:(2 * j + 2) * tq, :] = jnp.where(lo_half, 0.0, blk).astype(dst.dtype)

    stack_heads(qs_ref, q8_ref)

    row = lax.broadcasted_iota(jnp.int32, (tk, tq), 0)
    col = lax.broadcasted_iota(jnp.int32, (tk, tq), 1)
    key_minus_query = row - col
    nt_dims = (((1,), (1,)), ((), ()))
    fold = lambda x: x.reshape(tk // 32, 32, tq)

    def score_chunk(c, carry, blk, dst_ref, w_t):
        mn, mx = carry
        k0 = pl.multiple_of(c * tk, tk)
        rt = lax.dot_general(ki_ref[pl.ds(k0, tk), :], qi8_ref[...], nt_dims,
                             preferred_element_type=_F32)
        s = None
        for hb in range(N_IDX_HEADS):
            t = jnp.maximum(rt[:, hb * tq:(hb + 1) * tq], 0.0) * w_t[hb:hb + 1, :]
            s = t if s is None else s + t
        dst_ref[c] = jnp.where(key_minus_query <= blk * tq - c * tk, s, _NEG_INF)
        return jnp.minimum(mn, fold(s).min(axis=0)), jnp.maximum(mx, fold(s).max(axis=0))

    no_range = (jnp.full((32, tq), jnp.inf, _F32), jnp.full((32, tq), -jnp.inf, _F32))

    @pl.when(i == 0)
    def _():
        stack_heads(qi_ref, qi8_ref)
        w_t = wi_ref[...].T
        mn, mx = lax.fori_loop(0, nchunks,
                               lambda c, carry: score_chunk(c, carry, i, s_ref, w_t), no_range)
        range_ref[cur, 0] = mn
        range_ref[cur, 1] = mx

    smin = jnp.min(range_ref[cur, 0], axis=0, keepdims=True)
    smax = jnp.max(range_ref[cur, 1], axis=0, keepdims=True)

    def count_ge(t):
        def body(c, acc):
            return acc + fold(jnp.where(s_ref[c] >= t, 1.0, 0.0)).sum(axis=0)
        acc = lax.fori_loop(0, nchunks, body, jnp.zeros((32, tq), _F32))
        return jnp.sum(acc, axis=0, keepdims=True)

    def max_below(x):
        def body(c, acc):
            s = s_ref[c]
            return jnp.maximum(acc, fold(jnp.where(s < x, s, _NEG_INF)).max(axis=0))
        acc = lax.fori_loop(0, nchunks, body, jnp.full((32, tq), -jnp.inf, _F32))
        return jnp.max(acc, axis=0, keepdims=True)

    n_adm = (i * tq + 1 + lax.broadcasted_iota(jnp.int32, (1, tq), 1)).astype(_F32)
    thr, clo, chi, short = _topk_threshold(count_ge, max_below, smin, smax, n_adm, f0_ref[0],
                                           k_sel)

    need = jnp.logical_and(clo > kf, jnp.logical_not(short))

    @pl.when(jnp.max(jnp.where(need, 1.0, 0.0)) > 0.5)
    def _():
        keep = jnp.where(need, kf - chi, _FLT_MAX)
        tied = lambda s: jnp.where(s == thr, 1.0, 0.0)
        col_sum = lambda x: jnp.sum(fold(x).sum(axis=0), axis=0, keepdims=True)

        def count_ties(c, carry):
            ties_ref[c] = jnp.broadcast_to(col_sum(tied(s_ref[c])), ties_ref.shape[1:])
            return carry

        lax.fori_loop(0, nchunks, count_ties, 0)

        def drop(c, seen):
            here = ties_ref[c][0:1, :]
            s = s_ref[c]
            s = jnp.where(jnp.where(s == thr, seen - keep, -1.0) >= 0.0, _NEG_INF, s)
            s_ref[c] = s
            runs_out = jnp.logical_and(seen < keep, seen + here > keep)

            @pl.when(jnp.max(jnp.where(runs_out, 1.0, 0.0)) > 0.5)
            def _():
                rk = lax.broadcasted_iota(jnp.int32, (tk, tk), 0)
                ck = lax.broadcasted_iota(jnp.int32, (tk, tk), 1)
                earlier = jnp.where(rk > ck, 1.0, 0.0).astype(_MXU_DTYPE)
                tm = tied(s)
                before = jnp.dot(earlier, tm.astype(_MXU_DTYPE),
                                 preferred_element_type=_F32) + seen
                s_ref[c] = jnp.where(jnp.where(tm > 0.5, before, -1.0) >= keep, _NEG_INF, s)

            return seen + here

        lax.fori_loop(0, nchunks, drop, jnp.zeros((1, tq), _F32))

    def chunk_probs(c, running_max):
        bias_ref[...] = jnp.where(s_ref[c] >= thr, 0.0, _NEG_INF)
        k_chunk = k_ref[pl.ds(pl.multiple_of(c * tk, tk), tk), :]
        slot = lax.rem(c, 2)
        for j in range(N_HEADS // 2):
            l2 = lax.dot_general(k_chunk, q8_ref[(2 * j) * tq:(2 * j + 2) * tq, :], nt_dims,
                                 preferred_element_type=_F32)
            for half in range(2):
                hb = 2 * j + half
                lh = l2[:, half * tq:(half + 1) * tq] + bias_ref[...]
                m_old = m_ref[hb:hb + 1, :]
                if running_max:
                    m_new = jnp.maximum(m_old,
                                        jnp.max(fold(lh).max(axis=0), axis=0, keepdims=True))
                    alpha = jnp.exp(m_old - m_new)
                    m_ref[hb:hb + 1, :] = m_new
                    scale_ref[0:1, hb * tq:(hb + 1) * tq] = alpha
                    p = jnp.exp(lh - m_new)
                    l_ref[hb:hb + 1, :] = (alpha * l_ref[hb:hb + 1, :]
                                           + jnp.sum(fold(p).sum(axis=0), axis=0, keepdims=True))
                else:
                    p = jnp.exp(lh - m_old)
                    l_ref[hb:hb + 1, :] = (l_ref[hb:hb + 1, :]
                                           + jnp.sum(fold(p).sum(axis=0), axis=0, keepdims=True))
                pt_ref[slot, :, hb * tq:(hb + 1) * tq] = p.astype(pt_ref.dtype)

    def values(c):
        return jnp.dot(vt_ref[c], pt_ref[lax.rem(c, 2)], preferred_element_type=_F32)

    def reset():
        m_ref[...] = jnp.full(m_ref.shape, -_FLT_MAX, _F32)
        l_ref[...] = jnp.zeros(l_ref.shape, _F32)
        acc_ref[...] = jnp.zeros(acc_ref.shape, _F32)

    stack_heads(qin_ref, qi8_ref)
    w_t_next = win_ref[...].T
    score_next = lambda c, carry: score_chunk(c, carry, i_next, s_next_ref, w_t_next)

    reset()
    chunk_probs(0, running_max=True)
    next_range = score_next(0, no_range)

    def attend(c, next_range):
        acc_ref[...] = acc_ref[...] + values(c - 1)
        chunk_probs(c, running_max=False)
        return score_next(c, next_range)

    mn, mx = lax.fori_loop(1, nchunks, attend, next_range)
    acc_ref[...] = acc_ref[...] + values(nchunks - 1)
    range_ref[1 - cur, 0] = mn
    range_ref[1 - cur, 1] = mx

    @pl.when(nchunks_next > nchunks)
    def _():
        mn, mx = score_next(nchunks, (range_ref[1 - cur, 0], range_ref[1 - cur, 1]))
        range_ref[1 - cur, 0] = mn
        range_ref[1 - cur, 1] = mx

    sane = jnp.logical_and(jnp.max(l_ref[...]) < HUGE_SUM,
                           jnp.max(jnp.abs(acc_ref[...])) < HUGE_SUM)
    sane = jnp.logical_and(sane, jnp.min(m_ref[...]) > -_FLT_MAX)

    @pl.when(jnp.logical_not(sane))
    def _():
        reset()

        def attend_rescaling(c, carry):
            chunk_probs(c, running_max=True)
            acc_ref[...] = acc_ref[...] * scale_ref[0:1, :] + values(c)
            return carry

        lax.fori_loop(0, nchunks, attend_rescaling, 0)

    l_all = jnp.concatenate([l_ref[hb:hb + 1, :] for hb in range(N_HEADS)], axis=1)
    o_t = acc_ref[...] / l_all
    blocks = []
    for j in range(N_HEADS // 2):
        mj = jnp.concatenate(
            [o_t[:HEAD_DIM, (2 * j) * tq:(2 * j + 1) * tq],
             o_t[HEAD_DIM:, (2 * j + 1) * tq:(2 * j + 2) * tq]], axis=0)
        blocks.append(mj.T)
    yb = jnp.concatenate(blocks, axis=1)
    out_ref[...] = (yb * zbs_ref[...]).astype(out_ref.dtype)


def _attn_prompt(qs, qi, wi, zbs, kibf, kbf, vt3, *, nb, t, tq, tk):
    nq = t // tq
    k_sel = min(TOP_K_MAX, t // 4)
    f0 = _seed_fraction(jnp.arange(1, t + 1), k_sel).reshape(nq, 1, tq)
    blk = lambda w: pl.BlockSpec((tq, w), lambda b, i: (b * nq + i, 0))
    nxt = lambda w: pl.BlockSpec((tq, w), lambda b, i: (b * nq + jnp.minimum(i + 1, nq - 1), 0))
    res = lambda w: pl.BlockSpec((t, w), lambda b, i: (b, 0))
    stat = pltpu.VMEM((N_HEADS, tq), _F32)
    kernel = functools.partial(_attn_prompt_kernel, tq=tq, tk=tk, k_sel=k_sel)
    return pl.pallas_call(
        kernel,
        grid=(nb, nq),
        in_specs=[blk(D_B), blk(N_IDX_HEADS * D_IDX), blk(LANES), blk(D_B),
                  pl.BlockSpec((1, 1, tq), lambda b, i: (i, 0, 0)),
                  nxt(N_IDX_HEADS * D_IDX), nxt(LANES),
                  res(2 * D_IDX), res(KV_W),
                  pl.BlockSpec((t // tk, KV_W, tk), lambda b, i: (b, 0, 0))],
        out_specs=blk(D_B),
        out_shape=jax.ShapeDtypeStruct((nb * t, D_B), _MXU_DTYPE),
        scratch_shapes=[
            pltpu.VMEM((2, t // tk, tk, tq), _F32),
            pltpu.VMEM((N_HEADS * tq, LANES), _MXU_DTYPE),
            pltpu.VMEM((N_IDX_HEADS * tq, LANES), _MXU_DTYPE),
            pltpu.VMEM((tk, tq), _F32),
            pltpu.VMEM((2, tk, N_HEADS * tq), _MXU_DTYPE),
            pltpu.VMEM((KV_W, N_HEADS * tq), _F32),
            stat, stat,
            pltpu.VMEM((1, N_HEADS * tq), _F32),
            pltpu.VMEM((2, 2, 32, tq), _F32),
            pltpu.VMEM((t // tk, 8, tq), _F32),
        ],
        compiler_params=pltpu.CompilerParams(
            dimension_semantics=("arbitrary", "arbitrary"), vmem_limit_bytes=VMEM_LIMIT_BYTES),
        name="attn_prompt",
    )(qs, qi, wi, zbs, f0, qi, wi, kibf, kbf, vt3)


def _stream_pages(pt_ref, b, nb, npages, sources, buffers, sems):
    slot = lax.rem(b, 2)

    def copies(bb, sl, p):
        phys = pt_ref[bb * npages + p]
        dst = pl.ds(pl.multiple_of(p * PAGE_SIZE, PAGE_SIZE), PAGE_SIZE)
        return [pltpu.make_async_copy(src.at[phys], buf.at[sl, :, dst], sems.at[i, sl])
                for i, (src, buf) in enumerate(zip(sources, buffers))]

    def start(bb, sl):
        def body(p, carry):
            for cp in copies(bb, sl, p):
                cp.start()
            return carry
        lax.fori_loop(0, npages, body, 0, unroll=PAGE_LOOP_UNROLL)

    def wait(bb, sl):
        def body(p, carry):
            for cp in copies(bb, sl, p):
                cp.wait()
            return carry
        lax.fori_loop(0, npages, body, 0, unroll=PAGE_LOOP_UNROLL)

    @pl.when(b == 0)
    def _():
        past = npages * PAGE_SIZE
        for buf in buffers:
            for sl in range(2):
                buf[sl, :, past:] = jnp.zeros((buf.shape[1], buf.shape[2] - past), buf.dtype)
        start(0, 0)

    @pl.when(b + 1 < nb)
    def _():
        start(b + 1, 1 - slot)

    wait(b, slot)
    return slot


def _put_new_keys(buf, slot, past, rows):
    tq, feat = rows.shape
    tile = jnp.concatenate([rows, jnp.zeros((PAGE_SIZE - tq, feat), _F32)], axis=0)
    if feat < LANES:
        tile = jnp.concatenate([tile, jnp.zeros((PAGE_SIZE, LANES - feat), _F32)], axis=1)
    buf[slot, :, past:past + PAGE_SIZE] = tile.T[:feat, :].astype(buf.dtype)


def _sample_scores_kernel(pt_ref, qi_ref, wi_ref, kin_ref, ci_hbm, s_ref, ibuf, sems,
                          *, npages, tk, nchunks):
    b = pl.program_id(0)
    past = npages * PAGE_SIZE
    tq = qi_ref.shape[1]
    slot = _stream_pages(pt_ref, b, pl.num_programs(0), npages, [ci_hbm], [ibuf], sems)
    _put_new_keys(ibuf, slot, past, kin_ref[0])

    qi = qi_ref[0].astype(_F32)
    wi = wi_ref[0]
    qi8 = jnp.concatenate([qi[:, h * D_IDX:(h + 1) * D_IDX] for h in range(N_IDX_HEADS)],
                          axis=0).astype(_MXU_DTYPE)
    w_wide = [jnp.broadcast_to(wi[:, h:h + 1], (tq, tk)) for h in range(N_IDX_HEADS)]
    key_minus_query = (lax.broadcasted_iota(jnp.int32, (tq, tk), 1)
                       - lax.broadcasted_iota(jnp.int32, (tq, tk), 0))
    for c in range(nchunks):
        keys_t = ibuf[slot, :, c * tk:(c + 1) * tk].astype(_MXU_DTYPE)
        r = jnp.dot(qi8, keys_t, preferred_element_type=_F32)
        s = None
        for h in range(N_IDX_HEADS):
            t = jnp.maximum(r[h * tq:(h + 1) * tq, :], 0.0) * w_wide[h]
            s = t if s is None else s + t
        s_ref[0, c] = jnp.where(key_minus_query <= past - c * tk, s, _NEG_INF)


def _sample_threshold_kernel(s_ref, f0_ref, nadm_ref, out_ref, thr_ref, *, tk, nchunks, k_sel):
    rows = s_ref.shape[2]
    kf = jnp.float32(k_sel)
    groups = tk // LANES
    tile = lambda col: jnp.broadcast_to(col, (rows, tk))

    def fold_with(x, op):
        out = x[:, :LANES]
        for g in range(1, groups):
            out = op(out, x[:, g * LANES:(g + 1) * LANES])
        return out

    def over_chunks(fn, op, init):
        def body(c, acc):
            return op(acc, fold_with(fn(s_ref[0, c]), op))
        return lax.fori_loop(0, nchunks, body, jnp.full((rows, LANES), init, _F32))

    def count_ge(t):
        t_wide = tile(t)
        acc = over_chunks(lambda s: jnp.where(s >= t_wide, 1.0, 0.0), jnp.add, 0.0)
        return jnp.sum(acc, axis=1, keepdims=True)

    def max_below(x):
        x_wide = tile(x)
        acc = over_chunks(lambda s: jnp.where(s < x_wide, s, _NEG_INF), jnp.maximum, -jnp.inf)
        return jnp.max(acc, axis=1, keepdims=True)

    finite_min = over_chunks(lambda s: jnp.where(s > _NEG_INF, s, jnp.inf), jnp.minimum, jnp.inf)
    finite_max = over_chunks(lambda s: s, jnp.maximum, -jnp.inf)
    smin = jnp.min(finite_min, axis=1, keepdims=True)
    smax = jnp.max(finite_max, axis=1, keepdims=True)
    thr, clo, chi, short = _topk_threshold(count_ge, max_below, smin, smax, nadm_ref[...],
                                           f0_ref[...], k_sel)
    thr_ref[...] = thr
    need = jnp.logical_and(clo > kf, jnp.logical_not(short))
    any_ties = jnp.max(jnp.where(need, 1.0, 0.0)) > 0.5

    @pl.when(jnp.logical_not(any_ties))
    def _():
        out_ref[...] = s_ref[...]

    @pl.when(any_ties)
    def _():
        keep_wide = tile(jnp.where(need, kf - chi, _FLT_MAX))
        thr_wide = tile(thr)
        rk = lax.broadcasted_iota(jnp.int32, (tk, tk), 0)
        ck = lax.broadcasted_iota(jnp.int32, (tk, tk), 1)
        earlier = jnp.where(rk < ck, 1.0, 0.0).astype(_MXU_DTYPE)

        def drop(c, seen):
            s = s_ref[0, c]
            tie = s == thr_wide
            tm = jnp.where(tie, 1.0, 0.0)
            before = (jnp.dot(tm.astype(_MXU_DTYPE), earlier, preferred_element_type=_F32)
                      + tile(seen))
            out_ref[0, c] = jnp.where(jnp.where(tie, before, -1.0) >= keep_wide, _NEG_INF, s)
            return seen + jnp.sum(tm, axis=1, keepdims=True)

        lax.fori_loop(0, nchunks, drop, jnp.zeros((rows, 1), _F32))


def _sample_attend_kernel(pt_ref, qs_ref, zbs_ref, kn_ref, vn_ref, s_ref, thr_ref,
                          ck_hbm, cv_hbm, out_ref, kbuf, vbuf, sems, l_ref,
                          *, npages, tk, nchunks):
    b = pl.program_id(0)
    past = npages * PAGE_SIZE
    tq = qs_ref.shape[1]
    slot = _stream_pages(pt_ref, b, pl.num_programs(0), npages, [ck_hbm, cv_hbm], [kbuf, vbuf],
                         sems)
    _put_new_keys(kbuf, slot, past, kn_ref[0])
    _put_new_keys(vbuf, slot, past, vn_ref[0])

    qs = qs_ref[0].astype(_F32)
    lane = lax.broadcasted_iota(jnp.int32, (tq, LANES), 1)
    lo_half = lane < HEAD_DIM
    q8_rows = []
    for j in range(N_HEADS // 2):
        blk = qs[:, j * LANES:(j + 1) * LANES]
        q8_rows.append(jnp.where(lo_half, blk, 0.0))
        q8_rows.append(jnp.where(lo_half, 0.0, blk))
    q8 = jnp.concatenate(q8_rows, axis=0).astype(_MXU_DTYPE)
    thr_wide = jnp.broadcast_to(thr_ref[...], (tq, tk))

    mx = jnp.full((N_HEADS * tq, tk), -jnp.inf, _F32)
    for c in range(nchunks):
        keys_t = kbuf[slot, :, c * tk:(c + 1) * tk].astype(_MXU_DTYPE)
        lg = jnp.dot(q8, keys_t, preferred_element_type=_F32)
        sel = s_ref[0, c] >= thr_wide
        lg = jnp.concatenate(
            [jnp.where(sel, lg[h * tq:(h + 1) * tq, :], _NEG_INF) for h in range(N_HEADS)], axis=0)
        l_ref[c] = lg
        mx = jnp.maximum(mx, lg)
    m_wide = jnp.broadcast_to(jnp.max(mx, axis=1, keepdims=True), (N_HEADS * tq, tk))

    nt_dims = (((1,), (1,)), ((), ()))
    psum = jnp.zeros((N_HEADS * tq, tk), _F32)
    acc = jnp.zeros((N_HEADS * tq, KV_W), _F32)
    for c in range(nchunks):
        p = jnp.exp(l_ref[c] - m_wide)
        psum = psum + p
        vals_t = vbuf[slot, :, c * tk:(c + 1) * tk].astype(_MXU_DTYPE)
        acc = acc + lax.dot_general(p.astype(_MXU_DTYPE), vals_t, nt_dims,
                                    preferred_element_type=_F32)
    o = acc / jnp.sum(psum, axis=1, keepdims=True)
    blocks = []
    for j in range(N_HEADS // 2):
        blocks.append(jnp.where(lo_half, o[(2 * j) * tq:(2 * j + 1) * tq, :],
                                o[(2 * j + 1) * tq:(2 * j + 2) * tq, :]))
    out_ref[0] = jnp.concatenate(blocks, axis=1) * zbs_ref[0]


def _attn_sample(page_table, qs, qi, wi, zbs, k_new, v_new, ki_new, cache_kt, cache_vt, cache_it,
                 *, tk):
    nb, npages = page_table.shape
    tq = qs.shape[0] // nb
    past = npages * PAGE_SIZE
    nchunks = -(-(past + PAGE_SIZE) // tk)
    keys = nchunks * tk
    k_sel = min(TOP_K_MAX, (past + tq) // 4)
    group = LANES // tq
    assert LANES % tq == 0 and nb % group == 0
    three = lambda a: a.reshape(nb, tq, a.shape[-1])
    blk = lambda w: pl.BlockSpec((1, tq, w), lambda b, pt: (b, 0, 0))
    hbm = pl.BlockSpec(memory_space=pl.ANY)
    params = pltpu.CompilerParams(dimension_semantics=("arbitrary",),
                                  vmem_limit_bytes=VMEM_LIMIT_BYTES)
    pt_flat = page_table.reshape(-1)
    s_shape = jax.ShapeDtypeStruct((nb // group, nchunks, LANES, tk), _F32)
    s_blk = pl.BlockSpec((1, nchunks, tq, tk), lambda b, pt: (b // group, 0, b % group, 0))

    scores = pl.pallas_call(
        functools.partial(_sample_scores_kernel, npages=npages, tk=tk, nchunks=nchunks),
        grid_spec=pltpu.PrefetchScalarGridSpec(
            num_scalar_prefetch=1, grid=(nb,),
            in_specs=[blk(N_IDX_HEADS * D_IDX), blk(LANES), blk(D_IDX), hbm],
            out_specs=s_blk,
            scratch_shapes=[pltpu.VMEM((2, D_IDX, keys), cache_it.dtype),
                            pltpu.SemaphoreType.DMA((1, 2))]),
        out_shape=s_shape, compiler_params=params, name="sample_scores",
    )(pt_flat, three(qi), three(wi), three(ki_new), cache_it)

    n_adm = past + 1 + (jnp.arange(LANES) % tq)
    whole = pl.BlockSpec((1, nchunks, LANES, tk), lambda g: (g, 0, 0, 0))
    col = pl.BlockSpec((LANES, 1), lambda g: (0, 0))
    kept, thr = pl.pallas_call(
        functools.partial(_sample_threshold_kernel, tk=tk, nchunks=nchunks, k_sel=k_sel),
        grid=(nb // group,),
        in_specs=[whole, col, col],
        out_specs=(whole, pl.BlockSpec((LANES, 1), lambda g: (g, 0))),
        out_shape=(s_shape, jax.ShapeDtypeStruct((nb * tq, 1), _F32)),
        compiler_params=params, name="sample_threshold",
    )(scores, _seed_fraction(n_adm, k_sel).reshape(LANES, 1),
      n_adm.astype(_F32).reshape(LANES, 1))

    out = pl.pallas_call(
        functools.partial(_sample_attend_kernel, npages=npages, tk=tk, nchunks=nchunks),
        grid_spec=pltpu.PrefetchScalarGridSpec(
            num_scalar_prefetch=1, grid=(nb,),
            in_specs=[blk(D_B), blk(D_B), blk(KV_W), blk(KV_W), s_blk,
                      pl.BlockSpec((tq, 1), lambda b, pt: (b, 0)), hbm, hbm],
            out_specs=blk(D_B),
            scratch_shapes=[pltpu.VMEM((2, KV_W, keys), cache_kt.dtype),
                            pltpu.VMEM((2, KV_W, keys), cache_vt.dtype),
                            pltpu.SemaphoreType.DMA((2, 2)),
                            pltpu.VMEM((nchunks, N_HEADS * tq, tk), _F32)]),
        out_shape=jax.ShapeDtypeStruct((nb, tq, D_B), _F32),
        compiler_params=params, name="sample_attend",
    )(pt_flat, three(qs), three(zbs), three(k_new), three(v_new), kept, thr, cache_kt, cache_vt)
    return out.reshape(nb * tq, D_B)


def _outproj_kernel(yag_ref, ybg_ref, x_ref, w_ref, g_ref, out_ref):
    y = (jnp.dot(yag_ref[...], w_ref[:D_A, :], preferred_element_type=_F32)
         + jnp.dot(ybg_ref[...].astype(_MXU_DTYPE), w_ref[D_A:, :], preferred_element_type=_F32))
    ms = jnp.mean(y * y, axis=-1, keepdims=True)
    out_ref[...] = x_ref[...] + y * lax.rsqrt(ms + EPS) * g_ref[...]


def _outproj(yag, ybg, x2d, w_out_p, g_post, *, tm):
    r = x2d.shape[0]
    row = lambda w: pl.BlockSpec((tm, w), lambda i: (i, 0))
    const = lambda shape: pl.BlockSpec(shape, lambda i: (0,) * len(shape))
    return pl.pallas_call(
        _outproj_kernel,
        grid=(r // tm,),
        in_specs=[row(D_A), row(D_B), row(D_MODEL), const((D_A + D_B, D_MODEL)),
                  const((1, D_MODEL))],
        out_specs=row(D_MODEL),
        out_shape=jax.ShapeDtypeStruct((r, D_MODEL), _F32),
        compiler_params=pltpu.CompilerParams(
            dimension_semantics=("arbitrary",), vmem_limit_bytes=VMEM_LIMIT_BYTES),
        name="outproj",
    )(yag, ybg, x2d, w_out_p, g_post)


def _pad_in_weights(w):
    perm = _head_perm()
    nat = np.cumsum([0, D_A, D_A, D_A, D_B, KV_W, KV_W, D_B, N_IDX_HEADS * D_IDX, D_IDX,
                     N_IDX_HEADS])
    seg = lambda n: w[:, nat[n]:nat[n + 1]]
    pad = jnp.zeros((w.shape[0], C_END - C_WI - N_IDX_HEADS), w.dtype)
    return jnp.concatenate(
        [seg(0), seg(1), seg(2), seg(3)[:, perm], seg(4), seg(5), seg(6)[:, perm], seg(7),
         seg(8), seg(8), seg(9), pad], axis=1).astype(_MXU_DTYPE)


def _layer(x_prompt, x_sample, cache_k, cache_v, cache_idx_k, page_table, g_pre, w_in, ln_v_g,
           ln_v_b, w_s, b_s, w_out, g_post):
    nb, t, _ = x_prompt.shape
    nd, td, _ = x_sample.shape
    npages = page_table.shape[1]
    past = npages * PAGE_SIZE
    tm = 1024
    tq = 128
    tk = 1024
    tk_sample = 512

    w_pad = _pad_in_weights(w_in)
    perm = _head_perm()
    w_out_p = jnp.concatenate([w_out[:D_A], w_out[D_A:][perm]], axis=0).astype(_MXU_DTYPE)
    gpre = g_pre.reshape(1, D_MODEL)
    gpost = g_post.reshape(1, D_MODEL)
    lng = ln_v_g.reshape(1, D_A)
    lnb = ln_v_b.reshape(1, D_A)
    tril = jnp.tril(jnp.ones((CHUNK, CHUNK), dtype=bool))
    ws_tril = jnp.where(tril[None], w_s, jnp.zeros_like(w_s))

    xp = x_prompt.reshape(nb * t, D_MODEL)
    tpb = t // tm
    wmix_p = ws_tril.astype(_MXU_DTYPE)
    bmix_p = jnp.repeat(b_s.T, A_GROUP_DIM, axis=1)
    outs = _inproj(xp, _rope_tables(jnp.arange(t, dtype=jnp.int32)), lambda i: (i % tpb, 0),
                   gpre, w_pad, lng, lnb, wmix_p, bmix_p, tm=tm, state_rows=CHUNK,
                   state_shape=(nb, CHUNK, D_A), state_index=lambda i: (i // tpb, 0, 0), seqs=nb)
    yag, zbs, qs, qi, k_t, v_t, ki_t, wi, kbf, vt3, kibf, vstate_p = outs
    ybg = _attn_prompt(qs, qi, wi, zbs, kibf, kbf, vt3, nb=nb, t=t, tq=tq, tk=tk)
    y_prompt = _outproj(yag, ybg, xp, w_out_p, gpost, tm=tm).reshape(nb, t, D_MODEL)

    xs = x_sample.reshape(nd * td, D_MODEL)
    reps = CHUNK // td
    eye = jnp.eye(reps, dtype=w_s.dtype)
    wmix_s = jnp.stack([jnp.kron(eye, ws_tril[g, :td, :td]) for g in range(N_A_GROUPS)]
                       ).astype(_MXU_DTYPE)
    bmix_s = jnp.repeat(jnp.tile(b_s[:, :td], (1, reps)).T, A_GROUP_DIM, axis=1)
    tm_s = min(tm, nd * td)
    pos_s = past + (jnp.arange(tm_s, dtype=jnp.int32) % td)
    outs = _inproj(xs, _rope_tables(pos_s), lambda i: (0, 0), gpre, w_pad, lng, lnb, wmix_s,
                   bmix_s, tm=tm_s, state_rows=tm_s, state_shape=(nd * td, D_A),
                   state_index=lambda i: (i, 0))
    yag_s, zbs_s, qs_s, qi_s, k32_s, v32_s, ki32_s, wi_s, _, _, _, vstate_s = outs
    pool = cache_k.shape[0]
    ybg_s = _attn_sample(page_table, qs_s, qi_s, wi_s, zbs_s, k32_s, v32_s, ki32_s,
                         jnp.transpose(cache_k, (0, 2, 3, 1)).reshape(pool, KV_W, PAGE_SIZE),
                         jnp.transpose(cache_v, (0, 2, 3, 1)).reshape(pool, KV_W, PAGE_SIZE),
                         jnp.transpose(cache_idx_k, (0, 2, 1)), tk=tk_sample)
    y_sample = _outproj(yag_s, ybg_s, xs, w_out_p, gpost, tm=tm_s).reshape(nd, td, D_MODEL)

    return (y_prompt, y_sample,
            jnp.transpose(k_t.reshape(nb, N_KV_HEADS, HEAD_DIM, t), (0, 3, 1, 2)),
            jnp.transpose(v_t.reshape(nb, N_KV_HEADS, HEAD_DIM, t), (0, 3, 1, 2)),
            jnp.transpose(ki_t, (0, 2, 1)), vstate_p,
            k32_s.reshape(nd, td, N_KV_HEADS, HEAD_DIM), v32_s.reshape(nd, td, N_KV_HEADS, HEAD_DIM),
            ki32_s.reshape(nd, td, D_IDX), vstate_s.reshape(nd, td, D_A))


def kernel(x_prompt, x_sample, cache_k, cache_v, cache_idx_k, page_table, g_pre, w_in, ln_v_g,
           ln_v_b, w_s, b_s, w_out, g_post):
    xp, xs = x_prompt, x_sample
    per_layer = []
    for l in range(g_pre.shape[0]):
        outs = _layer(xp, xs, cache_k[l], cache_v[l], cache_idx_k[l], page_table, g_pre[l],
                      w_in[l], ln_v_g[l], ln_v_b[l], w_s[l], b_s[l], w_out[l], g_post[l])
        xp, xs = outs[0], outs[1]
        per_layer.append(outs[2:])
    stacked = tuple(jnp.stack(leaves, 0) for leaves in zip(*per_layer))
    return (xp, xs) + stacked
```

```python
import functools

import numpy as np
import jax
import jax.numpy as jnp
from jax import lax
from jax.scipy.special import ndtri
from jax.experimental import pallas as pl
from jax.experimental.pallas import tpu as pltpu

_F32 = jnp.float32
_MXU_DTYPE = jnp.bfloat16

D_MODEL = 1024
D_A = 512
D_B = 512
CHUNK = 128
N_A_GROUPS = 4
A_GROUP_DIM = D_A // N_A_GROUPS
HEAD_DIM = 64
N_HEADS = 8
N_KV_HEADS = 2
KV_W = N_KV_HEADS * HEAD_DIM
ROT_DIM = HEAD_DIM // 4
N_IDX_HEADS = 8
D_IDX = 64
TOP_K_MAX = 256
ROPE_THETA = 500000.0
EPS = 1e-6
PAGE_SIZE = 128

LANES = 128
VMEM_LIMIT_BYTES = 56 * 1024 * 1024

C_U, C_V, C_ZA, C_Q, C_K, C_VV, C_ZB, C_QI, C_KI, C_WI, C_END = (
    0, 512, 1024, 1536, 2048, 2176, 2304, 2816, 3328, 3456, 3584)

_FLT_MAX = float(np.finfo(np.float32).max)
_NEG_INF = float("-inf")


def _head_perm():
    c = np.arange(N_HEADS * HEAD_DIM)
    j = c // LANES
    half = (c % LANES) // HEAD_DIM
    d = c % HEAD_DIM
    return (j + 4 * half) * HEAD_DIM + d


def _rope_tables(pos):
    half = ROT_DIM // 2
    inv = jnp.power(jnp.float32(ROPE_THETA), -jnp.arange(half, dtype=_F32) * 2.0 / ROT_DIM)
    ang = pos.astype(_F32)[:, None] * inv[None, :]
    cos = jnp.cos(ang)
    sin = jnp.sin(ang)
    l64 = np.arange(LANES) % HEAD_DIM
    fidx = l64 % half
    in_rot = (l64 < ROT_DIM)[None, :]
    first = (l64 < half)[None, :]
    second = ((l64 >= half) & (l64 < ROT_DIM))[None, :]
    cos_t = jnp.where(in_rot, cos[:, fidx], 1.0)
    sin_a = jnp.where(first, -sin[:, fidx], 0.0)
    sin_b = jnp.where(second, sin[:, fidx], 0.0)
    return cos_t, sin_a, sin_b


def _silu(z):
    return z * (1.0 / (1.0 + jnp.exp(-z)))


def _inproj_kernel(x_ref, cos_ref, sa_ref, sb_ref, gpre_ref, win_ref, lng_ref, lnb_ref,
                   wmix_ref, bmix_ref,
                   yag_ref, zbs_ref, qs_ref, qi_ref, k32_ref, v32_ref, ki32_ref, wi_ref,
                   kbf_ref, vt_ref, kibf_ref, vstate_ref, *, tm, state_rows, kv_transposed):
    x = x_ref[...]
    ms = jnp.mean(x * x, axis=-1, keepdims=True)
    h = (x * lax.rsqrt(ms + EPS) * gpre_ref[...]).astype(_MXU_DTYPE)

    def proj(a, b):
        return jnp.dot(h, win_ref[:, a:b], preferred_element_type=_F32)

    cos_t = cos_ref[...]
    sin_a = sa_ref[...]
    sin_b = sb_ref[...]

    def rope(p):
        blocks = []
        for j in range(p.shape[1] // LANES):
            pj = p[:, j * LANES:(j + 1) * LANES]
            blocks.append(pj * cos_t
                          + pltpu.roll(pj, LANES - ROT_DIM // 2, 1) * sin_a
                          + pltpu.roll(pj, ROT_DIM // 2, 1) * sin_b)
        return blocks[0] if len(blocks) == 1 else jnp.concatenate(blocks, axis=1)

    u = proj(C_U, C_V)
    v = proj(C_V, C_ZA)
    mu = jnp.mean(v, axis=-1, keepdims=True)
    vc = v - mu
    var = jnp.mean(vc * vc, axis=-1, keepdims=True)
    vn = vc * lax.rsqrt(var + EPS) * lng_ref[...] + lnb_ref[...]
    vstate_ref[...] = vn[tm - state_rows:, :].reshape(vstate_ref.shape)
    vnb = vn.astype(_MXU_DTYPE)
    bmix = bmix_ref[...]
    rows = []
    for c in range(tm // CHUNK):
        cols = []
        for g in range(N_A_GROUPS):
            blk = vnb[c * CHUNK:(c + 1) * CHUNK, g * A_GROUP_DIM:(g + 1) * A_GROUP_DIM]
            cols.append(jnp.dot(wmix_ref[g], blk, preferred_element_type=_F32)
                        + bmix[:, g * A_GROUP_DIM:(g + 1) * A_GROUP_DIM])
        rows.append(jnp.concatenate(cols, axis=1))
    mixed = jnp.concatenate(rows, axis=0)
    za = proj(C_ZA, C_Q)
    yag_ref[...] = ((u * mixed) * _silu(za)).astype(yag_ref.dtype)

    q = rope(proj(C_Q, C_K))
    qs_ref[...] = (q * (HEAD_DIM ** -0.5)).astype(qs_ref.dtype)
    k = rope(proj(C_K, C_VV))
    kbf_ref[...] = k.astype(kbf_ref.dtype)
    vv = proj(C_VV, C_ZB)
    vv_t = vv.T
    vt_ref[0] = vv_t.astype(vt_ref.dtype)
    if kv_transposed:
        k32_ref[0] = k.T
        v32_ref[0] = vv_t
    else:
        k32_ref[...] = k
        v32_ref[...] = vv
    zb = proj(C_ZB, C_QI)
    zbs_ref[...] = _silu(zb)
    qi = rope(proj(C_QI, C_KI))
    qi_ref[...] = qi.astype(qi_ref.dtype)
    ki = rope(proj(C_KI, C_WI))
    if kv_transposed:
        ki32_ref[0] = ki.T[:D_IDX, :]
    else:
        ki32_ref[...] = ki[:, :D_IDX]
    kibf_ref[...] = ki.astype(kibf_ref.dtype)
    wi_ref[...] = proj(C_WI, C_END) * (N_IDX_HEADS ** -0.5)


def _inproj(x2d, tables, table_index, gpre, w_pad, lng, lnb, wmix, bmix, *, tm, state_rows,
            state_shape, state_index, seqs=None):
    r = x2d.shape[0]
    nt = r // tm
    if seqs is None:
        kv = lambda w: (jax.ShapeDtypeStruct((r, w), _F32), pl.BlockSpec((tm, w), lambda i: (i, 0)))
    else:
        tps = nt // seqs
        kv = lambda w: (jax.ShapeDtypeStruct((seqs, w, r // seqs), _F32),
                        pl.BlockSpec((1, w, tm), lambda i: (i // tps, 0, i % tps)))
    cos_t, sin_a, sin_b = tables
    row = lambda w: pl.BlockSpec((tm, w), lambda i: (i, 0))
    const = lambda shape: pl.BlockSpec(shape, lambda i: (0,) * len(shape))
    tab = pl.BlockSpec((tm, LANES), table_index)
    out_shapes = (
        jax.ShapeDtypeStruct((r, D_A), _MXU_DTYPE),
        jax.ShapeDtypeStruct((r, D_B), _F32),
        jax.ShapeDtypeStruct((r, D_B), _MXU_DTYPE),
        jax.ShapeDtypeStruct((r, N_IDX_HEADS * D_IDX), _MXU_DTYPE),
        kv(KV_W)[0],
        kv(KV_W)[0],
        kv(D_IDX)[0],
        jax.ShapeDtypeStruct((r, LANES), _F32),
        jax.ShapeDtypeStruct((r, KV_W), _MXU_DTYPE),
        jax.ShapeDtypeStruct((nt, KV_W, tm), _MXU_DTYPE),
        jax.ShapeDtypeStruct((r, 2 * D_IDX), _MXU_DTYPE),
        jax.ShapeDtypeStruct(state_shape, _F32),
    )
    out_specs = (
        row(D_A), row(D_B), row(D_B), row(N_IDX_HEADS * D_IDX), kv(KV_W)[1], kv(KV_W)[1],
        kv(D_IDX)[1], row(LANES), row(KV_W),
        pl.BlockSpec((1, KV_W, tm), lambda i: (i, 0, 0)),
        row(2 * D_IDX),
        pl.BlockSpec((1,) * (len(state_shape) - 2) + (state_rows, D_A), state_index),
    )
    return pl.pallas_call(
        functools.partial(_inproj_kernel, tm=tm, state_rows=state_rows,
                          kv_transposed=seqs is not None),
        grid=(nt,),
        in_specs=[row(D_MODEL), tab, tab, tab, const((1, D_MODEL)), const((D_MODEL, C_END)),
                  const((1, D_A)), const((1, D_A)), const((N_A_GROUPS, CHUNK, CHUNK)),
                  const((CHUNK, D_A))],
        out_specs=out_specs,
        out_shape=out_shapes,
        compiler_params=pltpu.CompilerParams(
            dimension_semantics=("arbitrary",), vmem_limit_bytes=VMEM_LIMIT_BYTES),
        name="inproj",
    )(x2d, cos_t, sin_a, sin_b, gpre, w_pad, lng, lnb, wmix, bmix)


def _f32_to_key(x):
    b = lax.bitcast_convert_type(x, jnp.int32)
    return jnp.where(b >= 0, b, b ^ jnp.int32(0x7FFFFFFF))


def _key_to_f32(k):
    return lax.bitcast_convert_type(jnp.where(k >= 0, k, k ^ jnp.int32(0x7FFFFFFF)), _F32)


N_INTERP_PASSES = 13
HUGE_SUM = 1e30
PAGE_LOOP_UNROLL = 8


def _topk_threshold(count_ge, max_below, smin, smax, n_adm, f0, k_sel):
    kf = jnp.float32(k_sel)
    log_k = jnp.log(kf - 0.5)
    lo0 = _f32_to_key(smin)
    hi0 = _f32_to_key(smax) + 1
    short = n_adm <= kf
    done0 = jnp.where(jnp.logical_or(short, hi0 <= lo0 + 1), 1.0, 0.0)
    g_hi0 = jnp.full(smin.shape, np.log(0.5), _F32) - log_k

    def all_done(done):
        return jnp.min(done) > 0.5

    def probe(carry, frac):
        it, lo, hi, clo, chi, glo, ghi, last, done = carry
        xlo = _key_to_f32(lo)
        xhi = _key_to_f32(hi)
        cand = jnp.clip(_f32_to_key(xlo + (xhi - xlo) * frac), lo + 1, hi - 1)
        cnt = count_ge(_key_to_f32(cand))
        g = jnp.log(jnp.maximum(cnt, 0.5)) - log_k
        active = done < 0.5
        new_lo = jnp.logical_and(active, cnt >= kf)
        new_hi = jnp.logical_and(active, cnt < kf)
        ghi = jnp.where(jnp.logical_and(new_lo, last == 1), ghi * 0.5, ghi)
        glo = jnp.where(jnp.logical_and(new_hi, last == 2), glo * 0.5, glo)
        lo = jnp.where(new_lo, cand, lo)
        clo = jnp.where(new_lo, cnt, clo)
        glo = jnp.where(new_lo, g, glo)
        hi = jnp.where(new_hi, cand, hi)
        chi = jnp.where(new_hi, cnt, chi)
        ghi = jnp.where(new_hi, g, ghi)
        last = jnp.where(new_lo, 1, jnp.where(new_hi, 2, last))
        done = jnp.where(jnp.logical_or(clo == kf, hi <= lo + 1), 1.0, done)
        return it + 1, lo, hi, clo, chi, glo, ghi, last, done

    def interp_body(_, carry):
        return probe(carry, carry[5] / (carry[5] - carry[6]))

    init = (jnp.int32(0), lo0, hi0, n_adm, jnp.zeros_like(n_adm),
            jnp.log(jnp.maximum(n_adm, 1.0)) - log_k, g_hi0, jnp.zeros(smin.shape, jnp.int32),
            done0)
    state = lax.fori_loop(1, N_INTERP_PASSES, interp_body, probe(init, f0))
    _, lo, hi, clo, chi, _, _, _, done = state

    def peel_cond(carry):
        return jnp.logical_not(all_done(carry[-1]))

    def peel_body(carry):
        lo, hi, clo, chi, done = carry
        top = max_below(_key_to_f32(hi))
        cnt = count_ge(top)
        active = done < 0.5
        hit = jnp.logical_and(active, cnt >= kf)
        miss = jnp.logical_and(active, cnt < kf)
        lo = jnp.where(hit, _f32_to_key(top), lo)
        clo = jnp.where(hit, cnt, clo)
        hi = jnp.where(miss, _f32_to_key(top), hi)
        chi = jnp.where(miss, cnt, chi)
        done = jnp.where(hit, 1.0, done)
        return lo, hi, clo, chi, done

    lo, _, clo, chi, _ = lax.while_loop(peel_cond, peel_body, (lo, hi, clo, chi, done))
    thr = jnp.where(short, -_FLT_MAX, _key_to_f32(lo))
    return thr, clo, chi, short


def _seed_fraction(n_adm, k_sel):
    n = n_adm.astype(_F32)
    z_k = ndtri(1.0 - jnp.minimum(k_sel / n, 0.999))
    z_n = ndtri(1.0 - 1.0 / (n + 1.0))
    return (0.5 + 0.5 * z_k / z_n).astype(_F32)


def _attn_prompt_kernel(qs_ref, qi_ref, wi_ref, zbs_ref, f0_ref, qin_ref, win_ref, ki_ref, k_ref,
                        vt_ref, out_ref,
                        s2_ref, q8_ref, qi8_ref, bias_ref, pt_ref, acc_ref, m_ref, l_ref,
                        scale_ref, range_ref, ties_ref, *, tq, tk, k_sel):
    i = pl.program_id(1)
    i_next = jnp.minimum(i + 1, pl.num_programs(1) - 1)
    ratio = tk // tq
    nchunks = lax.div(i + ratio, ratio)
    nchunks_next = lax.div(i_next + ratio, ratio)
    kf = jnp.float32(k_sel)
    cur = lax.rem(i, 2)
    s_ref = s2_ref.at[cur]
    s_next_ref = s2_ref.at[1 - cur]

    lane = lax.broadcasted_iota(jnp.int32, (tq, LANES), 1)
    lo_half = lane < HEAD_DIM
    def stack_heads(src, dst):
        for j in range(N_HEADS // 2):
            blk = src[:, j * LANES:(j + 1) * LANES].astype(_F32)
            dst[(2 * j) * tq:(2 * j + 1) * tq, :] = jnp.where(lo_half, blk, 0.0).astype(dst.dtype)
            dst[(2 * j + 1) * tq:(2 * j + 2) * tq, :] = jnp.where(lo_half, 0.0, blk).astype(dst.dtype)

    stack_heads(qs_ref, q8_ref)

    row = lax.broadcasted_iota(jnp.int32, (tk, tq), 0)
    col = lax.broadcasted_iota(jnp.int32, (tk, tq), 1)
    key_minus_query = row - col
    nt_dims = (((1,), (1,)), ((), ()))
    fold = lambda x: x.reshape(tk // 32, 32, tq)

    def score_chunk(c, carry, blk, dst_ref, w_t):
        mn, mx = carry
        k0 = pl.multiple_of(c * tk, tk)
        rt = lax.dot_general(ki_ref[pl.ds(k0, tk), :], qi8_ref[...], nt_dims,
                             preferred_element_type=_F32)
        s = None
        for hb in range(N_IDX_HEADS):
            t = jnp.maximum(rt[:, hb * tq:(hb + 1) * tq], 0.0) * w_t[hb:hb + 1, :]
            s = t if s is None else s + t
        dst_ref[c] = jnp.where(key_minus_query <= blk * tq - c * tk, s, _NEG_INF)
        return jnp.minimum(mn, fold(s).min(axis=0)), jnp.maximum(mx, fold(s).max(axis=0))

    no_range = (jnp.full((32, tq), jnp.inf, _F32), jnp.full((32, tq), -jnp.inf, _F32))

    @pl.when(i == 0)
    def _():
        stack_heads(qi_ref, qi8_ref)
        w_t = wi_ref[...].T
        mn, mx = lax.fori_loop(0, nchunks,
                               lambda c, carry: score_chunk(c, carry, i, s_ref, w_t), no_range)
        range_ref[cur, 0] = mn
        range_ref[cur, 1] = mx

    smin = jnp.min(range_ref[cur, 0], axis=0, keepdims=True)
    smax = jnp.max(range_ref[cur, 1], axis=0, keepdims=True)

    def count_ge(t):
        def body(c, acc):
            return acc + fold(jnp.where(s_ref[c] >= t, 1.0, 0.0)).sum(axis=0)
        acc = lax.fori_loop(0, nchunks, body, jnp.zeros((32, tq), _F32))
        return jnp.sum(acc, axis=0, keepdims=True)

    def max_below(x):
        def body(c, acc):
            s = s_ref[c]
            return jnp.maximum(acc, fold(jnp.where(s < x, s, _NEG_INF)).max(axis=0))
        acc = lax.fori_loop(0, nchunks, body, jnp.full((32, tq), -jnp.inf, _F32))
        return jnp.max(acc, axis=0, keepdims=True)

    n_adm = (i * tq + 1 + lax.broadcasted_iota(jnp.int32, (1, tq), 1)).astype(_F32)
    thr, clo, chi, short = _topk_threshold(count_ge, max_below, smin, smax, n_adm, f0_ref[0],
                                           k_sel)

    need = jnp.logical_and(clo > kf, jnp.logical_not(short))

    @pl.when(jnp.max(jnp.where(need, 1.0, 0.0)) > 0.5)
    def _():
        keep = jnp.where(need, kf - chi, _FLT_MAX)
        tied = lambda s: jnp.where(s == thr, 1.0, 0.0)
        col_sum = lambda x: jnp.sum(fold(x).sum(axis=0), axis=0, keepdims=True)

        def count_ties(c, carry):
            ties_ref[c] = jnp.broadcast_to(col_sum(tied(s_ref[c])), ties_ref.shape[1:])
            return carry

        lax.fori_loop(0, nchunks, count_ties, 0)

        def drop(c, seen):
            here = ties_ref[c][0:1, :]
            s = s_ref[c]
            s = jnp.where(jnp.where(s == thr, seen - keep, -1.0) >= 0.0, _NEG_INF, s)
            s_ref[c] = s
            runs_out = jnp.logical_and(seen < keep, seen + here > keep)

            @pl.when(jnp.max(jnp.where(runs_out, 1.0, 0.0)) > 0.5)
            def _():
                rk = lax.broadcasted_iota(jnp.int32, (tk, tk), 0)
                ck = lax.broadcasted_iota(jnp.int32, (tk, tk), 1)
                earlier = jnp.where(rk > ck, 1.0, 0.0).astype(_MXU_DTYPE)
                tm = tied(s)
                before = jnp.dot(earlier, tm.astype(_MXU_DTYPE),
                                 preferred_element_type=_F32) + seen
                s_ref[c] = jnp.where(jnp.where(tm > 0.5, before, -1.0) >= keep, _NEG_INF, s)

            return seen + here

        lax.fori_loop(0, nchunks, drop, jnp.zeros((1, tq), _F32))

    def chunk_probs(c, running_max):
        bias_ref[...] = jnp.where(s_ref[c] >= thr, 0.0, _NEG_INF)
        k_chunk = k_ref[pl.ds(pl.multiple_of(c * tk, tk), tk), :]
        slot = lax.rem(c, 2)
        for j in range(N_HEADS // 2):
            l2 = lax.dot_general(k_chunk, q8_ref[(2 * j) * tq:(2 * j + 2) * tq, :], nt_dims,
                                 preferred_element_type=_F32)
            for half in range(2):
                hb = 2 * j + half
                lh = l2[:, half * tq:(half + 1) * tq] + bias_ref[...]
                m_old = m_ref[hb:hb + 1, :]
                if running_max:
                    m_new = jnp.maximum(m_old,
                                        jnp.max(fold(lh).max(axis=0), axis=0, keepdims=True))
                    alpha = jnp.exp(m_old - m_new)
                    m_ref[hb:hb + 1, :] = m_new
                    scale_ref[0:1, hb * tq:(hb + 1) * tq] = alpha
                    p = jnp.exp(lh - m_new)
                    l_ref[hb:hb + 1, :] = (alpha * l_ref[hb:hb + 1, :]
                                           + jnp.sum(fold(p).sum(axis=0), axis=0, keepdims=True))
                else:
                    p = jnp.exp(lh - m_old)
                    l_ref[hb:hb + 1, :] = (l_ref[hb:hb + 1, :]
                                           + jnp.sum(fold(p).sum(axis=0), axis=0, keepdims=True))
                pt_ref[slot, :, hb * tq:(hb + 1) * tq] = p.astype(pt_ref.dtype)

    def values(c):
        return jnp.dot(vt_ref[c], pt_ref[lax.rem(c, 2)], preferred_element_type=_F32)

    def reset():
        m_ref[...] = jnp.full(m_ref.shape, -_FLT_MAX, _F32)
        l_ref[...] = jnp.zeros(l_ref.shape, _F32)
        acc_ref[...] = jnp.zeros(acc_ref.shape, _F32)

    stack_heads(qin_ref, qi8_ref)
    w_t_next = win_ref[...].T
    score_next = lambda c, carry: score_chunk(c, carry, i_next, s_next_ref, w_t_next)

    reset()
    chunk_probs(0, running_max=True)
    next_range = score_next(0, no_range)

    def attend(c, next_range):
        acc_ref[...] = acc_ref[...] + values(c - 1)
        chunk_probs(c, running_max=False)
        return score_next(c, next_range)

    mn, mx = lax.fori_loop(1, nchunks, attend, next_range)
    acc_ref[...] = acc_ref[...] + values(nchunks - 1)
    range_ref[1 - cur, 0] = mn
    range_ref[1 - cur, 1] = mx

    @pl.when(nchunks_next > nchunks)
    def _():
        mn, mx = score_next(nchunks, (range_ref[1 - cur, 0], range_ref[1 - cur, 1]))
        range_ref[1 - cur, 0] = mn
        range_ref[1 - cur, 1] = mx

    sane = jnp.logical_and(jnp.max(l_ref[...]) < HUGE_SUM,
                           jnp.max(jnp.abs(acc_ref[...])) < HUGE_SUM)
    sane = jnp.logical_and(sane, jnp.min(m_ref[...]) > -_FLT_MAX)

    @pl.when(jnp.logical_not(sane))
    def _():
        reset()

        def attend_rescaling(c, carry):
            chunk_probs(c, running_max=True)
            acc_ref[...] = acc_ref[...] * scale_ref[0:1, :] + values(c)
            return carry

        lax.fori_loop(0, nchunks, attend_rescaling, 0)

    l_all = jnp.concatenate([l_ref[hb:hb + 1, :] for hb in range(N_HEADS)], axis=1)
    o_t = acc_ref[...] / l_all
    blocks = []
    for j in range(N_HEADS // 2):
        mj = jnp.concatenate(
            [o_t[:HEAD_DIM, (2 * j) * tq:(2 * j + 1) * tq],
             o_t[HEAD_DIM:, (2 * j + 1) * tq:(2 * j + 2) * tq]], axis=0)
        blocks.append(mj.T)
    yb = jnp.concatenate(blocks, axis=1)
    out_ref[...] = (yb * zbs_ref[...]).astype(out_ref.dtype)


def _attn_prompt(qs, qi, wi, zbs, kibf, kbf, vt3, *, nb, t, tq, tk):
    nq = t // tq
    k_sel = min(TOP_K_MAX, t // 4)
    f0 = _seed_fraction(jnp.arange(1, t + 1), k_sel).reshape(nq, 1, tq)
    blk = lambda w: pl.BlockSpec((tq, w), lambda b, i: (b * nq + i, 0))
    nxt = lambda w: pl.BlockSpec((tq, w), lambda b, i: (b * nq + jnp.minimum(i + 1, nq - 1), 0))
    res = lambda w: pl.BlockSpec((t, w), lambda b, i: (b, 0))
    stat = pltpu.VMEM((N_HEADS, tq), _F32)
    kernel = functools.partial(_attn_prompt_kernel, tq=tq, tk=tk, k_sel=k_sel)
    return pl.pallas_call(
        kernel,
        grid=(nb, nq),
        in_specs=[blk(D_B), blk(N_IDX_HEADS * D_IDX), blk(LANES), blk(D_B),
                  pl.BlockSpec((1, 1, tq), lambda b, i: (i, 0, 0)),
                  nxt(N_IDX_HEADS * D_IDX), nxt(LANES),
                  res(2 * D_IDX), res(KV_W),
                  pl.BlockSpec((t // tk, KV_W, tk), lambda b, i: (b, 0, 0))],
        out_specs=blk(D_B),
        out_shape=jax.ShapeDtypeStruct((nb * t, D_B), _MXU_DTYPE),
        scratch_shapes=[
            pltpu.VMEM((2, t // tk, tk, tq), _F32),
            pltpu.VMEM((N_HEADS * tq, LANES), _MXU_DTYPE),
            pltpu.VMEM((N_IDX_HEADS * tq, LANES), _MXU_DTYPE),
            pltpu.VMEM((tk, tq), _F32),
            pltpu.VMEM((2, tk, N_HEADS * tq), _MXU_DTYPE),
            pltpu.VMEM((KV_W, N_HEADS * tq), _F32),
            stat, stat,
            pltpu.VMEM((1, N_HEADS * tq), _F32),
            pltpu.VMEM((2, 2, 32, tq), _F32),
            pltpu.VMEM((t // tk, 8, tq), _F32),
        ],
        compiler_params=pltpu.CompilerParams(
            dimension_semantics=("arbitrary", "arbitrary"), vmem_limit_bytes=VMEM_LIMIT_BYTES),
        name="attn_prompt",
    )(qs, qi, wi, zbs, f0, qi, wi, kibf, kbf, vt3)


def _stream_pages(pt_ref, b, nb, npages, sources, buffers, sems):
    slot = lax.rem(b, 2)

    def copies(bb, sl, p):
        phys = pt_ref[bb * npages + p]
        dst = pl.ds(pl.multiple_of(p * PAGE_SIZE, PAGE_SIZE), PAGE_SIZE)
        return [pltpu.make_async_copy(src.at[phys], buf.at[sl, :, dst], sems.at[i, sl])
                for i, (src, buf) in enumerate(zip(sources, buffers))]

    def start(bb, sl):
        def body(p, carry):
            for cp in copies(bb, sl, p):
                cp.start()
            return carry
        lax.fori_loop(0, npages, body, 0, unroll=PAGE_LOOP_UNROLL)

    def wait(bb, sl):
        def body(p, carry):
            for cp in copies(bb, sl, p):
                cp.wait()
            return carry
        lax.fori_loop(0, npages, body, 0, unroll=PAGE_LOOP_UNROLL)

    @pl.when(b == 0)
    def _():
        past = npages * PAGE_SIZE
        for buf in buffers:
            for sl in range(2):
                buf[sl, :, past:] = jnp.zeros((buf.shape[1], buf.shape[2] - past), buf.dtype)
        start(0, 0)

    @pl.when(b + 1 < nb)
    def _():
        start(b + 1, 1 - slot)

    wait(b, slot)
    return slot


def _put_new_keys(buf, slot, past, rows):
    tq, feat = rows.shape
    tile = jnp.concatenate([rows, jnp.zeros((PAGE_SIZE - tq, feat), _F32)], axis=0)
    if feat < LANES:
        tile = jnp.concatenate([tile, jnp.zeros((PAGE_SIZE, LANES - feat), _F32)], axis=1)
    buf[slot, :, past:past + PAGE_SIZE] = tile.T[:feat, :].astype(buf.dtype)


def _sample_scores_kernel(pt_ref, qi_ref, wi_ref, kin_ref, ci_hbm, s_ref, ibuf, sems,
                          *, npages, tk, nchunks):
    b = pl.program_id(0)
    past = npages * PAGE_SIZE
    tq = qi_ref.shape[1]
    slot = _stream_pages(pt_ref, b, pl.num_programs(0), npages, [ci_hbm], [ibuf], sems)
    _put_new_keys(ibuf, slot, past, kin_ref[0])

    qi = qi_ref[0].astype(_F32)
    wi = wi_ref[0]
    qi8 = jnp.concatenate([qi[:, h * D_IDX:(h + 1) * D_IDX] for h in range(N_IDX_HEADS)],
                          axis=0).astype(_MXU_DTYPE)
    w_wide = [jnp.broadcast_to(wi[:, h:h + 1], (tq, tk)) for h in range(N_IDX_HEADS)]
    key_minus_query = (lax.broadcasted_iota(jnp.int32, (tq, tk), 1)
                       - lax.broadcasted_iota(jnp.int32, (tq, tk), 0))
    for c in range(nchunks):
        keys_t = ibuf[slot, :, c * tk:(c + 1) * tk].astype(_MXU_DTYPE)
        r = jnp.dot(qi8, keys_t, preferred_element_type=_F32)
        s = None
        for h in range(N_IDX_HEADS):
            t = jnp.maximum(r[h * tq:(h + 1) * tq, :], 0.0) * w_wide[h]
            s = t if s is None else s + t
        s_ref[0, c] = jnp.where(key_minus_query <= past - c * tk, s, _NEG_INF)


def _sample_threshold_kernel(s_ref, f0_ref, nadm_ref, out_ref, thr_ref, *, tk, nchunks, k_sel):
    rows = s_ref.shape[2]
    kf = jnp.float32(k_sel)
    groups = tk // LANES
    tile = lambda col: jnp.broadcast_to(col, (rows, tk))

    def fold_with(x, op):
        out = x[:, :LANES]
        for g in range(1, groups):
            out = op(out, x[:, g * LANES:(g + 1) * LANES])
        return out

    def over_chunks(fn, op, init):
        def body(c, acc):
            return op(acc, fold_with(fn(s_ref[0, c]), op))
        return lax.fori_loop(0, nchunks, body, jnp.full((rows, LANES), init, _F32))

    def count_ge(t):
        t_wide = tile(t)
        acc = over_chunks(lambda s: jnp.where(s >= t_wide, 1.0, 0.0), jnp.add, 0.0)
        return jnp.sum(acc, axis=1, keepdims=True)

    def max_below(x):
        x_wide = tile(x)
        acc = over_chunks(lambda s: jnp.where(s < x_wide, s, _NEG_INF), jnp.maximum, -jnp.inf)
        return jnp.max(acc, axis=1, keepdims=True)

    finite_min = over_chunks(lambda s: jnp.where(s > _NEG_INF, s, jnp.inf), jnp.minimum, jnp.inf)
    finite_max = over_chunks(lambda s: s, jnp.maximum, -jnp.inf)
    smin = jnp.min(finite_min, axis=1, keepdims=True)
    smax = jnp.max(finite_max, axis=1, keepdims=True)
    thr, clo, chi, short = _topk_threshold(count_ge, max_below, smin, smax, nadm_ref[...],
                                           f0_ref[...], k_sel)
    thr_ref[...] = thr
    need = jnp.logical_and(clo > kf, jnp.logical_not(short))
    any_ties = jnp.max(jnp.where(need, 1.0, 0.0)) > 0.5

    @pl.when(jnp.logical_not(any_ties))
    def _():
        out_ref[...] = s_ref[...]

    @pl.when(any_ties)
    def _():
        keep_wide = tile(jnp.where(need, kf - chi, _FLT_MAX))
        thr_wide = tile(thr)
        rk = lax.broadcasted_iota(jnp.int32, (tk, tk), 0)
        ck = lax.broadcasted_iota(jnp.int32, (tk, tk), 1)
        earlier = jnp.where(rk < ck, 1.0, 0.0).astype(_MXU_DTYPE)

        def drop(c, seen):
            s = s_ref[0, c]
            tie = s == thr_wide
            tm = jnp.where(tie, 1.0, 0.0)
            before = (jnp.dot(tm.astype(_MXU_DTYPE), earlier, preferred_element_type=_F32)
                      + tile(seen))
            out_ref[0, c] = jnp.where(jnp.where(tie, before, -1.0) >= keep_wide, _NEG_INF, s)
            return seen + jnp.sum(tm, axis=1, keepdims=True)

        lax.fori_loop(0, nchunks, drop, jnp.zeros((rows, 1), _F32))


def _sample_attend_kernel(pt_ref, qs_ref, zbs_ref, kn_ref, vn_ref, s_ref, thr_ref,
                          ck_hbm, cv_hbm, out_ref, kbuf, vbuf, sems, l_ref,
                          *, npages, tk, nchunks):
    b = pl.program_id(0)
    past = npages * PAGE_SIZE
    tq = qs_ref.shape[1]
    slot = _stream_pages(pt_ref, b, pl.num_programs(0), npages, [ck_hbm, cv_hbm], [kbuf, vbuf],
                         sems)
    _put_new_keys(kbuf, slot, past, kn_ref[0])
    _put_new_keys(vbuf, slot, past, vn_ref[0])

    qs = qs_ref[0].astype(_F32)
    lane = lax.broadcasted_iota(jnp.int32, (tq, LANES), 1)
    lo_half = lane < HEAD_DIM
    q8_rows = []
    for j in range(N_HEADS // 2):
        blk = qs[:, j * LANES:(j + 1) * LANES]
        q8_rows.append(jnp.where(lo_half, blk, 0.0))
        q8_rows.append(jnp.where(lo_half, 0.0, blk))
    q8 = jnp.concatenate(q8_rows, axis=0).astype(_MXU_DTYPE)
    thr_wide = jnp.broadcast_to(thr_ref[...], (tq, tk))

    mx = jnp.full((N_HEADS * tq, tk), -jnp.inf, _F32)
    for c in range(nchunks):
        keys_t = kbuf[slot, :, c * tk:(c + 1) * tk].astype(_MXU_DTYPE)
        lg = jnp.dot(q8, keys_t, preferred_element_type=_F32)
        sel = s_ref[0, c] >= thr_wide
        lg = jnp.concatenate(
            [jnp.where(sel, lg[h * tq:(h + 1) * tq, :], _NEG_INF) for h in range(N_HEADS)], axis=0)
        l_ref[c] = lg
        mx = jnp.maximum(mx, lg)
    m_wide = jnp.broadcast_to(jnp.max(mx, axis=1, keepdims=True), (N_HEADS * tq, tk))

    nt_dims = (((1,), (1,)), ((), ()))
    psum = jnp.zeros((N_HEADS * tq, tk), _F32)
    acc = jnp.zeros((N_HEADS * tq, KV_W), _F32)
    for c in range(nchunks):
        p = jnp.exp(l_ref[c] - m_wide)
        psum = psum + p
        vals_t = vbuf[slot, :, c * tk:(c + 1) * tk].astype(_MXU_DTYPE)
        acc = acc + lax.dot_general(p.astype(_MXU_DTYPE), vals_t, nt_dims,
                                    preferred_element_type=_F32)
    o = acc / jnp.sum(psum, axis=1, keepdims=True)
    blocks = []
    for j in range(N_HEADS // 2):
        blocks.append(jnp.where(lo_half, o[(2 * j) * tq:(2 * j + 1) * tq, :],
                                o[(2 * j + 1) * tq:(2 * j + 2) * tq, :]))
    out_ref[0] = jnp.concatenate(blocks, axis=1) * zbs_ref[0]


def _attn_sample(page_table, qs, qi, wi, zbs, k_new, v_new, ki_new, cache_kt, cache_vt, cache_it,
                 *, tk):
    nb, npages = page_table.shape
    tq = qs.shape[0] // nb
    past = npages * PAGE_SIZE
    nchunks = -(-(past + PAGE_SIZE) // tk)
    keys = nchunks * tk
    k_sel = min(TOP_K_MAX, (past + tq) // 4)
    group = LANES // tq
    assert LANES % tq == 0 and nb % group == 0
    three = lambda a: a.reshape(nb, tq, a.shape[-1])
    blk = lambda w: pl.BlockSpec((1, tq, w), lambda b, pt: (b, 0, 0))
    hbm = pl.BlockSpec(memory_space=pl.ANY)
    params = pltpu.CompilerParams(dimension_semantics=("arbitrary",),
                                  vmem_limit_bytes=VMEM_LIMIT_BYTES)
    pt_flat = page_table.reshape(-1)
    s_shape = jax.ShapeDtypeStruct((nb // group, nchunks, LANES, tk), _F32)
    s_blk = pl.BlockSpec((1, nchunks, tq, tk), lambda b, pt: (b // group, 0, b % group, 0))

    scores = pl.pallas_call(
        functools.partial(_sample_scores_kernel, npages=npages, tk=tk, nchunks=nchunks),
        grid_spec=pltpu.PrefetchScalarGridSpec(
            num_scalar_prefetch=1, grid=(nb,),
            in_specs=[blk(N_IDX_HEADS * D_IDX), blk(LANES), blk(D_IDX), hbm],
            out_specs=s_blk,
            scratch_shapes=[pltpu.VMEM((2, D_IDX, keys), cache_it.dtype),
                            pltpu.SemaphoreType.DMA((1, 2))]),
        out_shape=s_shape, compiler_params=params, name="sample_scores",
    )(pt_flat, three(qi), three(wi), three(ki_new), cache_it)

    n_adm = past + 1 + (jnp.arange(LANES) % tq)
    whole = pl.BlockSpec((1, nchunks, LANES, tk), lambda g: (g, 0, 0, 0))
    col = pl.BlockSpec((LANES, 1), lambda g: (0, 0))
    kept, thr = pl.pallas_call(
        functools.partial(_sample_threshold_kernel, tk=tk, nchunks=nchunks, k_sel=k_sel),
        grid=(nb // group,),
        in_specs=[whole, col, col],
        out_specs=(whole, pl.BlockSpec((LANES, 1), lambda g: (g, 0))),
        out_shape=(s_shape, jax.ShapeDtypeStruct((nb * tq, 1), _F32)),
        compiler_params=params, name="sample_threshold",
    )(scores, _seed_fraction(n_adm, k_sel).reshape(LANES, 1),
      n_adm.astype(_F32).reshape(LANES, 1))

    out = pl.pallas_call(
        functools.partial(_sample_attend_kernel, npages=npages, tk=tk, nchunks=nchunks),
        grid_spec=pltpu.PrefetchScalarGridSpec(
            num_scalar_prefetch=1, grid=(nb,),
            in_specs=[blk(D_B), blk(D_B), blk(KV_W), blk(KV_W), s_blk,
                      pl.BlockSpec((tq, 1), lambda b, pt: (b, 0)), hbm, hbm],
            out_specs=blk(D_B),
            scratch_shapes=[pltpu.VMEM((2, KV_W, keys), cache_kt.dtype),
                            pltpu.VMEM((2, KV_W, keys), cache_vt.dtype),
                            pltpu.SemaphoreType.DMA((2, 2)),
                            pltpu.VMEM((nchunks, N_HEADS * tq, tk), _F32)]),
        out_shape=jax.ShapeDtypeStruct((nb, tq, D_B), _F32),
        compiler_params=params, name="sample_attend",
    )(pt_flat, three(qs), three(zbs), three(k_new), three(v_new), kept, thr, cache_kt, cache_vt)
    return out.reshape(nb * tq, D_B)


def _outproj_kernel(yag_ref, ybg_ref, x_ref, w_ref, g_ref, out_ref):
    y = (jnp.dot(yag_ref[...], w_ref[:D_A, :], preferred_element_type=_F32)
         + jnp.dot(ybg_ref[...].astype(_MXU_DTYPE), w_ref[D_A:, :], preferred_element_type=_F32))
    ms = jnp.mean(y * y, axis=-1, keepdims=True)
    out_ref[...] = x_ref[...] + y * lax.rsqrt(ms + EPS) * g_ref[...]


def _outproj(yag, ybg, x2d, w_out_p, g_post, *, tm):
    r = x2d.shape[0]
    row = lambda w: pl.BlockSpec((tm, w), lambda i: (i, 0))
    const = lambda shape: pl.BlockSpec(shape, lambda i: (0,) * len(shape))
    return pl.pallas_call(
        _outproj_kernel,
        grid=(r // tm,),
        in_specs=[row(D_A), row(D_B), row(D_MODEL), const((D_A + D_B, D_MODEL)),
                  const((1, D_MODEL))],
        out_specs=row(D_MODEL),
        out_shape=jax.ShapeDtypeStruct((r, D_MODEL), _F32),
        compiler_params=pltpu.CompilerParams(
            dimension_semantics=("arbitrary",), vmem_limit_bytes=VMEM_LIMIT_BYTES),
        name="outproj",
    )(yag, ybg, x2d, w_out_p, g_post)


def _pad_in_weights(w):
    perm = _head_perm()
    nat = np.cumsum([0, D_A, D_A, D_A, D_B, KV_W, KV_W, D_B, N_IDX_HEADS * D_IDX, D_IDX,
                     N_IDX_HEADS])
    seg = lambda n: w[:, nat[n]:nat[n + 1]]
    pad = jnp.zeros((w.shape[0], C_END - C_WI - N_IDX_HEADS), w.dtype)
    return jnp.concatenate(
        [seg(0), seg(1), seg(2), seg(3)[:, perm], seg(4), seg(5), seg(6)[:, perm], seg(7),
         seg(8), seg(8), seg(9), pad], axis=1).astype(_MXU_DTYPE)


def _layer(x_prompt, x_sample, cache_k, cache_v, cache_idx_k, page_table, g_pre, w_in, ln_v_g,
           ln_v_b, w_s, b_s, w_out, g_post):
    nb, t, _ = x_prompt.shape
    nd, td, _ = x_sample.shape
    npages = page_table.shape[1]
    past = npages * PAGE_SIZE
    tm = 1024
    tq = 128
    tk = 1024
    tk_sample = 512

    w_pad = _pad_in_weights(w_in)
    perm = _head_perm()
    w_out_p = jnp.concatenate([w_out[:D_A], w_out[D_A:][perm]], axis=0).astype(_MXU_DTYPE)
    gpre = g_pre.reshape(1, D_MODEL)
    gpost = g_post.reshape(1, D_MODEL)
    lng = ln_v_g.reshape(1, D_A)
    lnb = ln_v_b.reshape(1, D_A)
    tril = jnp.tril(jnp.ones((CHUNK, CHUNK), dtype=bool))
    ws_tril = jnp.where(tril[None], w_s, jnp.zeros_like(w_s))

    xp = x_prompt.reshape(nb * t, D_MODEL)
    tpb = t // tm
    wmix_p = ws_tril.astype(_MXU_DTYPE)
    bmix_p = jnp.repeat(b_s.T, A_GROUP_DIM, axis=1)
    outs = _inproj(xp, _rope_tables(jnp.arange(t, dtype=jnp.int32)), lambda i: (i % tpb, 0),
                   gpre, w_pad, lng, lnb, wmix_p, bmix_p, tm=tm, state_rows=CHUNK,
                   state_shape=(nb, CHUNK, D_A), state_index=lambda i: (i // tpb, 0, 0), seqs=nb)
    yag, zbs, qs, qi, k_t, v_t, ki_t, wi, kbf, vt3, kibf, vstate_p = outs
    ybg = _attn_prompt(qs, qi, wi, zbs, kibf, kbf, vt3, nb=nb, t=t, tq=tq, tk=tk)
    y_prompt = _outproj(yag, ybg, xp, w_out_p, gpost, tm=tm).reshape(nb, t, D_MODEL)

    xs = x_sample.reshape(nd * td, D_MODEL)
    reps = CHUNK // td
    eye = jnp.eye(reps, dtype=w_s.dtype)
    wmix_s = jnp.stack([jnp.kron(eye, ws_tril[g, :td, :td]) for g in range(N_A_GROUPS)]
                       ).astype(_MXU_DTYPE)
    bmix_s = jnp.repeat(jnp.tile(b_s[:, :td], (1, reps)).T, A_GROUP_DIM, axis=1)
    tm_s = min(tm, nd * td)
    pos_s = past + (jnp.arange(tm_s, dtype=jnp.int32) % td)
    outs = _inproj(xs, _rope_tables(pos_s), lambda i: (0, 0), gpre, w_pad, lng, lnb, wmix_s,
                   bmix_s, tm=tm_s, state_rows=tm_s, state_shape=(nd * td, D_A),
                   state_index=lambda i: (i, 0))
    yag_s, zbs_s, qs_s, qi_s, k32_s, v32_s, ki32_s, wi_s, _, _, _, vstate_s = outs
    pool = cache_k.shape[0]
    ybg_s = _attn_sample(page_table, qs_s, qi_s, wi_s, zbs_s, k32_s, v32_s, ki32_s,
                         jnp.transpose(cache_k, (0, 2, 3, 1)).reshape(pool, KV_W, PAGE_SIZE),
                         jnp.transpose(cache_v, (0, 2, 3, 1)).reshape(pool, KV_W, PAGE_SIZE),
                         jnp.transpose(cache_idx_k, (0, 2, 1)), tk=tk_sample)
    y_sample = _outproj(yag_s, ybg_s, xs, w_out_p, gpost, tm=tm_s).reshape(nd, td, D_MODEL)

    return (y_prompt, y_sample,
            jnp.transpose(k_t.reshape(nb, N_KV_HEADS, HEAD_DIM, t), (0, 3, 1, 2)),
            jnp.transpose(v_t.reshape(nb, N_KV_HEADS, HEAD_DIM, t), (0, 3, 1, 2)),
            jnp.transpose(ki_t, (0, 2, 1)), vstate_p,
            k32_s.reshape(nd, td, N_KV_HEADS, HEAD_DIM), v32_s.reshape(nd, td, N_KV_HEADS, HEAD_DIM),
            ki32_s.reshape(nd, td, D_IDX), vstate_s.reshape(nd, td, D_A))


def kernel(x_prompt, x_sample, cache_k, cache_v, cache_idx_k, page_table, g_pre, w_in, ln_v_g,
           ln_v_b, w_s, b_s, w_out, g_post):
    xp, xs = x_prompt, x_sample
    per_layer = []
    for l in range(g_pre.shape[0]):
        outs = _layer(xp, xs, cache_k[l], cache_v[l], cache_idx_k[l], page_table, g_pre[l],
                      w_in[l], ln_v_g[l], ln_v_b[l], w_s[l], b_s[l], w_out[l], g_post[l])
        xp, xs = outs[0], outs[1]
        per_layer.append(outs[2:])
    stacked = tuple(jnp.stack(leaves, 0) for leaves in zip(*per_layer))
    return (xp, xs) + stacked
```

```python
import functools

import numpy as np
import jax
import jax.numpy as jnp
from jax import lax
from jax.scipy.special import ndtri
from jax.experimental import pallas as pl
from jax.experimental.pallas import tpu as pltpu

_F32 = jnp.float32
_MXU_DTYPE = jnp.bfloat16

D_MODEL = 1024
D_A = 512
D_B = 512
CHUNK = 128
N_A_GROUPS = 4
A_GROUP_DIM = D_A // N_A_GROUPS
HEAD_DIM = 64
N_HEADS = 8
N_KV_HEADS = 2
KV_W = N_KV_HEADS * HEAD_DIM
ROT_DIM = HEAD_DIM // 4
N_IDX_HEADS = 8
D_IDX = 64
TOP_K_MAX = 256
ROPE_THETA = 500000.0
EPS = 1e-6
PAGE_SIZE = 128

LANES = 128
VMEM_LIMIT_BYTES = 56 * 1024 * 1024

C_U, C_V, C_ZA, C_Q, C_K, C_VV, C_ZB, C_QI, C_KI, C_WI, C_END = (
    0, 512, 1024, 1536, 2048, 2176, 2304, 2816, 3328, 3456, 3584)

_FLT_MAX = float(np.finfo(np.float32).max)
_NEG_INF = float("-inf")


def _head_perm():
    c = np.arange(N_HEADS * HEAD_DIM)
    j = c // LANES
    half = (c % LANES) // HEAD_DIM
    d = c % HEAD_DIM
    return (j + 4 * half) * HEAD_DIM + d


def _rope_tables(pos):
    half = ROT_DIM // 2
    inv = jnp.power(jnp.float32(ROPE_THETA), -jnp.arange(half, dtype=_F32) * 2.0 / ROT_DIM)
    ang = pos.astype(_F32)[:, None] * inv[None, :]
    cos = jnp.cos(ang)
    sin = jnp.sin(ang)
    l64 = np.arange(LANES) % HEAD_DIM
    fidx = l64 % half
    in_rot = (l64 < ROT_DIM)[None, :]
    first = (l64 < half)[None, :]
    second = ((l64 >= half) & (l64 < ROT_DIM))[None, :]
    cos_t = jnp.where(in_rot, cos[:, fidx], 1.0)
    sin_a = jnp.where(first, -sin[:, fidx], 0.0)
    sin_b = jnp.where(second, sin[:, fidx], 0.0)
    return cos_t, sin_a, sin_b


def _silu(z):
    return z * (1.0 / (1.0 + jnp.exp(-z)))


def _inproj_kernel(x_ref, cos_ref, sa_ref, sb_ref, gpre_ref, win_ref, lng_ref, lnb_ref,
                   wmix_ref, bmix_ref,
                   yag_ref, zbs_ref, qs_ref, qi_ref, k32_ref, v32_ref, ki32_ref, wi_ref,
                   kbf_ref, vt_ref, kibf_ref, vstate_ref, *, tm, state_rows, kv_transposed):
    x = x_ref[...]
    ms = jnp.mean(x * x, axis=-1, keepdims=True)
    h = (x * lax.rsqrt(ms + EPS) * gpre_ref[...]).astype(_MXU_DTYPE)

    def proj(a, b):
        return jnp.dot(h, win_ref[:, a:b], preferred_element_type=_F32)

    cos_t = cos_ref[...]
    sin_a = sa_ref[...]
    sin_b = sb_ref[...]

    def rope(p):
        blocks = []
        for j in range(p.shape[1] // LANES):
            pj = p[:, j * LANES:(j + 1) * LANES]
            blocks.append(pj * cos_t
                          + pltpu.roll(pj, LANES - ROT_DIM // 2, 1) * sin_a
                          + pltpu.roll(pj, ROT_DIM // 2, 1) * sin_b)
        return blocks[0] if len(blocks) == 1 else jnp.concatenate(blocks, axis=1)

    u = proj(C_U, C_V)
    v = proj(C_V, C_ZA)
    mu = jnp.mean(v, axis=-1, keepdims=True)
    vc = v - mu
    var = jnp.mean(vc * vc, axis=-1, keepdims=True)
    vn = vc * lax.rsqrt(var + EPS) * lng_ref[...] + lnb_ref[...]
    vstate_ref[...] = vn[tm - state_rows:, :].reshape(vstate_ref.shape)
    vnb = vn.astype(_MXU_DTYPE)
    bmix = bmix_ref[...]
    rows = []
    for c in range(tm // CHUNK):
        cols = []
        for g in range(N_A_GROUPS):
            blk = vnb[c * CHUNK:(c + 1) * CHUNK, g * A_GROUP_DIM:(g + 1) * A_GROUP_DIM]
            cols.append(jnp.dot(wmix_ref[g], blk, preferred_element_type=_F32)
                        + bmix[:, g * A_GROUP_DIM:(g + 1) * A_GROUP_DIM])
        rows.append(jnp.concatenate(cols, axis=1))
    mixed = jnp.concatenate(rows, axis=0)
    za = proj(C_ZA, C_Q)
    yag_ref[...] = ((u * mixed) * _silu(za)).astype(yag_ref.dtype)

    q = rope(proj(C_Q, C_K))
    qs_ref[...] = (q * (HEAD_DIM ** -0.5)).astype(qs_ref.dtype)
    k = rope(proj(C_K, C_VV))
    kbf_ref[...] = k.astype(kbf_ref.dtype)
    vv = proj(C_VV, C_ZB)
    vv_t = vv.T
    vt_ref[0] = vv_t.astype(vt_ref.dtype)
    if kv_transposed:
        k32_ref[0] = k.T
        v32_ref[0] = vv_t
    else:
        k32_ref[...] = k
        v32_ref[...] = vv
    zb = proj(C_ZB, C_QI)
    zbs_ref[...] = _silu(zb)
    qi = rope(proj(C_QI, C_KI))
    qi_ref[...] = qi.astype(qi_ref.dtype)
    ki = rope(proj(C_KI, C_WI))
    if kv_transposed:
        ki32_ref[0] = ki.T[:D_IDX, :]
    else:
        ki32_ref[...] = ki[:, :D_IDX]
    kibf_ref[...] = ki.astype(kibf_ref.dtype)
    wi_ref[...] = proj(C_WI, C_END) * (N_IDX_HEADS ** -0.5)


def _inproj(x2d, tables, table_index, gpre, w_pad, lng, lnb, wmix, bmix, *, tm, state_rows,
            state_shape, state_index, seqs=None):
    r = x2d.shape[0]
    nt = r // tm
    if seqs is None:
        kv = lambda w: (jax.ShapeDtypeStruct((r, w), _F32), pl.BlockSpec((tm, w), lambda i: (i, 0)))
    else:
        tps = nt // seqs
        kv = lambda w: (jax.ShapeDtypeStruct((seqs, w, r // seqs), _F32),
                        pl.BlockSpec((1, w, tm), lambda i: (i // tps, 0, i % tps)))
    cos_t, sin_a, sin_b = tables
    row = lambda w: pl.BlockSpec((tm, w), lambda i: (i, 0))
    const = lambda shape: pl.BlockSpec(shape, lambda i: (0,) * len(shape))
    tab = pl.BlockSpec((tm, LANES), table_index)
    out_shapes = (
        jax.ShapeDtypeStruct((r, D_A), _MXU_DTYPE),
        jax.ShapeDtypeStruct((r, D_B), _F32),
        jax.ShapeDtypeStruct((r, D_B), _MXU_DTYPE),
        jax.ShapeDtypeStruct((r, N_IDX_HEADS * D_IDX), _MXU_DTYPE),
        kv(KV_W)[0],
        kv(KV_W)[0],
        kv(D_IDX)[0],
        jax.ShapeDtypeStruct((r, LANES), _F32),
        jax.ShapeDtypeStruct((r, KV_W), _MXU_DTYPE),
        jax.ShapeDtypeStruct((nt, KV_W, tm), _MXU_DTYPE),
        jax.ShapeDtypeStruct((r, 2 * D_IDX), _MXU_DTYPE),
        jax.ShapeDtypeStruct(state_shape, _F32),
    )
    out_specs = (
        row(D_A), row(D_B), row(D_B), row(N_IDX_HEADS * D_IDX), kv(KV_W)[1], kv(KV_W)[1],
        kv(D_IDX)[1], row(LANES), row(KV_W),
        pl.BlockSpec((1, KV_W, tm), lambda i: (i, 0, 0)),
        row(2 * D_IDX),
        pl.BlockSpec((1,) * (len(state_shape) - 2) + (state_rows, D_A), state_index),
    )
    return pl.pallas_call(
        functools.partial(_inproj_kernel, tm=tm, state_rows=state_rows,
                          kv_transposed=seqs is not None),
        grid=(nt,),
        in_specs=[row(D_MODEL), tab, tab, tab, const((1, D_MODEL)), const((D_MODEL, C_END)),
                  const((1, D_A)), const((1, D_A)), const((N_A_GROUPS, CHUNK, CHUNK)),
                  const((CHUNK, D_A))],
        out_specs=out_specs,
        out_shape=out_shapes,
        compiler_params=pltpu.CompilerParams(
            dimension_semantics=("arbitrary",), vmem_limit_bytes=VMEM_LIMIT_BYTES),
        name="inproj",
    )(x2d, cos_t, sin_a, sin_b, gpre, w_pad, lng, lnb, wmix, bmix)


def _f32_to_key(x):
    b = lax.bitcast_convert_type(x, jnp.int32)
    return jnp.where(b >= 0, b, b ^ jnp.int32(0x7FFFFFFF))


def _key_to_f32(k):
    return lax.bitcast_convert_type(jnp.where(k >= 0, k, k ^ jnp.int32(0x7FFFFFFF)), _F32)


N_INTERP_PASSES = 13
HUGE_SUM = 1e30
PAGE_LOOP_UNROLL = 8


def _topk_threshold(count_ge, max_below, smin, smax, n_adm, f0, k_sel):
    kf = jnp.float32(k_sel)
    log_k = jnp.log(kf - 0.5)
    lo0 = _f32_to_key(smin)
    hi0 = _f32_to_key(smax) + 1
    short = n_adm <= kf
    done0 = jnp.where(jnp.logical_or(short, hi0 <= lo0 + 1), 1.0, 0.0)
    g_hi0 = jnp.full(smin.shape, np.log(0.5), _F32) - log_k

    def all_done(done):
        return jnp.min(done) > 0.5

    def probe(carry, frac):
        it, lo, hi, clo, chi, glo, ghi, last, done = carry
        xlo = _key_to_f32(lo)
        xhi = _key_to_f32(hi)
        cand = jnp.clip(_f32_to_key(xlo + (xhi - xlo) * frac), lo + 1, hi - 1)
        cnt = count_ge(_key_to_f32(cand))
        g = jnp.log(jnp.maximum(cnt, 0.5)) - log_k
        active = done < 0.5
        new_lo = jnp.logical_and(active, cnt >= kf)
        new_hi = jnp.logical_and(active, cnt < kf)
        ghi = jnp.where(jnp.logical_and(new_lo, last == 1), ghi * 0.5, ghi)
        glo = jnp.where(jnp.logical_and(new_hi, last == 2), glo * 0.5, glo)
        lo = jnp.where(new_lo, cand, lo)
        clo = jnp.where(new_lo, cnt, clo)
        glo = jnp.where(new_lo, g, glo)
        hi = jnp.where(new_hi, cand, hi)
        chi = jnp.where(new_hi, cnt, chi)
        ghi = jnp.where(new_hi, g, ghi)
        last = jnp.where(new_lo, 1, jnp.where(new_hi, 2, last))
        done = jnp.where(jnp.logical_or(clo == kf, hi <= lo + 1), 1.0, done)
        return it + 1, lo, hi, clo, chi, glo, ghi, last, done

    def interp_body(_, carry):
        return probe(carry, carry[5] / (carry[5] - carry[6]))

    init = (jnp.int32(0), lo0, hi0, n_adm, jnp.zeros_like(n_adm),
            jnp.log(jnp.maximum(n_adm, 1.0)) - log_k, g_hi0, jnp.zeros(smin.shape, jnp.int32),
            done0)
    state = lax.fori_loop(1, N_INTERP_PASSES, interp_body, probe(init, f0))
    _, lo, hi, clo, chi, _, _, _, done = state

    def peel_cond(carry):
        return jnp.logical_not(all_done(carry[-1]))

    def peel_body(carry):
        lo, hi, clo, chi, done = carry
        top = max_below(_key_to_f32(hi))
        cnt = count_ge(top)
        active = done < 0.5
        hit = jnp.logical_and(active, cnt >= kf)
        miss = jnp.logical_and(active, cnt < kf)
        lo = jnp.where(hit, _f32_to_key(top), lo)
        clo = jnp.where(hit, cnt, clo)
        hi = jnp.where(miss, _f32_to_key(top), hi)
        chi = jnp.where(miss, cnt, chi)
        done = jnp.where(hit, 1.0, done)
        return lo, hi, clo, chi, done

    lo, _, clo, chi, _ = lax.while_loop(peel_cond, peel_body, (lo, hi, clo, chi, done))
    thr = jnp.where(short, -_FLT_MAX, _key_to_f32(lo))
    return thr, clo, chi, short


def _seed_fraction(n_adm, k_sel):
    n = n_adm.astype(_F32)
    z_k = ndtri(1.0 - jnp.minimum(k_sel / n, 0.999))
    z_n = ndtri(1.0 - 1.0 / (n + 1.0))
    return (0.5 + 0.5 * z_k / z_n).astype(_F32)


def _attn_prompt_kernel(qs_ref, qi_ref, wi_ref, zbs_ref, f0_ref, qin_ref, win_ref, ki_ref, k_ref,
                        vt_ref, out_ref,
                        s2_ref, q8_ref, qi8_ref, bias_ref, pt_ref, acc_ref, m_ref, l_ref,
                        scale_ref, range_ref, ties_ref, *, tq, tk, k_sel):
    i = pl.program_id(1)
    i_next = jnp.minimum(i + 1, pl.num_programs(1) - 1)
    ratio = tk // tq
    nchunks = lax.div(i + ratio, ratio)
    nchunks_next = lax.div(i_next + ratio, ratio)
    kf = jnp.float32(k_sel)
    cur = lax.rem(i, 2)
    s_ref = s2_ref.at[cur]
    s_next_ref = s2_ref.at[1 - cur]

    lane = lax.broadcasted_iota(jnp.int32, (tq, LANES), 1)
    lo_half = lane < HEAD_DIM
    def stack_heads(src, dst):
        for j in range(N_HEADS // 2):
            blk = src[:, j * LANES:(j + 1) * LANES].astype(_F32)
            dst[(2 * j) * tq:(2 * j + 1) * tq, :] = jnp.where(lo_half, blk, 0.0).astype(dst.dtype)
            dst[(2 * j + 1) * tq:(2 * j + 2) * tq, :] = jnp.where(lo_half, 0.0, blk).astype(dst.dtype)

    stack_heads(qs_ref, q8_ref)

    row = lax.broadcasted_iota(jnp.int32, (tk, tq), 0)
    col = lax.broadcasted_iota(jnp.int32, (tk, tq), 1)
    key_minus_query = row - col
    nt_dims = (((1,), (1,)), ((), ()))
    fold = lambda x: x.reshape(tk // 32, 32, tq)

    def score_chunk(c, carry, blk, dst_ref, w_t):
        mn, mx = carry
        k0 = pl.multiple_of(c * tk, tk)
        rt = lax.dot_general(ki_ref[pl.ds(k0, tk), :], qi8_ref[...], nt_dims,
                             preferred_element_type=_F32)
        s = None
        for hb in range(N_IDX_HEADS):
            t = jnp.maximum(rt[:, hb * tq:(hb + 1) * tq], 0.0) * w_t[hb:hb + 1, :]
            s = t if s is None else s + t
        dst_ref[c] = jnp.where(key_minus_query <= blk * tq - c * tk, s, _NEG_INF)
        return jnp.minimum(mn, fold(s).min(axis=0)), jnp.maximum(mx, fold(s).max(axis=0))

    no_range = (jnp.full((32, tq), jnp.inf, _F32), jnp.full((32, tq), -jnp.inf, _F32))

    @pl.when(i == 0)
    def _():
        stack_heads(qi_ref, qi8_ref)
        w_t = wi_ref[...].T
        mn, mx = lax.fori_loop(0, nchunks,
                               lambda c, carry: score_chunk(c, carry, i, s_ref, w_t), no_range)
        range_ref[cur, 0] = mn
        range_ref[cur, 1] = mx

    smin = jnp.min(range_ref[cur, 0], axis=0, keepdims=True)
    smax = jnp.max(range_ref[cur, 1], axis=0, keepdims=True)

    def count_ge(t):
        def body(c, acc):
            return acc + fold(jnp.where(s_ref[c] >= t, 1.0, 0.0)).sum(axis=0)
        acc = lax.fori_loop(0, nchunks, body, jnp.zeros((32, tq), _F32))
        return jnp.sum(acc, axis=0, keepdims=True)

    def max_below(x):
        def body(c, acc):
            s = s_ref[c]
            return jnp.maximum(acc, fold(jnp.where(s < x, s, _NEG_INF)).max(axis=0))
        acc = lax.fori_loop(0, nchunks, body, jnp.full((32, tq), -jnp.inf, _F32))
        return jnp.max(acc, axis=0, keepdims=True)

    n_adm = (i * tq + 1 + lax.broadcasted_iota(jnp.int32, (1, tq), 1)).astype(_F32)
    thr, clo, chi, short = _topk_threshold(count_ge, max_below, smin, smax, n_adm, f0_ref[0],
                                           k_sel)

    need = jnp.logical_and(clo > kf, jnp.logical_not(short))

    @pl.when(jnp.max(jnp.where(need, 1.0, 0.0)) > 0.5)
    def _():
        keep = jnp.where(need, kf - chi, _FLT_MAX)
        tied = lambda s: jnp.where(s == thr, 1.0, 0.0)
        col_sum = lambda x: jnp.sum(fold(x).sum(axis=0), axis=0, keepdims=True)

        def count_ties(c, carry):
            ties_ref[c] = jnp.broadcast_to(col_sum(tied(s_ref[c])), ties_ref.shape[1:])
            return carry

        lax.fori_loop(0, nchunks, count_ties, 0)

        def drop(c, seen):
            here = ties_ref[c][0:1, :]
            s = s_ref[c]
            s = jnp.where(jnp.where(s == thr, seen - keep, -1.0) >= 0.0, _NEG_INF, s)
            s_ref[c] = s
            runs_out = jnp.logical_and(seen < keep, seen + here > keep)

            @pl.when(jnp.max(jnp.where(runs_out, 1.0, 0.0)) > 0.5)
            def _():
                rk = lax.broadcasted_iota(jnp.int32, (tk, tk), 0)
                ck = lax.broadcasted_iota(jnp.int32, (tk, tk), 1)
                earlier = jnp.where(rk > ck, 1.0, 0.0).astype(_MXU_DTYPE)
                tm = tied(s)
                before = jnp.dot(earlier, tm.astype(_MXU_DTYPE),
                                 preferred_element_type=_F32) + seen
                s_ref[c] = jnp.where(jnp.where(tm > 0.5, before, -1.0) >= keep, _NEG_INF, s)

            return seen + here

        lax.fori_loop(0, nchunks, drop, jnp.zeros((1, tq), _F32))

    def chunk_probs(c, running_max):
        bias_ref[...] = jnp.where(s_ref[c] >= thr, 0.0, _NEG_INF)
        k_chunk = k_ref[pl.ds(pl.multiple_of(c * tk, tk), tk), :]
        slot = lax.rem(c, 2)
        for j in range(N_HEADS // 2):
            l2 = lax.dot_general(k_chunk, q8_ref[(2 * j) * tq:(2 * j + 2) * tq, :], nt_dims,
                                 preferred_element_type=_F32)
            for half in range(2):
                hb = 2 * j + half
                lh = l2[:, half * tq:(half + 1) * tq] + bias_ref[...]
                m_old = m_ref[hb:hb + 1, :]
                if running_max:
                    m_new = jnp.maximum(m_old,
                                        jnp.max(fold(lh).max(axis=0), axis=0, keepdims=True))
                    alpha = jnp.exp(m_old - m_new)
                    m_ref[hb:hb + 1, :] = m_new
                    scale_ref[0:1, hb * tq:(hb + 1) * tq] = alpha
                    p = jnp.exp(lh - m_new)
                    l_ref[hb:hb + 1, :] = (alpha * l_ref[hb:hb + 1, :]
                                           + jnp.sum(fold(p).sum(axis=0), axis=0, keepdims=True))
                else:
                    p = jnp.exp(lh - m_old)
                    l_ref[hb:hb + 1, :] = (l_ref[hb:hb + 1, :]
                                           + jnp.sum(fold(p).sum(axis=0), axis=0, keepdims=True))
                pt_ref[slot, :, hb * tq:(hb + 1) * tq] = p.astype(pt_ref.dtype)

    def values(c):
        return jnp.dot(vt_ref[c], pt_ref[lax.rem(c, 2)], preferred_element_type=_F32)

    def reset():
        m_ref[...] = jnp.full(m_ref.shape, -_FLT_MAX, _F32)
        l_ref[...] = jnp.zeros(l_ref.shape, _F32)
        acc_ref[...] = jnp.zeros(acc_ref.shape, _F32)

    stack_heads(qin_ref, qi8_ref)
    w_t_next = win_ref[...].T
    score_next = lambda c, carry: score_chunk(c, carry, i_next, s_next_ref, w_t_next)

    reset()
    chunk_probs(0, running_max=True)
    next_range = score_next(0, no_range)

    def attend(c, next_range):
        acc_ref[...] = acc_ref[...] + values(c - 1)
        chunk_probs(c, running_max=False)
        return score_next(c, next_range)

    mn, mx = lax.fori_loop(1, nchunks, attend, next_range)
    acc_ref[...] = acc_ref[...] + values(nchunks - 1)
    range_ref[1 - cur, 0] = mn
    range_ref[1 - cur, 1] = mx

    @pl.when(nchunks_next > nchunks)
    def _():
        mn, mx = score_next(nchunks, (range_ref[1 - cur, 0], range_ref[1 - cur, 1]))
        range_ref[1 - cur, 0] = mn
        range_ref[1 - cur, 1] = mx

    sane = jnp.logical_and(jnp.max(l_ref[...]) < HUGE_SUM,
                           jnp.max(jnp.abs(acc_ref[...])) < HUGE_SUM)
    sane = jnp.logical_and(sane, jnp.min(m_ref[...]) > -_FLT_MAX)

    @pl.when(jnp.logical_not(sane))
    def _():
        reset()

        def attend_rescaling(c, carry):
            chunk_probs(c, running_max=True)
            acc_ref[...] = acc_ref[...] * scale_ref[0:1, :] + values(c)
            return carry

        lax.fori_loop(0, nchunks, attend_rescaling, 0)

    l_all = jnp.concatenate([l_ref[hb:hb + 1, :] for hb in range(N_HEADS)], axis=1)
    o_t = acc_ref[...] / l_all
    blocks = []
    for j in range(N_HEADS // 2):
        mj = jnp.concatenate(
            [o_t[:HEAD_DIM, (2 * j) * tq:(2 * j + 1) * tq],
             o_t[HEAD_DIM:, (2 * j + 1) * tq:(2 * j + 2) * tq]], axis=0)
        blocks.append(mj.T)
    yb = jnp.concatenate(blocks, axis=1)
    out_ref[...] = (yb * zbs_ref[...]).astype(out_ref.dtype)


def _attn_prompt(qs, qi, wi, zbs, kibf, kbf, vt3, *, nb, t, tq, tk):
    nq = t // tq
    k_sel = min(TOP_K_MAX, t // 4)
    f0 = _seed_fraction(jnp.arange(1, t + 1), k_sel).reshape(nq, 1, tq)
    blk = lambda w: pl.BlockSpec((tq, w), lambda b, i: (b * nq + i, 0))
    nxt = lambda w: pl.BlockSpec((tq, w), lambda b, i: (b * nq + jnp.minimum(i + 1, nq - 1), 0))
    res = lambda w: pl.BlockSpec((t, w), lambda b, i: (b, 0))
    stat = pltpu.VMEM((N_HEADS, tq), _F32)
    kernel = functools.partial(_attn_prompt_kernel, tq=tq, tk=tk, k_sel=k_sel)
    return pl.pallas_call(
        kernel,
        grid=(nb, nq),
        in_specs=[blk(D_B), blk(N_IDX_HEADS * D_IDX), blk(LANES), blk(D_B),
                  pl.BlockSpec((1, 1, tq), lambda b, i: (i, 0, 0)),
                  nxt(N_IDX_HEADS * D_IDX), nxt(LANES),
                  res(2 * D_IDX), res(KV_W),
                  pl.BlockSpec((t // tk, KV_W, tk), lambda b, i: (b, 0, 0))],
        out_specs=blk(D_B),
        out_shape=jax.ShapeDtypeStruct((nb * t, D_B), _MXU_DTYPE),
        scratch_shapes=[
            pltpu.VMEM((2, t // tk, tk, tq), _F32),
            pltpu.VMEM((N_HEADS * tq, LANES), _MXU_DTYPE),
            pltpu.VMEM((N_IDX_HEADS * tq, LANES), _MXU_DTYPE),
            pltpu.VMEM((tk, tq), _F32),
            pltpu.VMEM((2, tk, N_HEADS * tq), _MXU_DTYPE),
            pltpu.VMEM((KV_W, N_HEADS * tq), _F32),
            stat, stat,
            pltpu.VMEM((1, N_HEADS * tq), _F32),
            pltpu.VMEM((2, 2, 32, tq), _F32),
            pltpu.VMEM((t // tk, 8, tq), _F32),
        ],
        compiler_params=pltpu.CompilerParams(
            dimension_semantics=("arbitrary", "arbitrary"), vmem_limit_bytes=VMEM_LIMIT_BYTES),
        name="attn_prompt",
    )(qs, qi, wi, zbs, f0, qi, wi, kibf, kbf, vt3)


def _stream_pages(pt_ref, b, nb, npages, sources, buffers, sems):
    slot = lax.rem(b, 2)

    def copies(bb, sl, p):
        phys = pt_ref[bb * npages + p]
        dst = pl.ds(pl.multiple_of(p * PAGE_SIZE, PAGE_SIZE), PAGE_SIZE)
        return [pltpu.make_async_copy(src.at[phys], buf.at[sl, :, dst], sems.at[i, sl])
                for i, (src, buf) in enumerate(zip(sources, buffers))]

    def start(bb, sl):
        def body(pair, carry):
            for priority in range(2):
                for cp in copies(bb, sl, 2 * pair + priority):
                    cp.start(priority=priority)
            return carry
        assert npages % 2 == 0
        lax.fori_loop(0, npages // 2, body, 0, unroll=PAGE_LOOP_UNROLL // 2)

    def wait(bb, sl):
        def body(p, carry):
            for cp in copies(bb, sl, p):
                cp.wait()
            return carry
        lax.fori_loop(0, npages, body, 0, unroll=PAGE_LOOP_UNROLL)

    @pl.when(b == 0)
    def _():
        past = npages * PAGE_SIZE
        for buf in buffers:
            for sl in range(2):
                buf[sl, :, past:] = jnp.zeros((buf.shape[1], buf.shape[2] - past), buf.dtype)
        start(0, 0)

    @pl.when(b + 1 < nb)
    def _():
        start(b + 1, 1 - slot)

    wait(b, slot)
    return slot


def _put_new_keys(buf, slot, past, rows):
    tq, feat = rows.shape
    tile = jnp.concatenate([rows, jnp.zeros((PAGE_SIZE - tq, feat), _F32)], axis=0)
    if feat < LANES:
        tile = jnp.concatenate([tile, jnp.zeros((PAGE_SIZE, LANES - feat), _F32)], axis=1)
    buf[slot, :, past:past + PAGE_SIZE] = tile.T[:feat, :].astype(buf.dtype)


def _sample_scores_kernel(pt_ref, qi_ref, wi_ref, kin_ref, ci_hbm, s_ref, ibuf, sems,
                          *, npages, tk, nchunks):
    b = pl.program_id(0)
    past = npages * PAGE_SIZE
    tq = qi_ref.shape[1]
    slot = _stream_pages(pt_ref, b, pl.num_programs(0), npages, [ci_hbm], [ibuf], sems)
    _put_new_keys(ibuf, slot, past, kin_ref[0])

    qi = qi_ref[0].astype(_F32)
    wi = wi_ref[0]
    qi8 = jnp.concatenate([qi[:, h * D_IDX:(h + 1) * D_IDX] for h in range(N_IDX_HEADS)],
                          axis=0).astype(_MXU_DTYPE)
    w_wide = [jnp.broadcast_to(wi[:, h:h + 1], (tq, tk)) for h in range(N_IDX_HEADS)]
    key_minus_query = (lax.broadcasted_iota(jnp.int32, (tq, tk), 1)
                       - lax.broadcasted_iota(jnp.int32, (tq, tk), 0))
    for c in range(nchunks):
        keys_t = ibuf[slot, :, c * tk:(c + 1) * tk].astype(_MXU_DTYPE)
        r = jnp.dot(qi8, keys_t, preferred_element_type=_F32)
        s = None
        for h in range(N_IDX_HEADS):
            t = jnp.maximum(r[h * tq:(h + 1) * tq, :], 0.0) * w_wide[h]
            s = t if s is None else s + t
        s_ref[0, c] = jnp.where(key_minus_query <= past - c * tk, s, _NEG_INF)


def _sample_threshold_kernel(s_ref, f0_ref, nadm_ref, out_ref, thr_ref, *, tk, nchunks, k_sel):
    rows = s_ref.shape[2]
    kf = jnp.float32(k_sel)
    groups = tk // LANES
    tile = lambda col: jnp.broadcast_to(col, (rows, tk))

    def fold_with(x, op):
        out = x[:, :LANES]
        for g in range(1, groups):
            out = op(out, x[:, g * LANES:(g + 1) * LANES])
        return out

    def over_chunks(fn, op, init):
        def body(c, acc):
            return op(acc, fold_with(fn(s_ref[0, c]), op))
        return lax.fori_loop(0, nchunks, body, jnp.full((rows, LANES), init, _F32))

    def count_ge(t):
        t_wide = tile(t)
        acc = over_chunks(lambda s: jnp.where(s >= t_wide, 1.0, 0.0), jnp.add, 0.0)
        return jnp.sum(acc, axis=1, keepdims=True)

    def max_below(x):
        x_wide = tile(x)
        acc = over_chunks(lambda s: jnp.where(s < x_wide, s, _NEG_INF), jnp.maximum, -jnp.inf)
        return jnp.max(acc, axis=1, keepdims=True)

    finite_min = over_chunks(lambda s: jnp.where(s > _NEG_INF, s, jnp.inf), jnp.minimum, jnp.inf)
    finite_max = over_chunks(lambda s: s, jnp.maximum, -jnp.inf)
    smin = jnp.min(finite_min, axis=1, keepdims=True)
    smax = jnp.max(finite_max, axis=1, keepdims=True)
    thr, clo, chi, short = _topk_threshold(count_ge, max_below, smin, smax, nadm_ref[...],
                                           f0_ref[...], k_sel)
    thr_ref[...] = thr
    need = jnp.logical_and(clo > kf, jnp.logical_not(short))
    any_ties = jnp.max(jnp.where(need, 1.0, 0.0)) > 0.5

    @pl.when(jnp.logical_not(any_ties))
    def _():
        out_ref[...] = s_ref[...]

    @pl.when(any_ties)
    def _():
        keep_wide = tile(jnp.where(need, kf - chi, _FLT_MAX))
        thr_wide = tile(thr)
        rk = lax.broadcasted_iota(jnp.int32, (tk, tk), 0)
        ck = lax.broadcasted_iota(jnp.int32, (tk, tk), 1)
        earlier = jnp.where(rk < ck, 1.0, 0.0).astype(_MXU_DTYPE)

        def drop(c, seen):
            s = s_ref[0, c]
            tie = s == thr_wide
            tm = jnp.where(tie, 1.0, 0.0)
            before = (jnp.dot(tm.astype(_MXU_DTYPE), earlier, preferred_element_type=_F32)
                      + tile(seen))
            out_ref[0, c] = jnp.where(jnp.where(tie, before, -1.0) >= keep_wide, _NEG_INF, s)
            return seen + jnp.sum(tm, axis=1, keepdims=True)

        lax.fori_loop(0, nchunks, drop, jnp.zeros((rows, 1), _F32))


def _sample_attend_kernel(pt_ref, qs_ref, zbs_ref, kn_ref, vn_ref, s_ref, thr_ref,
                          ck_hbm, cv_hbm, out_ref, kbuf, vbuf, sems, l_ref,
                          *, npages, tk, nchunks):
    b = pl.program_id(0)
    past = npages * PAGE_SIZE
    tq = qs_ref.shape[1]
    slot = _stream_pages(pt_ref, b, pl.num_programs(0), npages, [ck_hbm, cv_hbm], [kbuf, vbuf],
                         sems)
    _put_new_keys(kbuf, slot, past, kn_ref[0])
    _put_new_keys(vbuf, slot, past, vn_ref[0])

    qs = qs_ref[0].astype(_F32)
    lane = lax.broadcasted_iota(jnp.int32, (tq, LANES), 1)
    lo_half = lane < HEAD_DIM
    q8_rows = []
    for j in range(N_HEADS // 2):
        blk = qs[:, j * LANES:(j + 1) * LANES]
        q8_rows.append(jnp.where(lo_half, blk, 0.0))
        q8_rows.append(jnp.where(lo_half, 0.0, blk))
    q8 = jnp.concatenate(q8_rows, axis=0).astype(_MXU_DTYPE)
    thr_wide = jnp.broadcast_to(thr_ref[...], (tq, tk))

    mx = jnp.full((N_HEADS * tq, tk), -jnp.inf, _F32)
    for c in range(nchunks):
        keys_t = kbuf[slot, :, c * tk:(c + 1) * tk].astype(_MXU_DTYPE)
        lg = jnp.dot(q8, keys_t, preferred_element_type=_F32)
        sel = s_ref[0, c] >= thr_wide
        lg = jnp.concatenate(
            [jnp.where(sel, lg[h * tq:(h + 1) * tq, :], _NEG_INF) for h in range(N_HEADS)], axis=0)
        l_ref[c] = lg
        mx = jnp.maximum(mx, lg)
    m_wide = jnp.broadcast_to(jnp.max(mx, axis=1, keepdims=True), (N_HEADS * tq, tk))

    nt_dims = (((1,), (1,)), ((), ()))
    psum = jnp.zeros((N_HEADS * tq, tk), _F32)
    acc = jnp.zeros((N_HEADS * tq, KV_W), _F32)
    for c in range(nchunks):
        p = jnp.exp(l_ref[c] - m_wide)
        psum = psum + p
        vals_t = vbuf[slot, :, c * tk:(c + 1) * tk].astype(_MXU_DTYPE)
        acc = acc + lax.dot_general(p.astype(_MXU_DTYPE), vals_t, nt_dims,
                                    preferred_element_type=_F32)
    o = acc / jnp.sum(psum, axis=1, keepdims=True)
    blocks = []
    for j in range(N_HEADS // 2):
        blocks.append(jnp.where(lo_half, o[(2 * j) * tq:(2 * j + 1) * tq, :],
                                o[(2 * j + 1) * tq:(2 * j + 2) * tq, :]))
    out_ref[0] = jnp.concatenate(blocks, axis=1) * zbs_ref[0]


def _attn_sample(page_table, qs, qi, wi, zbs, k_new, v_new, ki_new, cache_kt, cache_vt, cache_it,
                 *, tk):
    nb, npages = page_table.shape
    tq = qs.shape[0] // nb
    past = npages * PAGE_SIZE
    nchunks = -(-(past + PAGE_SIZE) // tk)
    keys = nchunks * tk
    k_sel = min(TOP_K_MAX, (past + tq) // 4)
    group = LANES // tq
    assert LANES % tq == 0 and nb % group == 0
    three = lambda a: a.reshape(nb, tq, a.shape[-1])
    blk = lambda w: pl.BlockSpec((1, tq, w), lambda b, pt: (b, 0, 0))
    hbm = pl.BlockSpec(memory_space=pl.ANY)
    params = pltpu.CompilerParams(dimension_semantics=("arbitrary",),
                                  vmem_limit_bytes=VMEM_LIMIT_BYTES)
    pt_flat = page_table.reshape(-1)
    s_shape = jax.ShapeDtypeStruct((nb // group, nchunks, LANES, tk), _F32)
    s_blk = pl.BlockSpec((1, nchunks, tq, tk), lambda b, pt: (b // group, 0, b % group, 0))

    scores = pl.pallas_call(
        functools.partial(_sample_scores_kernel, npages=npages, tk=tk, nchunks=nchunks),
        grid_spec=pltpu.PrefetchScalarGridSpec(
            num_scalar_prefetch=1, grid=(nb,),
            in_specs=[blk(N_IDX_HEADS * D_IDX), blk(LANES), blk(D_IDX), hbm],
            out_specs=s_blk,
            scratch_shapes=[pltpu.VMEM((2, D_IDX, keys), cache_it.dtype),
                            pltpu.SemaphoreType.DMA((1, 2))]),
        out_shape=s_shape, compiler_params=params, name="sample_scores",
    )(pt_flat, three(qi), three(wi), three(ki_new), cache_it)

    n_adm = past + 1 + (jnp.arange(LANES) % tq)
    whole = pl.BlockSpec((1, nchunks, LANES, tk), lambda g: (g, 0, 0, 0))
    col = pl.BlockSpec((LANES, 1), lambda g: (0, 0))
    kept, thr = pl.pallas_call(
        functools.partial(_sample_threshold_kernel, tk=tk, nchunks=nchunks, k_sel=k_sel),
        grid=(nb // group,),
        in_specs=[whole, col, col],
        out_specs=(whole, pl.BlockSpec((LANES, 1), lambda g: (g, 0))),
        out_shape=(s_shape, jax.ShapeDtypeStruct((nb * tq, 1), _F32)),
        compiler_params=params, name="sample_threshold",
    )(scores, _seed_fraction(n_adm, k_sel).reshape(LANES, 1),
      n_adm.astype(_F32).reshape(LANES, 1))

    out = pl.pallas_call(
        functools.partial(_sample_attend_kernel, npages=npages, tk=tk, nchunks=nchunks),
        grid_spec=pltpu.PrefetchScalarGridSpec(
            num_scalar_prefetch=1, grid=(nb,),
            in_specs=[blk(D_B), blk(D_B), blk(KV_W), blk(KV_W), s_blk,
                      pl.BlockSpec((tq, 1), lambda b, pt: (b, 0)), hbm, hbm],
            out_specs=blk(D_B),
            scratch_shapes=[pltpu.VMEM((2, KV_W, keys), cache_kt.dtype),
                            pltpu.VMEM((2, KV_W, keys), cache_vt.dtype),
                            pltpu.SemaphoreType.DMA((2, 2)),
                            pltpu.VMEM((nchunks, N_HEADS * tq, tk), _F32)]),
        out_shape=jax.ShapeDtypeStruct((nb, tq, D_B), _F32),
        compiler_params=params, name="sample_attend",
    )(pt_flat, three(qs), three(zbs), three(k_new), three(v_new), kept, thr, cache_kt, cache_vt)
    return out.reshape(nb * tq, D_B)


def _outproj_kernel(yag_ref, ybg_ref, x_ref, w_ref, g_ref, out_ref):
    y = (jnp.dot(yag_ref[...], w_ref[:D_A, :], preferred_element_type=_F32)
         + jnp.dot(ybg_ref[...].astype(_MXU_DTYPE), w_ref[D_A:, :], preferred_element_type=_F32))
    ms = jnp.mean(y * y, axis=-1, keepdims=True)
    out_ref[...] = x_ref[...] + y * lax.rsqrt(ms + EPS) * g_ref[...]


def _outproj(yag, ybg, x2d, w_out_p, g_post, *, tm):
    r = x2d.shape[0]
    row = lambda w: pl.BlockSpec((tm, w), lambda i: (i, 0))
    const = lambda shape: pl.BlockSpec(shape, lambda i: (0,) * len(shape))
    return pl.pallas_call(
        _outproj_kernel,
        grid=(r // tm,),
        in_specs=[row(D_A), row(D_B), row(D_MODEL), const((D_A + D_B, D_MODEL)),
                  const((1, D_MODEL))],
        out_specs=row(D_MODEL),
        out_shape=jax.ShapeDtypeStruct((r, D_MODEL), _F32),
        compiler_params=pltpu.CompilerParams(
            dimension_semantics=("arbitrary",), vmem_limit_bytes=VMEM_LIMIT_BYTES),
        name="outproj",
    )(yag, ybg, x2d, w_out_p, g_post)


def _pad_in_weights(w):
    perm = _head_perm()
    nat = np.cumsum([0, D_A, D_A, D_A, D_B, KV_W, KV_W, D_B, N_IDX_HEADS * D_IDX, D_IDX,
                     N_IDX_HEADS])
    seg = lambda n: w[:, nat[n]:nat[n + 1]]
    pad = jnp.zeros((w.shape[0], C_END - C_WI - N_IDX_HEADS), w.dtype)
    return jnp.concatenate(
        [seg(0), seg(1), seg(2), seg(3)[:, perm], seg(4), seg(5), seg(6)[:, perm], seg(7),
         seg(8), seg(8), seg(9), pad], axis=1).astype(_MXU_DTYPE)


def _layer(x_prompt, x_sample, cache_k, cache_v, cache_idx_k, page_table, g_pre, w_in, ln_v_g,
           ln_v_b, w_s, b_s, w_out, g_post):
    nb, t, _ = x_prompt.shape
    nd, td, _ = x_sample.shape
    npages = page_table.shape[1]
    past = npages * PAGE_SIZE
    tm = 1024
    tq = 128
    tk = 1024
    tk_sample = 512

    w_pad = _pad_in_weights(w_in)
    perm = _head_perm()
    w_out_p = jnp.concatenate([w_out[:D_A], w_out[D_A:][perm]], axis=0).astype(_MXU_DTYPE)
    gpre = g_pre.reshape(1, D_MODEL)
    gpost = g_post.reshape(1, D_MODEL)
    lng = ln_v_g.reshape(1, D_A)
    lnb = ln_v_b.reshape(1, D_A)
    tril = jnp.tril(jnp.ones((CHUNK, CHUNK), dtype=bool))
    ws_tril = jnp.where(tril[None], w_s, jnp.zeros_like(w_s))

    xp = x_prompt.reshape(nb * t, D_MODEL)
    tpb = t // tm
    wmix_p = ws_tril.astype(_MXU_DTYPE)
    bmix_p = jnp.repeat(b_s.T, A_GROUP_DIM, axis=1)
    outs = _inproj(xp, _rope_tables(jnp.arange(t, dtype=jnp.int32)), lambda i: (i % tpb, 0),
                   gpre, w_pad, lng, lnb, wmix_p, bmix_p, tm=tm, state_rows=CHUNK,
                   state_shape=(nb, CHUNK, D_A), state_index=lambda i: (i // tpb, 0, 0), seqs=nb)
    yag, zbs, qs, qi, k_t, v_t, ki_t, wi, kbf, vt3, kibf, vstate_p = outs
    ybg = _attn_prompt(qs, qi, wi, zbs, kibf, kbf, vt3, nb=nb, t=t, tq=tq, tk=tk)
    y_prompt = _outproj(yag, ybg, xp, w_out_p, gpost, tm=tm).reshape(nb, t, D_MODEL)

    xs = x_sample.reshape(nd * td, D_MODEL)
    reps = CHUNK // td
    eye = jnp.eye(reps, dtype=w_s.dtype)
    wmix_s = jnp.stack([jnp.kron(eye, ws_tril[g, :td, :td]) for g in range(N_A_GROUPS)]
                       ).astype(_MXU_DTYPE)
    bmix_s = jnp.repeat(jnp.tile(b_s[:, :td], (1, reps)).T, A_GROUP_DIM, axis=1)
    tm_s = min(tm, nd * td)
    pos_s = past + (jnp.arange(tm_s, dtype=jnp.int32) % td)
    outs = _inproj(xs, _rope_tables(pos_s), lambda i: (0, 0), gpre, w_pad, lng, lnb, wmix_s,
                   bmix_s, tm=tm_s, state_rows=tm_s, state_shape=(nd * td, D_A),
                   state_index=lambda i: (i, 0))
    yag_s, zbs_s, qs_s, qi_s, k32_s, v32_s, ki32_s, wi_s, _, _, _, vstate_s = outs
    pool = cache_k.shape[0]
    ybg_s = _attn_sample(page_table, qs_s, qi_s, wi_s, zbs_s, k32_s, v32_s, ki32_s,
                         jnp.transpose(cache_k, (0, 2, 3, 1)).reshape(pool, KV_W, PAGE_SIZE),
                         jnp.transpose(cache_v, (0, 2, 3, 1)).reshape(pool, KV_W, PAGE_SIZE),
                         jnp.transpose(cache_idx_k, (0, 2, 1)), tk=tk_sample)
    y_sample = _outproj(yag_s, ybg_s, xs, w_out_p, gpost, tm=tm_s).reshape(nd, td, D_MODEL)

    return (y_prompt, y_sample,
            jnp.transpose(k_t.reshape(nb, N_KV_HEADS, HEAD_DIM, t), (0, 3, 1, 2)),
            jnp.transpose(v_t.reshape(nb, N_KV_HEADS, HEAD_DIM, t), (0, 3, 1, 2)),
            jnp.transpose(ki_t, (0, 2, 1)), vstate_p,
            k32_s.reshape(nd, td, N_KV_HEADS, HEAD_DIM), v32_s.reshape(nd, td, N_KV_HEADS, HEAD_DIM),
            ki32_s.reshape(nd, td, D_IDX), vstate_s.reshape(nd, td, D_A))


def kernel(x_prompt, x_sample, cache_k, cache_v, cache_idx_k, page_table, g_pre, w_in, ln_v_g,
           ln_v_b, w_s, b_s, w_out, g_post):
    xp, xs = x_prompt, x_sample
    per_layer = []
    for l in range(g_pre.shape[0]):
        outs = _layer(xp, xs, cache_k[l], cache_v[l], cache_idx_k[l], page_table, g_pre[l],
                      w_in[l], ln_v_g[l], ln_v_b[l], w_s[l], b_s[l], w_out[l], g_post[l])
        xp, xs = outs[0], outs[1]
        per_layer.append(outs[2:])
    stacked = tuple(jnp.stack(leaves, 0) for leaves in zip(*per_layer))
    return (xp, xs) + stacked
```
